```python
import jax, jax.numpy as jnp
from jax import lax
import numpy as np

D_MODEL = 1024
BATCH = 8
SEQ = 2048
DEPTH = 1
DEC_BATCH = 16
DEC_SEQ = 64
PAST_LEN = 1024

CHUNK = 64
N_HEADS = 16
HEAD_DIM = 64
ATTN_DIM = N_HEADS * HEAD_DIM
CONV_DIM = 1024
CONV_WIDTH = 3
D_FF = 2816
Q_BLOCK = 128
N_MOD = 9
EPS = 1e-6

OFF_Q = 0
OFF_K = OFF_Q + ATTN_DIM
OFF_V = OFF_K + ATTN_DIM
OFF_F = OFF_V + ATTN_DIM
OFF_B = OFF_F + N_HEADS
OFF_C = OFF_B + CONV_DIM
OFF_X = OFF_C + CONV_DIM
OFF_GA = OFF_X + CONV_DIM
OFF_GC = OFF_GA + ATTN_DIM
IN_COLS = OFF_GC + CONV_DIM
MIX_DIM = ATTN_DIM

kernel_name = "fox_shortconv_macaron_adaln_stream"


def rmsnorm(x, g):
    xf = x.astype(jnp.float32)
    y = xf * lax.rsqrt(jnp.mean(xf * xf, axis=-1, keepdims=True) + EPS)
    return (y * g.astype(jnp.float32)).astype(x.dtype)


def modulate(h, shift, scale):
    return h * (1 + scale[:, None, :]) + shift[:, None, :]


def swiglu(h, w_in, w_out):
    a, b = jnp.split(h @ w_in, 2, axis=-1)
    return (jax.nn.silu(a) * b) @ w_out


def fox_attention(q, k, v, logf, past_k, past_v, past_logf):
    B, H, T, _ = q.shape
    P = past_k.shape[2]
    k_all = jnp.concatenate([past_k.astype(k.dtype), k], axis=2)
    v_all = jnp.concatenate([past_v.astype(v.dtype), v], axis=2)
    F = jnp.cumsum(jnp.concatenate([past_logf.astype(jnp.float32), logf], axis=-1), axis=-1)
    k_pos = jnp.arange(P + T)
    q_pos = P + jnp.arange(T)
    Fq = F[..., P:]
    scale = HEAD_DIM ** -0.5

    def attend(args):
        qb, Fqb, qpb = args
        s = jnp.einsum('bhqd,bhkd->bhqk', qb, k_all, preferred_element_type=jnp.float32) * scale
        s = s + (Fqb[..., :, None] - F[..., None, :])
        s = jnp.where(k_pos[None, :] <= qpb[:, None], s, -jnp.inf)
        p = jax.nn.softmax(s, axis=-1)
        return jnp.einsum('bhqk,bhkd->bhqd', p.astype(v_all.dtype), v_all)

    if T <= Q_BLOCK:
        return attend((q, Fq, q_pos))
    nb = T // Q_BLOCK
    qb = q.reshape(B, H, nb, Q_BLOCK, HEAD_DIM).transpose(2, 0, 1, 3, 4)
    Fqb = Fq.reshape(B, H, nb, Q_BLOCK).transpose(2, 0, 1, 3)
    qpb = q_pos.reshape(nb, Q_BLOCK)
    o = lax.map(attend, (qb, Fqb, qpb))
    return o.transpose(1, 2, 0, 3, 4).reshape(B, H, T, HEAD_DIM)


def short_conv(u, past_u, w):
    T = u.shape[1]
    up = jnp.concatenate([past_u.astype(u.dtype), u], axis=1)
    y = w[0] * up[:, 0:T]
    for j in range(1, CONV_WIDTH):
        y = y + w[j] * up[:, j:j + T]
    return y, up[:, -(CONV_WIDTH - 1):]


def encoder_layer(x, c, past_k, past_v, past_logf, past_u,
                  w_ada, b_ada, g_ffn1, w_ffn1_in, w_ffn1_out, g_mix, w_in, b_f,
                  conv_w, w_out, g_ffn2, w_ffn2_in, w_ffn2_out):
    B, T, _ = x.shape
    mod = jax.nn.silu(c) @ w_ada + b_ada
    sh1, sc1, gt1, sh2, sc2, gt2, sh3, sc3, gt3 = jnp.split(mod, N_MOD, axis=-1)

    h = modulate(rmsnorm(x, g_ffn1), sh1, sc1)
    x = x + 0.5 * gt1[:, None, :] * swiglu(h, w_ffn1_in, w_ffn1_out)

    h = modulate(rmsnorm(x, g_mix), sh2, sc2)
    z = h @ w_in

    def heads(a):
        return a.reshape(B, T, N_HEADS, HEAD_DIM).transpose(0, 2, 1, 3)

    q = heads(z[..., OFF_Q:OFF_K])
    k = heads(z[..., OFF_K:OFF_V])
    v = heads(z[..., OFF_V:OFF_F])
    logf = jax.nn.log_sigmoid((z[..., OFF_F:OFF_B] + b_f).astype(jnp.float32)).transpose(0, 2, 1)
    o_attn = fox_attention(q, k, v, logf, past_k, past_v, past_logf)
    o_attn = o_attn.transpose(0, 2, 1, 3).reshape(B, T, ATTN_DIM)

    u = z[..., OFF_C:OFF_X] * z[..., OFF_X:OFF_GA]
    conv_y, new_u = short_conv(u, past_u, conv_w)
    o_conv = z[..., OFF_B:OFF_C] * conv_y

    m = (jax.nn.sigmoid(z[..., OFF_GA:OFF_GC]) * o_attn
         + jax.nn.sigmoid(z[..., OFF_GC:IN_COLS]) * o_conv)
    x = x + gt2[:, None, :] * (m @ w_out)

    h = modulate(rmsnorm(x, g_ffn2), sh3, sc3)
    x = x + 0.5 * gt3[:, None, :] * swiglu(h, w_ffn2_in, w_ffn2_out)
    return x, k, v, logf, new_u


def setup_inputs(seed: int = 0) -> dict:
    key = jax.random.key(seed)
    ks = jax.random.split(key, 24)
    f32 = jnp.float32
    nrm = lambda k, s, sc: jax.random.normal(k, s, f32) * sc
    gain = lambda k, s: 1.0 + 0.05 * jax.random.normal(k, s, f32)
    return {
        "x_prompt": nrm(ks[0], (BATCH, SEQ, D_MODEL), 1.0),
        "x_sample": nrm(ks[1], (DEC_BATCH, DEC_SEQ, D_MODEL), 1.0),
        "cache_k": nrm(ks[2], (DEPTH, DEC_BATCH, N_HEADS, PAST_LEN, HEAD_DIM), 1.0),
        "cache_v": nrm(ks[3], (DEPTH, DEC_BATCH, N_HEADS, PAST_LEN, HEAD_DIM), 1.0),
        "cache_logf": jax.nn.log_sigmoid(2.5 + jax.random.normal(ks[4], (DEPTH, DEC_BATCH, N_HEADS, PAST_LEN), f32)),
        "state_conv": nrm(ks[5], (DEPTH, DEC_BATCH, CONV_WIDTH - 1, CONV_DIM), 1.0),
        "c_prompt": nrm(ks[6], (BATCH, D_MODEL), 1.0),
        "c_sample": nrm(ks[7], (DEC_BATCH, D_MODEL), 1.0),
        "w_ada": nrm(ks[8], (DEPTH, D_MODEL, N_MOD * D_MODEL), D_MODEL ** -0.5),
        "b_ada": nrm(ks[9], (DEPTH, N_MOD * D_MODEL), 0.02),
        "g_ffn1": gain(ks[10], (DEPTH, D_MODEL)),
        "w_ffn1_in": nrm(ks[11], (DEPTH, D_MODEL, 2 * D_FF), D_MODEL ** -0.5),
        "w_ffn1_out": nrm(ks[12], (DEPTH, D_FF, D_MODEL), D_FF ** -0.5),
        "g_mix": gain(ks[13], (DEPTH, D_MODEL)),
        "w_in": nrm(ks[14], (DEPTH, D_MODEL, IN_COLS), D_MODEL ** -0.5),
        "b_f": jax.random.uniform(ks[15], (DEPTH, N_HEADS), f32, 1.0, 4.0),
        "conv_w": nrm(ks[16], (DEPTH, CONV_WIDTH, CONV_DIM), CONV_WIDTH ** -0.5),
        "w_out": nrm(ks[17], (DEPTH, MIX_DIM, D_MODEL), MIX_DIM ** -0.5),
        "g_ffn2": gain(ks[18], (DEPTH, D_MODEL)),
        "w_ffn2_in": nrm(ks[19], (DEPTH, D_MODEL, 2 * D_FF), D_MODEL ** -0.5),
        "w_ffn2_out": nrm(ks[20], (DEPTH, D_FF, D_MODEL), D_FF ** -0.5),
        "g_final": gain(ks[21], (D_MODEL,)),
    }


def reference(x_prompt, x_sample, cache_k, cache_v, cache_logf, state_conv, c_prompt, c_sample,
              w_ada, b_ada, g_ffn1, w_ffn1_in, w_ffn1_out, g_mix, w_in, b_f, conv_w, w_out,
              g_ffn2, w_ffn2_in, w_ffn2_out, g_final):
    Bp = x_prompt.shape[0]
    empty_kv = jnp.zeros((Bp, N_HEADS, 0, HEAD_DIM), x_prompt.dtype)
    empty_f = jnp.zeros((Bp, N_HEADS, 0), jnp.float32)
    zero_u = jnp.zeros((Bp, CONV_WIDTH - 1, CONV_DIM), x_prompt.dtype)
    xp, xs = x_prompt, x_sample
    kp, vp, fp, up, ksm, vsm, fsm, usm = [], [], [], [], [], [], [], []
    for l in range(DEPTH):
        w = (w_ada[l], b_ada[l], g_ffn1[l], w_ffn1_in[l], w_ffn1_out[l], g_mix[l], w_in[l], b_f[l],
             conv_w[l], w_out[l], g_ffn2[l], w_ffn2_in[l], w_ffn2_out[l])
        xp, k_, v_, f_, u_ = encoder_layer(xp, c_prompt, empty_kv, empty_kv, empty_f, zero_u, *w)
        kp.append(k_); vp.append(v_); fp.append(f_); up.append(u_)
        xs, k_, v_, f_, u_ = encoder_layer(xs, c_sample, cache_k[l], cache_v[l], cache_logf[l],
                                           state_conv[l], *w)
        ksm.append(k_); vsm.append(v_); fsm.append(f_); usm.append(u_)
    y_prompt = rmsnorm(xp, g_final)
    y_sample = rmsnorm(xs, g_final)
    return (y_prompt, y_sample,
            jnp.stack(kp), jnp.stack(vp), jnp.stack(fp), jnp.stack(up),
            jnp.stack(ksm), jnp.stack(vsm), jnp.stack(fsm), jnp.stack(usm))
```

```python
import functools

import jax
import jax.numpy as jnp
from jax import lax
from jax.experimental import pallas as pl
from jax.experimental.pallas import tpu as pltpu

F32 = jnp.float32
BF16 = jnp.bfloat16

EPS = 1e-6
N_HEADS = 16
HEAD_DIM = 64
N_MOD = 9
CONV_WIDTH = 3
LANES = 128
HEADS_PER_BLOCK = LANES // HEAD_DIM
V7X_VMEM_LIMIT_BYTES = 56 * 1024 * 1024
FFN_CHUNK = 256
ROW_TILE = 512
ATTN_BLOCK = 256


def _params(semantics):
    return pltpu.CompilerParams(dimension_semantics=semantics, vmem_limit_bytes=V7X_VMEM_LIMIT_BYTES)


def _resident(shape):
    zeros = (0,) * len(shape)
    return pl.BlockSpec(shape, lambda *_: zeros, pipeline_mode=pl.Buffered(1))


def _rms_mod(x, g, scale, shift):
    ms = jnp.mean(x * x, axis=-1, keepdims=True)
    y = x * lax.rsqrt(ms + EPS) * g
    return y * (1.0 + scale) + shift


def _sigmoid(x):
    return 1.0 / (1.0 + jnp.exp(-x))


def _log_sigmoid(x):
    return jnp.minimum(x, 0.0) - jnp.log1p(jnp.exp(-jnp.abs(x)))


def _cumsum_lanes(x):
    n = x.shape[-1]
    lane = lax.broadcasted_iota(jnp.int32, x.shape, x.ndim - 1)
    step = 1
    while step < n:
        x = x + jnp.where(lane >= step, pltpu.roll(x, step, axis=x.ndim - 1), 0.0)
        step *= 2
    return x


def _ada_kernel(c_ref, w_ref, b_ref, o_ref):
    c = c_ref[...]
    a = (c * _sigmoid(c)).astype(BF16)
    o_ref[...] = jnp.dot(a, w_ref[...].astype(BF16), preferred_element_type=F32) + b_ref[...]


def _ada(c, w_ada, b_ada):
    n, d = c.shape
    cols = w_ada.shape[1]
    tn = d
    return pl.pallas_call(
        _ada_kernel,
        grid=(cols // tn,),
        in_specs=[
            pl.BlockSpec((n, d), lambda j: (0, 0)),
            pl.BlockSpec((d, tn), lambda j: (0, j)),
            pl.BlockSpec((1, tn), lambda j: (0, j)),
        ],
        out_specs=pl.BlockSpec((n, tn), lambda j: (0, j)),
        out_shape=jax.ShapeDtypeStruct((n, cols), F32),
        compiler_params=_params(("parallel",)),
        name="ada_mod",
    )(c, w_ada, b_ada.reshape(1, cols))


def _ffn_kernel(x_ref, mod_ref, g_ref, w1_ref, w2_ref, gfin_ref, o_ref, h_scr, acc_scr, *, mod_base, final_norm):
    bb, tt, d = x_ref.shape
    tm = bb * tt
    nc, fc, _ = w2_ref.shape
    x = x_ref[...]
    shift = mod_ref[:, mod_base:mod_base + 1, :]
    scale = mod_ref[:, mod_base + 1:mod_base + 2, :]
    gate = mod_ref[:, mod_base + 2:mod_base + 3, :]
    h_scr[...] = _rms_mod(x, g_ref[...], scale, shift).reshape(tm, d).astype(BF16)
    acc_scr[...] = jnp.zeros_like(acc_scr)

    def chunk(c, carry):
        ab = jnp.dot(h_scr[...], w1_ref[c], preferred_element_type=F32)
        a = ab[:, :fc]
        b = ab[:, fc:]
        act = (a * _sigmoid(a) * b).astype(BF16)
        acc_scr[...] += jnp.dot(act, w2_ref[c], preferred_element_type=F32)
        return carry

    lax.fori_loop(0, nc, chunk, 0)
    y = x + 0.5 * gate * acc_scr[...].reshape(bb, tt, d)
    if final_norm:
        ms = jnp.mean(y * y, axis=-1, keepdims=True)
        y = y * lax.rsqrt(ms + EPS) * gfin_ref[...]
    o_ref[...] = y


def _ffn(x, mod, g, w1c, w2c, g_final, *, bb, tt, mod_base, final_norm):
    B, T, D = x.shape
    nc, fc, _ = w2c.shape
    tm = bb * tt
    kern = functools.partial(_ffn_kernel, mod_base=mod_base, final_norm=final_norm)
    return pl.pallas_call(
        kern,
        grid=(B // bb, T // tt),
        in_specs=[
            pl.BlockSpec((bb, tt, D), lambda b, t: (b, t, 0)),
            pl.BlockSpec((bb, N_MOD, D), lambda b, t: (b, 0, 0)),
            _resident((1, D)),
            _resident(w1c.shape),
            _resident(w2c.shape),
            _resident((1, D)),
        ],
        out_specs=pl.BlockSpec((bb, tt, D), lambda b, t: (b, t, 0)),
        out_shape=jax.ShapeDtypeStruct((B, T, D), F32),
        scratch_shapes=[pltpu.VMEM((tm, D), BF16), pltpu.VMEM((tm, D), F32)],
        compiler_params=_params(("parallel", "parallel")),
        name="ffn_final" if final_norm else "ffn",
    )(x, mod, g, w1c, w2c, g_final)


def _qkv_kernel(x_ref, mod_ref, g_ref, wqkv_ref, wf_ref, bf_ref,
                q_ref, kb_ref, vb_ref, k32_ref, v32_ref, lf_ref):
    bb, tt, d = x_ref.shape
    tm = bb * tt
    a_dim = wqkv_ref.shape[2]
    shift = mod_ref[:, 3:4, :]
    scale = mod_ref[:, 4:5, :]
    h = _rms_mod(x_ref[...], g_ref[...], scale, shift).reshape(tm, d).astype(BF16)

    zq = jnp.dot(h, wqkv_ref[0], preferred_element_type=F32)
    q_ref[...] = (zq * (HEAD_DIM ** -0.5)).astype(BF16).reshape(bb, tt, a_dim)

    for w_idx, (lo_ref, hi_ref) in ((1, (kb_ref, k32_ref)), (2, (vb_ref, v32_ref))):
        z = jnp.dot(h, wqkv_ref[w_idx], preferred_element_type=F32)
        lo_ref[...] = z.astype(BF16).reshape(bb, tt, a_dim)
        for hd in range(N_HEADS):
            hi_ref[:, hd, :, :] = z[:, hd * HEAD_DIM:(hd + 1) * HEAD_DIM].reshape(bb, tt, HEAD_DIM)

    zf = jnp.dot(h, wf_ref[...], preferred_element_type=F32)
    lf = _log_sigmoid(zf + bf_ref[...]).T
    for b in range(bb):
        lf_ref[b] = lf[:N_HEADS, b * tt:(b + 1) * tt]


def _qkv(x, mod, g, wqkv, wf, bf, *, bb, tt):
    B, T, D = x.shape
    A = wqkv.shape[2]
    tok = lambda b, t: (b, t, 0)
    head = lambda b, t: (b, 0, t, 0)
    return pl.pallas_call(
        _qkv_kernel,
        grid=(B // bb, T // tt),
        in_specs=[
            pl.BlockSpec((bb, tt, D), tok),
            pl.BlockSpec((bb, N_MOD, D), lambda b, t: (b, 0, 0)),
            _resident((1, D)),
            _resident(wqkv.shape),
            _resident(wf.shape),
            _resident(bf.shape),
        ],
        out_specs=[
            pl.BlockSpec((bb, tt, A), tok),
            pl.BlockSpec((bb, tt, A), tok),
            pl.BlockSpec((bb, tt, A), tok),
            pl.BlockSpec((bb, N_HEADS, tt, HEAD_DIM), head),
            pl.BlockSpec((bb, N_HEADS, tt, HEAD_DIM), head),
            pl.BlockSpec((bb, N_HEADS, tt), lambda b, t: (b, 0, t)),
        ],
        out_shape=[
            jax.ShapeDtypeStruct((B, T, A), BF16),
            jax.ShapeDtypeStruct((B, T, A), BF16),
            jax.ShapeDtypeStruct((B, T, A), BF16),
            jax.ShapeDtypeStruct((B, N_HEADS, T, HEAD_DIM), F32),
            jax.ShapeDtypeStruct((B, N_HEADS, T, HEAD_DIM), F32),
            jax.ShapeDtypeStruct((B, N_HEADS, T), F32),
        ],
        compiler_params=_params(("parallel", "parallel")),
        name="qkv_proj",
    )(x, mod, g, wqkv, wf, bf)


def _attn_prompt_kernel(q_ref, k_ref, v_ref, lf_ref, o_ref, negf_scr, m_scr, l_scr, acc_scr, *, blk):
    qi = pl.program_id(2)
    T = k_ref.shape[1]
    nkb = T // blk

    @pl.when(qi == 0)
    def _():
        f = _cumsum_lanes(lf_ref[0, 0])
        for c in range(nkb):
            negf_scr[c] = -f[:, c * blk:(c + 1) * blk]

    q = q_ref[0]
    lane = lax.broadcasted_iota(jnp.int32, q.shape, 1)
    zero = jnp.zeros_like(q)
    q2 = jnp.concatenate([jnp.where(lane < HEAD_DIM, q, zero), jnp.where(lane >= HEAD_DIM, q, zero)], axis=0)
    m_scr[...] = jnp.full_like(m_scr, -jnp.inf)
    l_scr[...] = jnp.zeros_like(l_scr)
    acc_scr[...] = jnp.zeros_like(acc_scr)

    def step(kb, masked):
        k = k_ref[0, pl.ds(kb * blk, blk), :]
        v = v_ref[0, pl.ds(kb * blk, blk), :]
        s = lax.dot_general(q2, k, (((1,), (1,)), ((), ())), preferred_element_type=F32)
        nf = negf_scr[kb]
        s = jnp.concatenate([s[:blk] + nf[0:1], s[blk:] + nf[1:2]], axis=0)
        if masked:
            row = lax.broadcasted_iota(jnp.int32, (blk, blk), 0)
            col = lax.broadcasted_iota(jnp.int32, (blk, blk), 1)
            keep = jnp.concatenate([col <= row, col <= row], axis=0)
            s = jnp.where(keep, s, -jnp.inf)
        m_old = m_scr[...]
        m_new = jnp.maximum(m_old, jnp.max(s, axis=-1, keepdims=True))
        alpha = jnp.exp(m_old - m_new)
        p = jnp.exp(s - m_new)
        l_scr[...] = alpha * l_scr[...] + jnp.sum(p, axis=-1, keepdims=True)
        acc_scr[...] = alpha * acc_scr[...] + jnp.dot(p.astype(BF16), v, preferred_element_type=F32)
        m_scr[...] = m_new

    def body(kb, carry):
        step(kb, False)
        return carry

    lax.fori_loop(0, qi, body, 0)
    step(qi, True)

    o = acc_scr[...] * (1.0 / l_scr[...])
    o_ref[0] = jnp.where(lane < HEAD_DIM, o[:blk], o[blk:]).astype(o_ref.dtype)


def _attn_prompt(q, k, v, logf, *, blk):
    B, T, A = q.shape
    npair = A // LANES
    lf = logf.reshape(B, npair, HEADS_PER_BLOCK, T)
    kern = functools.partial(_attn_prompt_kernel, blk=blk)
    return pl.pallas_call(
        kern,
        grid=(B, npair, T // blk),
        in_specs=[
            pl.BlockSpec((1, blk, LANES), lambda b, p, i: (b, i, p)),
            pl.BlockSpec((1, T, LANES), lambda b, p, i: (b, 0, p)),
            pl.BlockSpec((1, T, LANES), lambda b, p, i: (b, 0, p)),
            pl.BlockSpec((1, 1, HEADS_PER_BLOCK, T), lambda b, p, i: (b, p, 0, 0)),
        ],
        out_specs=pl.BlockSpec((1, blk, LANES), lambda b, p, i: (b, i, p)),
        out_shape=jax.ShapeDtypeStruct((B, T, A), BF16),
        scratch_shapes=[
            pltpu.VMEM((T // blk, HEADS_PER_BLOCK, blk), F32),
            pltpu.VMEM((HEADS_PER_BLOCK * blk, 1), F32),
            pltpu.VMEM((HEADS_PER_BLOCK * blk, 1), F32),
            pltpu.VMEM((HEADS_PER_BLOCK * blk, LANES), F32),
        ],
        compiler_params=_params(("parallel", "parallel", "arbitrary")),
        name="attn_prompt",
    )(q, k, v, lf)


def _attn_sample_kernel(q_ref, kn_ref, vn_ref, ck_ref, cv_ref, clf_ref, lf_ref, o_ref):
    tt = q_ref.shape[1]
    f_past = _cumsum_lanes(clf_ref[0, 0])
    f_new = f_past[:, -1:] + _cumsum_lanes(lf_ref[0, 0])
    row = lax.broadcasted_iota(jnp.int32, (tt, tt), 0)
    col = lax.broadcasted_iota(jnp.int32, (tt, tt), 1)
    outs = []
    for j in range(HEADS_PER_BLOCK):
        sl = slice(j * HEAD_DIM, (j + 1) * HEAD_DIM)
        qj = q_ref[0][:, sl]
        nt = (((1,), (1,)), ((), ()))
        s_p = lax.dot_general(qj, ck_ref[0, j].astype(BF16), nt, preferred_element_type=F32) - f_past[j:j + 1]
        s_n = lax.dot_general(qj, kn_ref[0][:, sl], nt, preferred_element_type=F32) - f_new[j:j + 1]
        s_n = jnp.where(col <= row, s_n, -jnp.inf)
        m = jnp.maximum(jnp.max(s_p, axis=-1, keepdims=True), jnp.max(s_n, axis=-1, keepdims=True))
        p_p = jnp.exp(s_p - m)
        p_n = jnp.exp(s_n - m)
        l = jnp.sum(p_p, axis=-1, keepdims=True) + jnp.sum(p_n, axis=-1, keepdims=True)
        o = (jnp.dot(p_p.astype(BF16), cv_ref[0, j].astype(BF16), preferred_element_type=F32)
             + jnp.dot(p_n.astype(BF16), vn_ref[0][:, sl], preferred_element_type=F32))
        outs.append(o * (1.0 / l))
    o_ref[0] = jnp.concatenate(outs, axis=-1).astype(o_ref.dtype)


def _attn_sample(q, k, v, cache_k, cache_v, cache_logf, logf):
    B, T, A = q.shape
    P = cache_k.shape[2]
    npair = A // LANES
    clf = cache_logf.reshape(B, npair, HEADS_PER_BLOCK, P)
    lf = logf.reshape(B, npair, HEADS_PER_BLOCK, T)
    tok = lambda b, p: (b, 0, p)
    return pl.pallas_call(
        _attn_sample_kernel,
        grid=(B, npair),
        in_specs=[
            pl.BlockSpec((1, T, LANES), tok),
            pl.BlockSpec((1, T, LANES), tok),
            pl.BlockSpec((1, T, LANES), tok),
            pl.BlockSpec((1, HEADS_PER_BLOCK, P, HEAD_DIM), lambda b, p: (b, p, 0, 0)),
            pl.BlockSpec((1, HEADS_PER_BLOCK, P, HEAD_DIM), lambda b, p: (b, p, 0, 0)),
            pl.BlockSpec((1, 1, HEADS_PER_BLOCK, P), lambda b, p: (b, p, 0, 0)),
            pl.BlockSpec((1, 1, HEADS_PER_BLOCK, T), lambda b, p: (b, p, 0, 0)),
        ],
        out_specs=pl.BlockSpec((1, T, LANES), tok),
        out_shape=jax.ShapeDtypeStruct((B, T, A), BF16),
        compiler_params=_params(("parallel", "parallel")),
        name="attn_sample",
    )(q, k, v, cache_k, cache_v, clf, lf)


def _mix_kernel(x_ref, o_ref, mod_ref, g_ref, w5_ref, cw_ref, pu_ref, wo_ref, y_ref, nu_ref, carry_scr):
    bb, tt, d = x_ref.shape
    tm = bb * tt
    t = pl.program_id(1)

    @pl.when(t == 0)
    def _():
        carry_scr[...] = pu_ref[...]

    x = x_ref[...]
    shift = mod_ref[:, 3:4, :]
    scale = mod_ref[:, 4:5, :]
    gate = mod_ref[:, 5:6, :]
    h = _rms_mod(x, g_ref[...], scale, shift).reshape(tm, d).astype(BF16)
    proj = lambda i: jnp.dot(h, w5_ref[i], preferred_element_type=F32)

    u2 = proj(1) * proj(2)
    cdim = u2.shape[1]
    u = u2.reshape(bb, tt, cdim)
    tpos = lax.broadcasted_iota(jnp.int32, (1, tt, 1), 1)
    c0 = carry_scr[:, 0:1, :]
    c1 = carry_scr[:, 1:2, :]
    um1 = jnp.where(tpos == 0, c1, pltpu.roll(u2, 1, axis=0).reshape(bb, tt, cdim))
    um2 = jnp.where(tpos == 0, c0, jnp.where(tpos == 1, c1, pltpu.roll(u2, 2, axis=0).reshape(bb, tt, cdim)))
    conv = cw_ref[0:1, :] * um2 + cw_ref[1:2, :] * um1 + cw_ref[2:3, :] * u
    new_u = u[:, tt - (CONV_WIDTH - 1):, :]
    carry_scr[...] = new_u
    nu_ref[...] = new_u

    o_conv = proj(0) * conv.reshape(tm, cdim)
    m = _sigmoid(proj(3)) * o_ref[...].reshape(tm, cdim).astype(F32) + _sigmoid(proj(4)) * o_conv
    mo = jnp.dot(m.astype(BF16), wo_ref[...], preferred_element_type=F32)
    y_ref[...] = x + gate * mo.reshape(bb, tt, d)


def _mix(x, o_attn, mod, g, w5, conv_w, past_u, w_out, *, bb, tt):
    B, T, D = x.shape
    C = w5.shape[2]
    tok = lambda b, t: (b, t, 0)
    per_b = lambda b, t: (b, 0, 0)
    return pl.pallas_call(
        _mix_kernel,
        grid=(B // bb, T // tt),
        in_specs=[
            pl.BlockSpec((bb, tt, D), tok),
            pl.BlockSpec((bb, tt, C), tok),
            pl.BlockSpec((bb, N_MOD, D), per_b),
            _resident((1, D)),
            _resident(w5.shape),
            _resident(conv_w.shape),
            pl.BlockSpec((bb, CONV_WIDTH - 1, C), per_b),
            _resident(w_out.shape),
        ],
        out_specs=[
            pl.BlockSpec((bb, tt, D), tok),
            pl.BlockSpec((bb, CONV_WIDTH - 1, C), per_b),
        ],
        out_shape=[
            jax.ShapeDtypeStruct((B, T, D), F32),
            jax.ShapeDtypeStruct((B, CONV_WIDTH - 1, C), F32),
        ],
        scratch_shapes=[pltpu.VMEM((bb, CONV_WIDTH - 1, C), F32)],
        compiler_params=_params(("parallel", "arbitrary")),
        name="mix",
    )(x, o_attn, mod, g, w5, conv_w, past_u, w_out)


def _ffn_weights(w_in, w_out):
    d, two_f = w_in.shape
    f = two_f // 2
    nc = f // FFN_CHUNK
    a = w_in[:, :f].reshape(d, nc, FFN_CHUNK)
    b = w_in[:, f:].reshape(d, nc, FFN_CHUNK)
    w1c = jnp.concatenate([a, b], axis=2).transpose(1, 0, 2).astype(BF16)
    w2c = w_out.reshape(nc, FFN_CHUNK, d).astype(BF16)
    return w1c, w2c


def _layer(x, mod, past, weights, g_final, *, bb, tt):
    (g1, w1a, w1b, gm, wqkv, wf, bf, w5, conv_w, w_out, g2, w2a, w2b) = weights
    past_k, past_v, past_logf, past_u = past
    x1 = _ffn(x, mod, g1, w1a, w1b, g_final, bb=bb, tt=tt, mod_base=0, final_norm=False)
    q, kb, vb, k32, v32, logf = _qkv(x1, mod, gm, wqkv, wf, bf, bb=bb, tt=tt)
    if past_k is None:
        o = _attn_prompt(q, kb, vb, logf, blk=ATTN_BLOCK)
    else:
        o = _attn_sample(q, kb, vb, past_k, past_v, past_logf, logf)
    x2, new_u = _mix(x1, o, mod, gm, w5, conv_w, past_u, w_out, bb=bb, tt=tt)
    y = _ffn(x2, mod, g2, w2a, w2b, g_final, bb=bb, tt=tt, mod_base=6, final_norm=True)
    return y, k32, v32, logf, new_u


def kernel(x_prompt, x_sample, cache_k, cache_v, cache_logf, state_conv, c_prompt, c_sample, w_ada, b_ada, g_ffn1, w_ffn1_in, w_ffn1_out, g_mix, w_in, b_f, conv_w, w_out, g_ffn2, w_ffn2_in, w_ffn2_out, g_final):
    assert w_ada.shape[0] == 1, "single-layer encoder"
    Bp, Tp, D = x_prompt.shape
    Bs, Ts, _ = x_sample.shape
    A = N_HEADS * HEAD_DIM
    C = conv_w.shape[2]

    mod = _ada(jnp.concatenate([c_prompt, c_sample], axis=0), w_ada[0], b_ada[0])
    mod = mod.reshape(Bp + Bs, N_MOD, D)
    mod_p, mod_s = mod[:Bp], mod[Bp:]

    wi = w_in[0]
    off_f = 3 * A
    off_b = off_f + N_HEADS
    wqkv = wi[:, :off_f].reshape(D, 3, A).transpose(1, 0, 2).astype(BF16)
    wf = jnp.pad(wi[:, off_f:off_b], ((0, 0), (0, LANES - N_HEADS))).astype(BF16)
    bf = jnp.pad(b_f[0], (0, LANES - N_HEADS)).reshape(1, LANES)
    w5 = wi[:, off_b:].reshape(D, 5, C).transpose(1, 0, 2).astype(BF16)
    w1a, w1b = _ffn_weights(w_ffn1_in[0], w_ffn1_out[0])
    w2a, w2b = _ffn_weights(w_ffn2_in[0], w_ffn2_out[0])
    weights = (g_ffn1, w1a, w1b, g_mix, wqkv, wf, bf, w5, conv_w[0], w_out[0].astype(BF16),
               g_ffn2, w2a, w2b)
    gfin = g_final.reshape(1, D)

    zero_u = jnp.zeros((Bp, CONV_WIDTH - 1, C), F32)
    yp, kp, vp, fp, up = _layer(x_prompt, mod_p, (None, None, None, zero_u), weights, gfin,
                                bb=1, tt=ROW_TILE)
    ys, ks, vs, fs, us = _layer(x_sample, mod_s, (cache_k[0], cache_v[0], cache_logf[0], state_conv[0]),
                                weights, gfin, bb=ROW_TILE // Ts, tt=Ts)
    return (yp, ys, kp[None], vp[None], fp[None], up[None], ks[None], vs[None], fs[None], us[None])
```

```python
import functools

import jax
import jax.numpy as jnp
from jax import lax
from jax.experimental import pallas as pl
from jax.experimental.pallas import tpu as pltpu

F32 = jnp.float32
BF16 = jnp.bfloat16

EPS = 1e-6
N_HEADS = 16
HEAD_DIM = 64
N_MOD = 9
CONV_WIDTH = 3
LANES = 128
HEADS_PER_BLOCK = LANES // HEAD_DIM
V7X_VMEM_LIMIT_BYTES = 56 * 1024 * 1024
FFN_CHUNK = 256
ROW_TILE = 512
ATTN_BLOCK = 256
SAMPLE_HEAD_GROUP = 8


def _params(semantics):
    return pltpu.CompilerParams(dimension_semantics=semantics, vmem_limit_bytes=V7X_VMEM_LIMIT_BYTES)


def _resident(shape):
    zeros = (0,) * len(shape)
    return pl.BlockSpec(shape, lambda *_: zeros, pipeline_mode=pl.Buffered(1))


def _rms_mod(x, g, scale, shift):
    ms = jnp.mean(x * x, axis=-1, keepdims=True)
    y = x * lax.rsqrt(ms + EPS) * g
    return y * (1.0 + scale) + shift


def _sigmoid(x):
    return 1.0 / (1.0 + jnp.exp(-x))


def _log_sigmoid(x):
    return jnp.minimum(x, 0.0) - jnp.log1p(jnp.exp(-jnp.abs(x)))


def _cumsum_lanes(x):
    n = x.shape[-1]
    lane = lax.broadcasted_iota(jnp.int32, x.shape, x.ndim - 1)
    step = 1
    while step < n:
        x = x + jnp.where(lane >= step, pltpu.roll(x, step, axis=x.ndim - 1), 0.0)
        step *= 2
    return x


def _ada_kernel(c_ref, w_ref, b_ref, o_ref):
    c = c_ref[...]
    a = (c * _sigmoid(c)).astype(BF16)
    o_ref[...] = jnp.dot(a, w_ref[...].astype(BF16), preferred_element_type=F32) + b_ref[...]


def _ada(c, w_ada, b_ada):
    n, d = c.shape
    cols = w_ada.shape[1]
    tn = d
    return pl.pallas_call(
        _ada_kernel,
        grid=(cols // tn,),
        in_specs=[
            pl.BlockSpec((n, d), lambda j: (0, 0)),
            pl.BlockSpec((d, tn), lambda j: (0, j)),
            pl.BlockSpec((1, tn), lambda j: (0, j)),
        ],
        out_specs=pl.BlockSpec((n, tn), lambda j: (0, j)),
        out_shape=jax.ShapeDtypeStruct((n, cols), F32),
        compiler_params=_params(("parallel",)),
        name="ada_mod",
    )(c, w_ada, b_ada.reshape(1, cols))


def _ffn_kernel(x_ref, mod_ref, g_ref, w1_ref, w2_ref, gfin_ref, o_ref, h_scr, *, mod_base, final_norm):
    bb, tt, d = x_ref.shape
    tm = bb * tt
    f = w2_ref.shape[0]
    fc = FFN_CHUNK
    x = x_ref[...]
    shift = mod_ref[:, mod_base:mod_base + 1, :]
    scale = mod_ref[:, mod_base + 1:mod_base + 2, :]
    gate = mod_ref[:, mod_base + 2:mod_base + 3, :]
    h_scr[...] = _rms_mod(x, g_ref[...], scale, shift).reshape(tm, d).astype(BF16)

    acc = None
    for lo in range(0, f, fc):
        h = h_scr[...]
        a = jnp.dot(h, w1_ref[:, lo:lo + fc], preferred_element_type=F32)
        b = jnp.dot(h, w1_ref[:, f + lo:f + lo + fc], preferred_element_type=F32)
        act = (a * _sigmoid(a) * b).astype(BF16)
        part = jnp.dot(act, w2_ref[lo:lo + fc, :], preferred_element_type=F32)
        acc = part if acc is None else acc + part

    y = x + 0.5 * gate * acc.reshape(bb, tt, d)
    if final_norm:
        ms = jnp.mean(y * y, axis=-1, keepdims=True)
        y = y * lax.rsqrt(ms + EPS) * gfin_ref[...]
    o_ref[...] = y


def _ffn(x, mod, g, w1c, w2c, g_final, *, bb, tt, mod_base, final_norm):
    B, T, D = x.shape
    assert w2c.shape[0] % FFN_CHUNK == 0
    tm = bb * tt
    kern = functools.partial(_ffn_kernel, mod_base=mod_base, final_norm=final_norm)
    return pl.pallas_call(
        kern,
        grid=(B // bb, T // tt),
        in_specs=[
            pl.BlockSpec((bb, tt, D), lambda b, t: (b, t, 0)),
            pl.BlockSpec((bb, N_MOD, D), lambda b, t: (b, 0, 0)),
            _resident((1, D)),
            _resident(w1c.shape),
            _resident(w2c.shape),
            _resident((1, D)),
        ],
        out_specs=pl.BlockSpec((bb, tt, D), lambda b, t: (b, t, 0)),
        out_shape=jax.ShapeDtypeStruct((B, T, D), F32),
        scratch_shapes=[pltpu.VMEM((tm, D), BF16)],
        compiler_params=_params(("parallel", "parallel")),
        name="ffn_final" if final_norm else "ffn",
    )(x, mod, g, w1c, w2c, g_final)


def _qkv_kernel(x_ref, mod_ref, g_ref, wqkv_ref, wf_ref, bf_ref,
                q_ref, kb_ref, vb_ref, k32_ref, v32_ref, lf_ref):
    bb, tt, d = x_ref.shape
    tm = bb * tt
    a_dim = wqkv_ref.shape[1] // 3
    shift = mod_ref[:, 3:4, :]
    scale = mod_ref[:, 4:5, :]
    h = _rms_mod(x_ref[...], g_ref[...], scale, shift).reshape(tm, d).astype(BF16)

    zq = jnp.dot(h, wqkv_ref[:, 0:a_dim], preferred_element_type=F32)
    q_ref[...] = (zq * (HEAD_DIM ** -0.5)).astype(BF16).reshape(bb, tt, a_dim)

    for w_idx, (lo_ref, hi_ref) in ((1, (kb_ref, k32_ref)), (2, (vb_ref, v32_ref))):
        z = jnp.dot(h, wqkv_ref[:, w_idx * a_dim:(w_idx + 1) * a_dim], preferred_element_type=F32)
        lo_ref[...] = z.astype(BF16).reshape(bb, tt, a_dim)
        for hd in range(N_HEADS):
            hi_ref[:, hd, :, :] = z[:, hd * HEAD_DIM:(hd + 1) * HEAD_DIM].reshape(bb, tt, HEAD_DIM)

    zf = jnp.dot(h, wf_ref[...], preferred_element_type=F32)
    lf = _log_sigmoid(zf + bf_ref[...]).T
    for b in range(bb):
        lf_ref[b] = lf[:N_HEADS, b * tt:(b + 1) * tt]


def _qkv(x, mod, g, wqkv, wf, bf, *, bb, tt):
    B, T, D = x.shape
    A = wqkv.shape[1] // 3
    tok = lambda b, t: (b, t, 0)
    head = lambda b, t: (b, 0, t, 0)
    return pl.pallas_call(
        _qkv_kernel,
        grid=(B // bb, T // tt),
        in_specs=[
            pl.BlockSpec((bb, tt, D), tok),
            pl.BlockSpec((bb, N_MOD, D), lambda b, t: (b, 0, 0)),
            _resident((1, D)),
            _resident(wqkv.shape),
            _resident(wf.shape),
            _resident(bf.shape),
        ],
        out_specs=[
            pl.BlockSpec((bb, tt, A), tok),
            pl.BlockSpec((bb, tt, A), tok),
            pl.BlockSpec((bb, tt, A), tok),
            pl.BlockSpec((bb, N_HEADS, tt, HEAD_DIM), head),
            pl.BlockSpec((bb, N_HEADS, tt, HEAD_DIM), head),
            pl.BlockSpec((bb, N_HEADS, tt), lambda b, t: (b, 0, t)),
        ],
        out_shape=[
            jax.ShapeDtypeStruct((B, T, A), BF16),
            jax.ShapeDtypeStruct((B, T, A), BF16),
            jax.ShapeDtypeStruct((B, T, A), BF16),
            jax.ShapeDtypeStruct((B, N_HEADS, T, HEAD_DIM), F32),
            jax.ShapeDtypeStruct((B, N_HEADS, T, HEAD_DIM), F32),
            jax.ShapeDtypeStruct((B, N_HEADS, T), F32),
        ],
        compiler_params=_params(("parallel", "parallel")),
        name="qkv_proj",
    )(x, mod, g, wqkv, wf, bf)


def _attn_rows(q2, k_ref, v_ref, negf, nk, blk):
    nt = (((1,), (1,)), ((), ()))
    row = lax.broadcasted_iota(jnp.int32, (blk, blk), 0)
    col = lax.broadcasted_iota(jnp.int32, (blk, blk), 1)
    scores = []
    for kb in range(nk):
        s = lax.dot_general(q2, k_ref[0, kb * blk:(kb + 1) * blk, :], nt, preferred_element_type=F32)
        nf = negf[:, kb * blk:(kb + 1) * blk]
        halves = [s[:blk] + nf[0:1], s[blk:] + nf[1:2]]
        if kb == nk - 1:
            halves = [jnp.where(col <= row, h, -jnp.inf) for h in halves]
        scores.append(jnp.concatenate(halves, axis=0))
    m = jnp.max(functools.reduce(jnp.maximum, scores), axis=-1, keepdims=True)
    mb = jnp.broadcast_to(m, scores[0].shape)
    probs = [jnp.exp(s - mb) for s in scores]
    l = jnp.sum(functools.reduce(jnp.add, probs), axis=-1, keepdims=True)
    acc = None
    for kb in range(nk):
        pv = jnp.dot(probs[kb].astype(BF16), v_ref[0, kb * blk:(kb + 1) * blk, :], preferred_element_type=F32)
        acc = pv if acc is None else acc + pv
    return acc * (1.0 / l)


def _attn_prompt_kernel(q_ref, k_ref, v_ref, lf_ref, o_ref, negf_scr, *, blk):
    T = k_ref.shape[1]
    negf_scr[...] = -_cumsum_lanes(lf_ref[0, 0])
    lane = lax.broadcasted_iota(jnp.int32, (blk, LANES), 1)
    for c in range(T // blk):
        q = q_ref[0, c * blk:(c + 1) * blk, :]
        zero = jnp.zeros_like(q)
        q2 = jnp.concatenate([jnp.where(lane < HEAD_DIM, q, zero), jnp.where(lane >= HEAD_DIM, q, zero)], axis=0)
        o = _attn_rows(q2, k_ref, v_ref, negf_scr[...], c + 1, blk)
        o_ref[0, c * blk:(c + 1) * blk, :] = jnp.where(lane < HEAD_DIM, o[:blk], o[blk:]).astype(o_ref.dtype)


def _attn_prompt(q, k, v, logf, *, blk):
    B, T, A = q.shape
    npair = A // LANES
    lf = logf.reshape(B, npair, HEADS_PER_BLOCK, T)
    kern = functools.partial(_attn_prompt_kernel, blk=blk)
    pair = pl.BlockSpec((1, T, LANES), lambda b, p: (b, 0, p))
    return pl.pallas_call(
        kern,
        grid=(B, npair),
        in_specs=[pair, pair, pair, pl.BlockSpec((1, 1, HEADS_PER_BLOCK, T), lambda b, p: (b, p, 0, 0))],
        out_specs=pair,
        out_shape=jax.ShapeDtypeStruct((B, T, A), BF16),
        scratch_shapes=[pltpu.VMEM((HEADS_PER_BLOCK, T), F32)],
        compiler_params=_params(("parallel", "parallel")),
        name="attn_prompt",
    )(q, k, v, lf)


def _attn_sample_kernel(q_ref, kn_ref, vn_ref, ck_ref, cv_ref, clf_ref, lf_ref, o_ref):
    tt = q_ref.shape[1]
    f_past = _cumsum_lanes(clf_ref[0, 0])
    f_new = f_past[:, -1:] + _cumsum_lanes(lf_ref[0, 0])
    row = lax.broadcasted_iota(jnp.int32, (tt, tt), 0)
    col = lax.broadcasted_iota(jnp.int32, (tt, tt), 1)
    outs = []
    for j in range(ck_ref.shape[1]):
        sl = slice(j * HEAD_DIM, (j + 1) * HEAD_DIM)
        qj = q_ref[0][:, sl]
        nt = (((1,), (1,)), ((), ()))
        s_p = lax.dot_general(qj, ck_ref[0, j].astype(BF16), nt, preferred_element_type=F32) - f_past[j:j + 1]
        s_n = lax.dot_general(qj, kn_ref[0][:, sl], nt, preferred_element_type=F32) - f_new[j:j + 1]
        s_n = jnp.where(col <= row, s_n, -jnp.inf)
        m = jnp.maximum(jnp.max(s_p, axis=-1, keepdims=True), jnp.max(s_n, axis=-1, keepdims=True))
        p_p = jnp.exp(s_p - m)
        p_n = jnp.exp(s_n - m)
        l = jnp.sum(p_p, axis=-1, keepdims=True) + jnp.sum(p_n, axis=-1, keepdims=True)
        o = (jnp.dot(p_p.astype(BF16), cv_ref[0, j].astype(BF16), preferred_element_type=F32)
             + jnp.dot(p_n.astype(BF16), vn_ref[0][:, sl], preferred_element_type=F32))
        outs.append(o * (1.0 / l))
    o_ref[0] = jnp.concatenate(outs, axis=-1).astype(o_ref.dtype)


def _attn_sample(q, k, v, cache_k, cache_v, cache_logf, logf):
    B, T, A = q.shape
    P = cache_k.shape[2]
    hg = SAMPLE_HEAD_GROUP
    ngrp = N_HEADS // hg
    clf = cache_logf.reshape(B, ngrp, hg, P)
    lf = logf.reshape(B, ngrp, hg, T)
    tok = lambda b, p: (b, 0, p)
    grp = lambda b, p: (b, p, 0, 0)
    return pl.pallas_call(
        _attn_sample_kernel,
        grid=(B, ngrp),
        in_specs=[
            pl.BlockSpec((1, T, hg * HEAD_DIM), tok),
            pl.BlockSpec((1, T, hg * HEAD_DIM), tok),
            pl.BlockSpec((1, T, hg * HEAD_DIM), tok),
            pl.BlockSpec((1, hg, P, HEAD_DIM), grp),
            pl.BlockSpec((1, hg, P, HEAD_DIM), grp),
            pl.BlockSpec((1, 1, hg, P), grp),
            pl.BlockSpec((1, 1, hg, T), grp),
        ],
        out_specs=pl.BlockSpec((1, T, hg * HEAD_DIM), tok),
        out_shape=jax.ShapeDtypeStruct((B, T, A), BF16),
        compiler_params=_params(("parallel", "parallel")),
        name="attn_sample",
    )(q, k, v, cache_k, cache_v, clf, lf)


def _mix_kernel(x_ref, o_ref, mod_ref, g_ref, w5_ref, cw_ref, pu_ref, wo_ref, y_ref, nu_ref, carry_scr):
    bb, tt, d = x_ref.shape
    tm = bb * tt
    t = pl.program_id(1)

    @pl.when(t == 0)
    def _():
        carry_scr[...] = pu_ref[...]

    x = x_ref[...]
    shift = mod_ref[:, 3:4, :]
    scale = mod_ref[:, 4:5, :]
    gate = mod_ref[:, 5:6, :]
    h = _rms_mod(x, g_ref[...], scale, shift).reshape(tm, d).astype(BF16)
    cdim = w5_ref.shape[1] // 5
    proj = lambda i: jnp.dot(h, w5_ref[:, i * cdim:(i + 1) * cdim], preferred_element_type=F32)

    u2 = proj(1) * proj(2)
    u = u2.reshape(bb, tt, cdim)
    tpos = lax.broadcasted_iota(jnp.int32, (1, tt, 1), 1)
    c0 = carry_scr[:, 0:1, :]
    c1 = carry_scr[:, 1:2, :]
    um1 = jnp.where(tpos == 0, c1, pltpu.roll(u2, 1, axis=0).reshape(bb, tt, cdim))
    um2 = jnp.where(tpos == 0, c0, jnp.where(tpos == 1, c1, pltpu.roll(u2, 2, axis=0).reshape(bb, tt, cdim)))
    conv = cw_ref[0:1, :] * um2 + cw_ref[1:2, :] * um1 + cw_ref[2:3, :] * u
    new_u = u[:, tt - (CONV_WIDTH - 1):, :]
    carry_scr[...] = new_u
    nu_ref[...] = new_u

    o_conv = proj(0) * conv.reshape(tm, cdim)
    m = _sigmoid(proj(3)) * o_ref[...].reshape(tm, cdim).astype(F32) + _sigmoid(proj(4)) * o_conv
    mo = jnp.dot(m.astype(BF16), wo_ref[...], preferred_element_type=F32)
    y_ref[...] = x + gate * mo.reshape(bb, tt, d)


def _mix(x, o_attn, mod, g, w5, conv_w, past_u, w_out, *, bb, tt):
    B, T, D = x.shape
    C = w5.shape[1] // 5
    tok = lambda b, t: (b, t, 0)
    per_b = lambda b, t: (b, 0, 0)
    return pl.pallas_call(
        _mix_kernel,
        grid=(B // bb, T // tt),
        in_specs=[
            pl.BlockSpec((bb, tt, D), tok),
            pl.BlockSpec((bb, tt, C), tok),
            pl.BlockSpec((bb, N_MOD, D), per_b),
            _resident((1, D)),
            _resident(w5.shape),
            _resident(conv_w.shape),
            pl.BlockSpec((bb, CONV_WIDTH - 1, C), per_b),
            _resident(w_out.shape),
        ],
        out_specs=[
            pl.BlockSpec((bb, tt, D), tok),
            pl.BlockSpec((bb, CONV_WIDTH - 1, C), per_b),
        ],
        out_shape=[
            jax.ShapeDtypeStruct((B, T, D), F32),
            jax.ShapeDtypeStruct((B, CONV_WIDTH - 1, C), F32),
        ],
        scratch_shapes=[pltpu.VMEM((bb, CONV_WIDTH - 1, C), F32)],
        compiler_params=_params(("parallel", "arbitrary")),
        name="mix",
    )(x, o_attn, mod, g, w5, conv_w, past_u, w_out)


def _layer(x, mod, past, weights, g_final, *, bb, tt):
    (g1, w1a, w1b, gm, wqkv, wf, bf, w5, conv_w, w_out, g2, w2a, w2b) = weights
    past_k, past_v, past_logf, past_u = past
    x1 = _ffn(x, mod, g1, w1a, w1b, g_final, bb=bb, tt=tt, mod_base=0, final_norm=False)
    q, kb, vb, k32, v32, logf = _qkv(x1, mod, gm, wqkv, wf, bf, bb=bb, tt=tt)
    if past_k is None:
        o = _attn_prompt(q, kb, vb, logf, blk=ATTN_BLOCK)
    else:
        o = _attn_sample(q, kb, vb, past_k, past_v, past_logf, logf)
    x2, new_u = _mix(x1, o, mod, gm, w5, conv_w, past_u, w_out, bb=bb, tt=tt)
    y = _ffn(x2, mod, g2, w2a, w2b, g_final, bb=bb, tt=tt, mod_base=6, final_norm=True)
    return y, k32, v32, logf, new_u


def kernel(x_prompt, x_sample, cache_k, cache_v, cache_logf, state_conv, c_prompt, c_sample, w_ada, b_ada, g_ffn1, w_ffn1_in, w_ffn1_out, g_mix, w_in, b_f, conv_w, w_out, g_ffn2, w_ffn2_in, w_ffn2_out, g_final):
    assert w_ada.shape[0] == 1, "single-layer encoder"
    Bp, Tp, D = x_prompt.shape
    Bs, Ts, _ = x_sample.shape
    A = N_HEADS * HEAD_DIM
    C = conv_w.shape[2]

    mod = _ada(jnp.concatenate([c_prompt, c_sample], axis=0), w_ada[0], b_ada[0])
    mod = mod.reshape(Bp + Bs, N_MOD, D)
    mod_p, mod_s = mod[:Bp], mod[Bp:]

    wi = w_in[0]
    off_f = 3 * A
    off_b = off_f + N_HEADS
    wqkv = wi[:, :off_f].astype(BF16)
    wf = jnp.pad(wi[:, off_f:off_b], ((0, 0), (0, LANES - N_HEADS))).astype(BF16)
    bf = jnp.pad(b_f[0], (0, LANES - N_HEADS)).reshape(1, LANES)
    w5 = wi[:, off_b:].astype(BF16)
    weights = (g_ffn1, w_ffn1_in[0].astype(BF16), w_ffn1_out[0].astype(BF16), g_mix, wqkv, wf, bf, w5,
               conv_w[0], w_out[0].astype(BF16), g_ffn2, w_ffn2_in[0].astype(BF16), w_ffn2_out[0].astype(BF16))
    gfin = g_final.reshape(1, D)

    zero_u = jnp.zeros((Bp, CONV_WIDTH - 1, C), F32)
    yp, kp, vp, fp, up = _layer(x_prompt, mod_p, (None, None, None, zero_u), weights, gfin,
                                bb=1, tt=ROW_TILE)
    ys, ks, vs, fs, us = _layer(x_sample, mod_s, (cache_k[0], cache_v[0], cache_logf[0], state_conv[0]),
                                weights, gfin, bb=ROW_TILE // Ts, tt=Ts)
    return (yp, ys, kp[None], vp[None], fp[None], up[None], ks[None], vs[None], fs[None], us[None])
```

```python
import functools

import jax
import jax.numpy as jnp
from jax import lax
from jax.experimental import pallas as pl
from jax.experimental.pallas import tpu as pltpu

F32 = jnp.float32
BF16 = jnp.bfloat16

EPS = 1e-6
N_HEADS = 16
HEAD_DIM = 64
N_MOD = 9
CONV_WIDTH = 3
LANES = 128
HEADS_PER_BLOCK = LANES // HEAD_DIM
V7X_VMEM_LIMIT_BYTES = 56 * 1024 * 1024
FFN_CHUNK = 256
ROW_TILE = 512
ATTN_BLOCK = 256
SAMPLE_HEAD_GROUP = 8


def _params(semantics):
    return pltpu.CompilerParams(dimension_semantics=semantics, vmem_limit_bytes=V7X_VMEM_LIMIT_BYTES)


def _resident(shape):
    zeros = (0,) * len(shape)
    return pl.BlockSpec(shape, lambda *_: zeros, pipeline_mode=pl.Buffered(1))


def _rms_mod(x, g, scale, shift):
    ms = jnp.mean(x * x, axis=-1, keepdims=True)
    y = x * lax.rsqrt(ms + EPS) * g
    return y * (1.0 + scale) + shift


def _sigmoid(x):
    return 1.0 / (1.0 + jnp.exp(-x))


def _log_sigmoid(x):
    return jnp.minimum(x, 0.0) - jnp.log1p(jnp.exp(-jnp.abs(x)))


def _cumsum_lanes(x):
    n = x.shape[-1]
    lane = lax.broadcasted_iota(jnp.int32, x.shape, x.ndim - 1)
    step = 1
    while step < n:
        x = x + jnp.where(lane >= step, pltpu.roll(x, step, axis=x.ndim - 1), 0.0)
        step *= 2
    return x


def _ada_kernel(c_ref, w_ref, b_ref, o_ref):
    c = c_ref[...]
    a = (c * _sigmoid(c)).astype(BF16)
    o_ref[...] = jnp.dot(a, w_ref[...].astype(BF16), preferred_element_type=F32) + b_ref[...]


def _ada(c, w_ada, b_ada):
    n, d = c.shape
    cols = w_ada.shape[1]
    tn = d
    return pl.pallas_call(
        _ada_kernel,
        grid=(cols // tn,),
        in_specs=[
            pl.BlockSpec((n, d), lambda j: (0, 0)),
            pl.BlockSpec((d, tn), lambda j: (0, j)),
            pl.BlockSpec((1, tn), lambda j: (0, j)),
        ],
        out_specs=pl.BlockSpec((n, tn), lambda j: (0, j)),
        out_shape=jax.ShapeDtypeStruct((n, cols), F32),
        compiler_params=_params(("parallel",)),
        name="ada_mod",
    )(c, w_ada, b_ada.reshape(1, cols))


def _ffn_kernel(x_ref, mod_ref, g_ref, w1_ref, w2_ref, gfin_ref, o_ref, h_scr, *, mod_base, final_norm):
    bb, tt, d = x_ref.shape
    tm = bb * tt
    f = w2_ref.shape[0]
    fc = FFN_CHUNK
    x = x_ref[...]
    shift = mod_ref[:, mod_base:mod_base + 1, :]
    scale = mod_ref[:, mod_base + 1:mod_base + 2, :]
    gate = mod_ref[:, mod_base + 2:mod_base + 3, :]
    h_scr[...] = _rms_mod(x, g_ref[...], scale, shift).reshape(tm, d).astype(BF16)

    acc = None
    for lo in range(0, f, fc):
        h = h_scr[...]
        a = jnp.dot(h, w1_ref[:, lo:lo + fc], preferred_element_type=F32)
        b = jnp.dot(h, w1_ref[:, f + lo:f + lo + fc], preferred_element_type=F32)
        act = (a * _sigmoid(a) * b).astype(BF16)
        part = jnp.dot(act, w2_ref[lo:lo + fc, :], preferred_element_type=F32)
        acc = part if acc is None else acc + part

    y = x + 0.5 * gate * acc.reshape(bb, tt, d)
    if final_norm:
        ms = jnp.mean(y * y, axis=-1, keepdims=True)
        y = y * lax.rsqrt(ms + EPS) * gfin_ref[...]
    o_ref[...] = y


def _ffn(x, mod, g, w1c, w2c, g_final, *, bb, tt, mod_base, final_norm):
    B, T, D = x.shape
    assert w2c.shape[0] % FFN_CHUNK == 0
    tm = bb * tt
    kern = functools.partial(_ffn_kernel, mod_base=mod_base, final_norm=final_norm)
    return pl.pallas_call(
        kern,
        grid=(B // bb, T // tt),
        in_specs=[
            pl.BlockSpec((bb, tt, D), lambda b, t: (b, t, 0)),
            pl.BlockSpec((bb, N_MOD, D), lambda b, t: (b, 0, 0)),
            _resident((1, D)),
            _resident(w1c.shape),
            _resident(w2c.shape),
            _resident((1, D)),
        ],
        out_specs=pl.BlockSpec((bb, tt, D), lambda b, t: (b, t, 0)),
        out_shape=jax.ShapeDtypeStruct((B, T, D), F32),
        scratch_shapes=[pltpu.VMEM((tm, D), BF16)],
        compiler_params=_params(("parallel", "parallel")),
        name="ffn_final" if final_norm else "ffn",
    )(x, mod, g, w1c, w2c, g_final)


def _qkv_kernel(x_ref, mod_ref, g_ref, wqkv_ref, wf_ref, bf_ref,
                q_ref, kb_ref, vb_ref, k32_ref, v32_ref, lf_ref, *, kv_transposed):
    bb, tt, d = x_ref.shape
    tm = bb * tt
    a_dim = wqkv_ref.shape[1] // 3
    shift = mod_ref[:, 3:4, :]
    scale = mod_ref[:, 4:5, :]
    h = _rms_mod(x_ref[...], g_ref[...], scale, shift).reshape(tm, d).astype(BF16)

    zq = jnp.dot(h, wqkv_ref[:, 0:a_dim], preferred_element_type=F32)
    q_ref[...] = (zq * (HEAD_DIM ** -0.5)).astype(BF16).reshape(bb, tt, a_dim)

    for w_idx, (lo_ref, hi_ref) in ((1, (kb_ref, k32_ref)), (2, (vb_ref, v32_ref))):
        z = jnp.dot(h, wqkv_ref[:, w_idx * a_dim:(w_idx + 1) * a_dim], preferred_element_type=F32)
        lo_ref[...] = z.astype(BF16).reshape(bb, tt, a_dim)
        if kv_transposed:
            hi_ref[0] = z.T.reshape(N_HEADS, HEAD_DIM, tt)
        else:
            for hd in range(N_HEADS):
                hi_ref[:, hd, :, :] = z[:, hd * HEAD_DIM:(hd + 1) * HEAD_DIM].reshape(bb, tt, HEAD_DIM)

    zf = jnp.dot(h, wf_ref[...], preferred_element_type=F32)
    lf = _log_sigmoid(zf + bf_ref[...]).T
    for b in range(bb):
        lf_ref[b] = lf[:N_HEADS, b * tt:(b + 1) * tt]


def _qkv(x, mod, g, wqkv, wf, bf, *, bb, tt, kv_transposed):
    B, T, D = x.shape
    A = wqkv.shape[1] // 3
    tok = lambda b, t: (b, t, 0)
    if kv_transposed:
        assert bb == 1
        kv_block, kv_shape = (1, N_HEADS, HEAD_DIM, tt), (B, N_HEADS, HEAD_DIM, T)
        head = lambda b, t: (b, 0, 0, t)
    else:
        kv_block, kv_shape = (bb, N_HEADS, tt, HEAD_DIM), (B, N_HEADS, T, HEAD_DIM)
        head = lambda b, t: (b, 0, t, 0)
    return pl.pallas_call(
        functools.partial(_qkv_kernel, kv_transposed=kv_transposed),
        grid=(B // bb, T // tt),
        in_specs=[
            pl.BlockSpec((bb, tt, D), tok),
            pl.BlockSpec((bb, N_MOD, D), lambda b, t: (b, 0, 0)),
            _resident((1, D)),
            _resident(wqkv.shape),
            _resident(wf.shape),
            _resident(bf.shape),
        ],
        out_specs=[
            pl.BlockSpec((bb, tt, A), tok),
            pl.BlockSpec((bb, tt, A), tok),
            pl.BlockSpec((bb, tt, A), tok),
            pl.BlockSpec(kv_block, head),
            pl.BlockSpec(kv_block, head),
            pl.BlockSpec((bb, N_HEADS, tt), lambda b, t: (b, 0, t)),
        ],
        out_shape=[
            jax.ShapeDtypeStruct((B, T, A), BF16),
            jax.ShapeDtypeStruct((B, T, A), BF16),
            jax.ShapeDtypeStruct((B, T, A), BF16),
            jax.ShapeDtypeStruct(kv_shape, F32),
            jax.ShapeDtypeStruct(kv_shape, F32),
            jax.ShapeDtypeStruct((B, N_HEADS, T), F32),
        ],
        compiler_params=_params(("parallel", "parallel")),
        name="qkv_proj",
    )(x, mod, g, wqkv, wf, bf)


def _attn_rows(q2, k_ref, v_ref, negf, nk, blk):
    nt = (((1,), (1,)), ((), ()))
    row = lax.broadcasted_iota(jnp.int32, (blk, blk), 0)
    col = lax.broadcasted_iota(jnp.int32, (blk, blk), 1)
    scores = []
    for kb in range(nk):
        s = lax.dot_general(q2, k_ref[0, kb * blk:(kb + 1) * blk, :], nt, preferred_element_type=F32)
        nf = negf[:, kb * blk:(kb + 1) * blk]
        halves = [s[:blk] + nf[0:1], s[blk:] + nf[1:2]]
        if kb == nk - 1:
            halves = [jnp.where(col <= row, h, -jnp.inf) for h in halves]
        scores.append(jnp.concatenate(halves, axis=0))
    m = jnp.max(functools.reduce(jnp.maximum, scores), axis=-1, keepdims=True)
    mb = jnp.broadcast_to(m, scores[0].shape)
    probs = [jnp.exp(s - mb) for s in scores]
    l = jnp.sum(functools.reduce(jnp.add, probs), axis=-1, keepdims=True)
    acc = None
    for kb in range(nk):
        pv = jnp.dot(probs[kb].astype(BF16), v_ref[0, kb * blk:(kb + 1) * blk, :], preferred_element_type=F32)
        acc = pv if acc is None else acc + pv
    return acc * (1.0 / l)


def _attn_prompt_kernel(q_ref, k_ref, v_ref, lf_ref, o_ref, negf_scr, *, blk):
    T = k_ref.shape[1]
    negf_scr[...] = -_cumsum_lanes(lf_ref[0, 0])
    lane = lax.broadcasted_iota(jnp.int32, (blk, LANES), 1)
    for c in range(T // blk):
        q = q_ref[0, c * blk:(c + 1) * blk, :]
        zero = jnp.zeros_like(q)
        q2 = jnp.concatenate([jnp.where(lane < HEAD_DIM, q, zero), jnp.where(lane >= HEAD_DIM, q, zero)], axis=0)
        o = _attn_rows(q2, k_ref, v_ref, negf_scr[...], c + 1, blk)
        o_ref[0, c * blk:(c + 1) * blk, :] = jnp.where(lane < HEAD_DIM, o[:blk], o[blk:]).astype(o_ref.dtype)


def _attn_prompt(q, k, v, logf, *, blk):
    B, T, A = q.shape
    npair = A // LANES
    lf = logf.reshape(B, npair, HEADS_PER_BLOCK, T)
    kern = functools.partial(_attn_prompt_kernel, blk=blk)
    pair = pl.BlockSpec((1, T, LANES), lambda b, p: (b, 0, p))
    return pl.pallas_call(
        kern,
        grid=(B, npair),
        in_specs=[pair, pair, pair, pl.BlockSpec((1, 1, HEADS_PER_BLOCK, T), lambda b, p: (b, p, 0, 0))],
        out_specs=pair,
        out_shape=jax.ShapeDtypeStruct((B, T, A), BF16),
        scratch_shapes=[pltpu.VMEM((HEADS_PER_BLOCK, T), F32)],
        compiler_params=_params(("parallel", "parallel")),
        name="attn_prompt",
    )(q, k, v, lf)


def _attn_sample_kernel(q_ref, kn_ref, vn_ref, ckt_ref, cvt_ref, clf_ref, lf_ref, o_ref):
    tt = q_ref.shape[1]
    f_past = _cumsum_lanes(clf_ref[0, 0])
    f_new = f_past[:, -1:] + _cumsum_lanes(lf_ref[0, 0])
    row = lax.broadcasted_iota(jnp.int32, (tt, tt), 0)
    col = lax.broadcasted_iota(jnp.int32, (tt, tt), 1)
    nt = (((1,), (1,)), ((), ()))
    outs = []
    for j in range(ckt_ref.shape[1]):
        sl = slice(j * HEAD_DIM, (j + 1) * HEAD_DIM)
        qj = q_ref[0][:, sl]
        s_p = jnp.dot(qj, ckt_ref[0, j].astype(BF16), preferred_element_type=F32) - f_past[j:j + 1]
        s_n = lax.dot_general(qj, kn_ref[0][:, sl], nt, preferred_element_type=F32) - f_new[j:j + 1]
        s_n = jnp.where(col <= row, s_n, -jnp.inf)
        m = jnp.maximum(jnp.max(s_p, axis=-1, keepdims=True), jnp.max(s_n, axis=-1, keepdims=True))
        p_p = jnp.exp(s_p - m)
        p_n = jnp.exp(s_n - m)
        l = jnp.sum(p_p, axis=-1, keepdims=True) + jnp.sum(p_n, axis=-1, keepdims=True)
        o = (lax.dot_general(p_p.astype(BF16), cvt_ref[0, j].astype(BF16), nt, preferred_element_type=F32)
             + jnp.dot(p_n.astype(BF16), vn_ref[0][:, sl], preferred_element_type=F32))
        outs.append(o * (1.0 / l))
    o_ref[0] = jnp.concatenate(outs, axis=-1).astype(o_ref.dtype)


def _attn_sample(q, k, v, cache_kt, cache_vt, cache_logf, logf):
    B, T, A = q.shape
    P = cache_kt.shape[3]
    hg = SAMPLE_HEAD_GROUP
    ngrp = N_HEADS // hg
    clf = cache_logf.reshape(B, ngrp, hg, P)
    lf = logf.reshape(B, ngrp, hg, T)
    tok = lambda b, p: (b, 0, p)
    grp = lambda b, p: (b, p, 0, 0)
    return pl.pallas_call(
        _attn_sample_kernel,
        grid=(B, ngrp),
        in_specs=[
            pl.BlockSpec((1, T, hg * HEAD_DIM), tok),
            pl.BlockSpec((1, T, hg * HEAD_DIM), tok),
            pl.BlockSpec((1, T, hg * HEAD_DIM), tok),
            pl.BlockSpec((1, hg, HEAD_DIM, P), grp),
            pl.BlockSpec((1, hg, HEAD_DIM, P), grp),
            pl.BlockSpec((1, 1, hg, P), grp),
            pl.BlockSpec((1, 1, hg, T), grp),
        ],
        out_specs=pl.BlockSpec((1, T, hg * HEAD_DIM), tok),
        out_shape=jax.ShapeDtypeStruct((B, T, A), BF16),
        compiler_params=_params(("parallel", "parallel")),
        name="attn_sample",
    )(q, k, v, cache_kt, cache_vt, clf, lf)


def _mix_kernel(x_ref, o_ref, mod_ref, g_ref, w5_ref, cw_ref, pu_ref, wo_ref, y_ref, nu_ref, carry_scr):
    bb, tt, d = x_ref.shape
    tm = bb * tt
    t = pl.program_id(1)

    @pl.when(t == 0)
    def _():
        carry_scr[...] = pu_ref[...]

    x = x_ref[...]
    shift = mod_ref[:, 3:4, :]
    scale = mod_ref[:, 4:5, :]
    gate = mod_ref[:, 5:6, :]
    h = _rms_mod(x, g_ref[...], scale, shift).reshape(tm, d).astype(BF16)
    cdim = w5_ref.shape[1] // 5
    proj = lambda i: jnp.dot(h, w5_ref[:, i * cdim:(i + 1) * cdim], preferred_element_type=F32)

    u2 = proj(1) * proj(2)
    u = u2.reshape(bb, tt, cdim)
    tpos = lax.broadcasted_iota(jnp.int32, (1, tt, 1), 1)
    c0 = carry_scr[:, 0:1, :]
    c1 = carry_scr[:, 1:2, :]
    um1 = jnp.where(tpos == 0, c1, pltpu.roll(u2, 1, axis=0).reshape(bb, tt, cdim))
    um2 = jnp.where(tpos == 0, c0, jnp.where(tpos == 1, c1, pltpu.roll(u2, 2, axis=0).reshape(bb, tt, cdim)))
    conv = cw_ref[0:1, :] * um2 + cw_ref[1:2, :] * um1 + cw_ref[2:3, :] * u
    new_u = u[:, tt - (CONV_WIDTH - 1):, :]
    carry_scr[...] = new_u
    nu_ref[...] = new_u

    o_conv = proj(0) * conv.reshape(tm, cdim)
    m = _sigmoid(proj(3)) * o_ref[...].reshape(tm, cdim).astype(F32) + _sigmoid(proj(4)) * o_conv
    mo = jnp.dot(m.astype(BF16), wo_ref[...], preferred_element_type=F32)
    y_ref[...] = x + gate * mo.reshape(bb, tt, d)


def _mix(x, o_attn, mod, g, w5, conv_w, past_u, w_out, *, bb, tt):
    B, T, D = x.shape
    C = w5.shape[1] // 5
    tok = lambda b, t: (b, t, 0)
    per_b = lambda b, t: (b, 0, 0)
    return pl.pallas_call(
        _mix_kernel,
        grid=(B // bb, T // tt),
        in_specs=[
            pl.BlockSpec((bb, tt, D), tok),
            pl.BlockSpec((bb, tt, C), tok),
            pl.BlockSpec((bb, N_MOD, D), per_b),
            _resident((1, D)),
            _resident(w5.shape),
            _resident(conv_w.shape),
            pl.BlockSpec((bb, CONV_WIDTH - 1, C), per_b),
            _resident(w_out.shape),
        ],
        out_specs=[
            pl.BlockSpec((bb, tt, D), tok),
            pl.BlockSpec((bb, CONV_WIDTH - 1, C), per_b),
        ],
        out_shape=[
            jax.ShapeDtypeStruct((B, T, D), F32),
            jax.ShapeDtypeStruct((B, CONV_WIDTH - 1, C), F32),
        ],
        scratch_shapes=[pltpu.VMEM((bb, CONV_WIDTH - 1, C), F32)],
        compiler_params=_params(("parallel", "arbitrary")),
        name="mix",
    )(x, o_attn, mod, g, w5, conv_w, past_u, w_out)


def _layer(x, mod, past, weights, g_final, *, bb, tt):
    (g1, w1a, w1b, gm, wqkv, wf, bf, w5, conv_w, w_out, g2, w2a, w2b) = weights
    past_kt, past_vt, past_logf, past_u = past
    prompt = past_kt is None
    x1 = _ffn(x, mod, g1, w1a, w1b, g_final, bb=bb, tt=tt, mod_base=0, final_norm=False)
    q, kb, vb, k32, v32, logf = _qkv(x1, mod, gm, wqkv, wf, bf, bb=bb, tt=tt, kv_transposed=prompt)
    if prompt:
        o = _attn_prompt(q, kb, vb, logf, blk=ATTN_BLOCK)
        k32, v32 = jnp.swapaxes(k32, 2, 3), jnp.swapaxes(v32, 2, 3)
    else:
        o = _attn_sample(q, kb, vb, past_kt, past_vt, past_logf, logf)
    x2, new_u = _mix(x1, o, mod, gm, w5, conv_w, past_u, w_out, bb=bb, tt=tt)
    y = _ffn(x2, mod, g2, w2a, w2b, g_final, bb=bb, tt=tt, mod_base=6, final_norm=True)
    return y, k32, v32, logf, new_u


def kernel(x_prompt, x_sample, cache_k, cache_v, cache_logf, state_conv, c_prompt, c_sample, w_ada, b_ada, g_ffn1, w_ffn1_in, w_ffn1_out, g_mix, w_in, b_f, conv_w, w_out, g_ffn2, w_ffn2_in, w_ffn2_out, g_final):
    assert w_ada.shape[0] == 1, "single-layer encoder"
    Bp, Tp, D = x_prompt.shape
    Bs, Ts, _ = x_sample.shape
    A = N_HEADS * HEAD_DIM
    C = conv_w.shape[2]

    mod = _ada(jnp.concatenate([c_prompt, c_sample], axis=0), w_ada[0], b_ada[0])
    mod = mod.reshape(Bp + Bs, N_MOD, D)
    mod_p, mod_s = mod[:Bp], mod[Bp:]

    wi = w_in[0]
    off_f = 3 * A
    off_b = off_f + N_HEADS
    wqkv = wi[:, :off_f].astype(BF16)
    wf = jnp.pad(wi[:, off_f:off_b], ((0, 0), (0, LANES - N_HEADS))).astype(BF16)
    bf = jnp.pad(b_f[0], (0, LANES - N_HEADS)).reshape(1, LANES)
    w5 = wi[:, off_b:].astype(BF16)
    weights = (g_ffn1, w_ffn1_in[0].astype(BF16), w_ffn1_out[0].astype(BF16), g_mix, wqkv, wf, bf, w5,
               conv_w[0], w_out[0].astype(BF16), g_ffn2, w_ffn2_in[0].astype(BF16), w_ffn2_out[0].astype(BF16))
    gfin = g_final.reshape(1, D)

    zero_u = jnp.zeros((Bp, CONV_WIDTH - 1, C), F32)
    yp, kp, vp, fp, up = _layer(x_prompt, mod_p, (None, None, None, zero_u), weights, gfin,
                                bb=1, tt=ROW_TILE)
    past = (jnp.swapaxes(cache_k[0], 2, 3), jnp.swapaxes(cache_v[0], 2, 3), cache_logf[0], state_conv[0])
    ys, ks, vs, fs, us = _layer(x_sample, mod_s, past, weights, gfin, bb=ROW_TILE // Ts, tt=Ts)
    return (yp, ys, kp[None], vp[None], fp[None], up[None], ks[None], vs[None], fs[None], us[None])
```

```python
import functools

import jax
import jax.numpy as jnp
from jax import lax
from jax.experimental import pallas as pl
from jax.experimental.pallas import tpu as pltpu

F32 = jnp.float32
BF16 = jnp.bfloat16

EPS = 1e-6
N_HEADS = 16
HEAD_DIM = 64
N_MOD = 9
CONV_WIDTH = 3
LANES = 128
HEADS_PER_BLOCK = LANES // HEAD_DIM
V7X_VMEM_LIMIT_BYTES = 56 * 1024 * 1024
FFN_CHUNK = 256
ROW_TILE = 512
ATTN_BLOCK = 256
ATTN_ROWS = 64
LOG2E = 1.4426950408889634
Q_SCALE = HEAD_DIM ** -0.5 * LOG2E
SAMPLE_HEAD_GROUP = 8


def _params(semantics):
    return pltpu.CompilerParams(dimension_semantics=semantics, vmem_limit_bytes=V7X_VMEM_LIMIT_BYTES)


def _resident(shape):
    zeros = (0,) * len(shape)
    return pl.BlockSpec(shape, lambda *_: zeros, pipeline_mode=pl.Buffered(1))


def _rms_mod(x, g, scale, shift):
    ms = jnp.mean(x * x, axis=-1, keepdims=True)
    y = x * lax.rsqrt(ms + EPS) * g
    return y * (1.0 + scale) + shift


def _sigmoid(x):
    return 1.0 / (1.0 + jnp.exp(-x))


def _log_sigmoid(x):
    return jnp.minimum(x, 0.0) - jnp.log1p(jnp.exp(-jnp.abs(x)))


def _cumsum_lanes(x):
    n = x.shape[-1]
    lane = lax.broadcasted_iota(jnp.int32, x.shape, x.ndim - 1)
    step = 1
    while step < n:
        x = x + jnp.where(lane >= step, pltpu.roll(x, step, axis=x.ndim - 1), 0.0)
        step *= 2
    return x


def _ada_kernel(c_ref, w_ref, b_ref, o_ref):
    c = c_ref[...]
    a = (c * _sigmoid(c)).astype(BF16)
    o_ref[...] = jnp.dot(a, w_ref[...].astype(BF16), preferred_element_type=F32) + b_ref[...]


def _ada(c, w_ada, b_ada):
    n, d = c.shape
    cols = w_ada.shape[1]
    tn = d
    return pl.pallas_call(
        _ada_kernel,
        grid=(cols // tn,),
        in_specs=[
            pl.BlockSpec((n, d), lambda j: (0, 0)),
            pl.BlockSpec((d, tn), lambda j: (0, j)),
            pl.BlockSpec((1, tn), lambda j: (0, j)),
        ],
        out_specs=pl.BlockSpec((n, tn), lambda j: (0, j)),
        out_shape=jax.ShapeDtypeStruct((n, cols), F32),
        compiler_params=_params(("parallel",)),
        name="ada_mod",
    )(c, w_ada, b_ada.reshape(1, cols))


def _ffn_kernel(x_ref, mod_ref, g_ref, w1_ref, w2_ref, gfin_ref, o_ref, h_scr, *, mod_base, final_norm):
    bb, tt, d = x_ref.shape
    tm = bb * tt
    f = w2_ref.shape[0]
    fc = FFN_CHUNK
    x = x_ref[...]
    shift = mod_ref[:, mod_base:mod_base + 1, :]
    scale = mod_ref[:, mod_base + 1:mod_base + 2, :]
    gate = mod_ref[:, mod_base + 2:mod_base + 3, :]
    h_scr[...] = _rms_mod(x, g_ref[...], scale, shift).reshape(tm, d).astype(BF16)

    acc = None
    for lo in range(0, f, fc):
        h = h_scr[...]
        a = jnp.dot(h, w1_ref[:, lo:lo + fc], preferred_element_type=F32)
        b = jnp.dot(h, w1_ref[:, f + lo:f + lo + fc], preferred_element_type=F32)
        act = (a * _sigmoid(a) * b).astype(BF16)
        part = jnp.dot(act, w2_ref[lo:lo + fc, :], preferred_element_type=F32)
        acc = part if acc is None else acc + part

    y = x + 0.5 * gate * acc.reshape(bb, tt, d)
    if final_norm:
        ms = jnp.mean(y * y, axis=-1, keepdims=True)
        y = y * lax.rsqrt(ms + EPS) * gfin_ref[...]
    o_ref[...] = y


def _ffn(x, mod, g, w1c, w2c, g_final, *, bb, tt, mod_base, final_norm):
    B, T, D = x.shape
    assert w2c.shape[0] % FFN_CHUNK == 0
    tm = bb * tt
    kern = functools.partial(_ffn_kernel, mod_base=mod_base, final_norm=final_norm)
    return pl.pallas_call(
        kern,
        grid=(B // bb, T // tt),
        in_specs=[
            pl.BlockSpec((bb, tt, D), lambda b, t: (b, t, 0)),
            pl.BlockSpec((bb, N_MOD, D), lambda b, t: (b, 0, 0)),
            _resident((1, D)),
            _resident(w1c.shape),
            _resident(w2c.shape),
            _resident((1, D)),
        ],
        out_specs=pl.BlockSpec((bb, tt, D), lambda b, t: (b, t, 0)),
        out_shape=jax.ShapeDtypeStruct((B, T, D), F32),
        scratch_shapes=[pltpu.VMEM((tm, D), BF16)],
        compiler_params=_params(("parallel", "parallel")),
        name="ffn_final" if final_norm else "ffn",
    )(x, mod, g, w1c, w2c, g_final)


def _qkv_kernel(x_ref, mod_ref, g_ref, wqkv_ref, wf_ref, bf_ref,
                q_ref, kb_ref, vb_ref, k32_ref, v32_ref, lf_ref, *, kv_transposed):
    bb, tt, d = x_ref.shape
    tm = bb * tt
    a_dim = wqkv_ref.shape[1] // 3
    shift = mod_ref[:, 3:4, :]
    scale = mod_ref[:, 4:5, :]
    h = _rms_mod(x_ref[...], g_ref[...], scale, shift).reshape(tm, d).astype(BF16)

    zq = jnp.dot(h, wqkv_ref[:, 0:a_dim], preferred_element_type=F32)
    q_ref[...] = (zq * Q_SCALE).astype(BF16).reshape(bb, tt, a_dim)

    for w_idx, (lo_ref, hi_ref) in ((1, (kb_ref, k32_ref)), (2, (vb_ref, v32_ref))):
        z = jnp.dot(h, wqkv_ref[:, w_idx * a_dim:(w_idx + 1) * a_dim], preferred_element_type=F32)
        lo_ref[...] = z.astype(BF16).reshape(bb, tt, a_dim)
        if kv_transposed:
            hi_ref[0] = z.T.reshape(N_HEADS, HEAD_DIM, tt)
        else:
            for hd in range(N_HEADS):
                hi_ref[:, hd, :, :] = z[:, hd * HEAD_DIM:(hd + 1) * HEAD_DIM].reshape(bb, tt, HEAD_DIM)

    zf = jnp.dot(h, wf_ref[...], preferred_element_type=F32)
    lf = _log_sigmoid(zf + bf_ref[...]).T
    for b in range(bb):
        lf_ref[b] = lf[:N_HEADS, b * tt:(b + 1) * tt]


def _qkv(x, mod, g, wqkv, wf, bf, *, bb, tt, kv_transposed):
    B, T, D = x.shape
    A = wqkv.shape[1] // 3
    tok = lambda b, t: (b, t, 0)
    if kv_transposed:
        assert bb == 1
        kv_block, kv_shape = (1, N_HEADS, HEAD_DIM, tt), (B, N_HEADS, HEAD_DIM, T)
        head = lambda b, t: (b, 0, 0, t)
    else:
        kv_block, kv_shape = (bb, N_HEADS, tt, HEAD_DIM), (B, N_HEADS, T, HEAD_DIM)
        head = lambda b, t: (b, 0, t, 0)
    return pl.pallas_call(
        functools.partial(_qkv_kernel, kv_transposed=kv_transposed),
        grid=(B // bb, T // tt),
        in_specs=[
            pl.BlockSpec((bb, tt, D), tok),
            pl.BlockSpec((bb, N_MOD, D), lambda b, t: (b, 0, 0)),
            _resident((1, D)),
            _resident(wqkv.shape),
            _resident(wf.shape),
            _resident(bf.shape),
        ],
        out_specs=[
            pl.BlockSpec((bb, tt, A), tok),
            pl.BlockSpec((bb, tt, A), tok),
            pl.BlockSpec((bb, tt, A), tok),
            pl.BlockSpec(kv_block, head),
            pl.BlockSpec(kv_block, head),
            pl.BlockSpec((bb, N_HEADS, tt), lambda b, t: (b, 0, t)),
        ],
        out_shape=[
            jax.ShapeDtypeStruct((B, T, A), BF16),
            jax.ShapeDtypeStruct((B, T, A), BF16),
            jax.ShapeDtypeStruct((B, T, A), BF16),
            jax.ShapeDtypeStruct(kv_shape, F32),
            jax.ShapeDtypeStruct(kv_shape, F32),
            jax.ShapeDtypeStruct((B, N_HEADS, T), F32),
        ],
        compiler_params=_params(("parallel", "parallel")),
        name="qkv_proj",
    )(x, mod, g, wqkv, wf, bf)


def _attn_rows(q2, k_ref, v1_ref, negf_ref, s_scr, p_scr, nk, blk):
    nt = (((1,), (1,)), ((), ()))
    rows = HEADS_PER_BLOCK * blk
    half = blk // 2
    nkeys = nk * blk
    keep = lax.broadcasted_iota(jnp.int32, (blk, blk), 1) <= lax.broadcasted_iota(jnp.int32, (blk, blk), 0)
    s = lax.dot_general(q2, k_ref[0, 0:nkeys, :], nt, preferred_element_type=F32)
    for kb in range(nk):
        ks = slice(kb * blk, (kb + 1) * blk)
        for h in range(HEADS_PER_BLOCK):
            t = s[h * blk:(h + 1) * blk, ks] + negf_ref[h:h + 1, ks]
            if kb == nk - 1:
                t = jnp.where(keep, t, -jnp.inf)
            s_scr[h * blk:(h + 1) * blk, ks] = t

    for r0 in range(0, rows, ATTN_ROWS):
        rs = slice(r0, r0 + ATTN_ROWS)
        mx = None
        for kb in range(nk):
            t = s_scr[rs, kb * blk:(kb + 1) * blk]
            t = jnp.maximum(t[:, :half], t[:, half:])
            mx = t if mx is None else jnp.maximum(mx, t)
        mb = jnp.broadcast_to(jnp.max(mx, axis=-1, keepdims=True), (ATTN_ROWS, blk))
        for kb in range(nk):
            ks = slice(kb * blk, (kb + 1) * blk)
            p_scr[rs, ks] = jnp.exp2(s_scr[rs, ks] - mb).astype(BF16)

    acc = jnp.dot(p_scr[:, 0:nkeys], v1_ref[0:nkeys, :], preferred_element_type=F32)
    return acc[:, :LANES] * (1.0 / acc[:, LANES:])


def _attn_prompt_kernel(q_ref, k_ref, v_ref, lf_ref, o_ref, negf_scr, v1_scr, s_scr, p_scr, *, blk):
    T = k_ref.shape[1]
    negf_scr[...] = -LOG2E * _cumsum_lanes(lf_ref[0, 0])
    v1_scr[:, :LANES] = v_ref[0]
    v1_scr[:, LANES:] = jnp.ones((T, LANES), v1_scr.dtype)
    lane = lax.broadcasted_iota(jnp.int32, (blk, LANES), 1)
    for c in range(T // blk):
        q = q_ref[0, c * blk:(c + 1) * blk, :]
        zero = jnp.zeros_like(q)
        q2 = jnp.concatenate([jnp.where(lane < HEAD_DIM, q, zero), jnp.where(lane >= HEAD_DIM, q, zero)], axis=0)
        o = _attn_rows(q2, k_ref, v1_scr, negf_scr, s_scr.at[c % 2], p_scr.at[c % 2], c + 1, blk)
        o_ref[0, c * blk:(c + 1) * blk, :] = jnp.where(lane < HEAD_DIM, o[:blk], o[blk:]).astype(o_ref.dtype)


def _attn_prompt(q, k, v, logf, *, blk):
    B, T, A = q.shape
    npair = A // LANES
    lf = logf.reshape(B, npair, HEADS_PER_BLOCK, T)
    kern = functools.partial(_attn_prompt_kernel, blk=blk)
    pair = pl.BlockSpec((1, T, LANES), lambda b, p: (b, 0, p))
    return pl.pallas_call(
        kern,
        grid=(B, npair),
        in_specs=[pair, pair, pair, pl.BlockSpec((1, 1, HEADS_PER_BLOCK, T), lambda b, p: (b, p, 0, 0))],
        out_specs=pair,
        out_shape=jax.ShapeDtypeStruct((B, T, A), BF16),
        scratch_shapes=[
            pltpu.VMEM((HEADS_PER_BLOCK, T), F32),
            pltpu.VMEM((T, 2 * LANES), BF16),
            pltpu.VMEM((2, HEADS_PER_BLOCK * blk, T), F32),
            pltpu.VMEM((2, HEADS_PER_BLOCK * blk, T), BF16),
        ],
        compiler_params=_params(("parallel", "parallel")),
        name="attn_prompt",
    )(q, k, v, lf)


def _attn_sample_kernel(q_ref, kn_ref, vn_ref, ckt_ref, cvt_ref, clf_ref, lf_ref, o_ref):
    tt = q_ref.shape[1]
    f_past = LOG2E * _cumsum_lanes(clf_ref[0, 0])
    f_new = f_past[:, -1:] + LOG2E * _cumsum_lanes(lf_ref[0, 0])
    row = lax.broadcasted_iota(jnp.int32, (tt, tt), 0)
    col = lax.broadcasted_iota(jnp.int32, (tt, tt), 1)
    nt = (((1,), (1,)), ((), ()))
    outs = []
    for j in range(ckt_ref.shape[1]):
        sl = slice(j * HEAD_DIM, (j + 1) * HEAD_DIM)
        qj = q_ref[0][:, sl]
        s_p = jnp.dot(qj, ckt_ref[0, j].astype(BF16), preferred_element_type=F32) - f_past[j:j + 1]
        s_n = lax.dot_general(qj, kn_ref[0][:, sl], nt, preferred_element_type=F32) - f_new[j:j + 1]
        s_n = jnp.where(col <= row, s_n, -jnp.inf)
        m = jnp.maximum(jnp.max(s_p, axis=-1, keepdims=True), jnp.max(s_n, axis=-1, keepdims=True))
        p_p = jnp.exp2(s_p - m)
        p_n = jnp.exp2(s_n - m)
        l = jnp.sum(p_p, axis=-1, keepdims=True) + jnp.sum(p_n, axis=-1, keepdims=True)
        o = (lax.dot_general(p_p.astype(BF16), cvt_ref[0, j].astype(BF16), nt, preferred_element_type=F32)
             + jnp.dot(p_n.astype(BF16), vn_ref[0][:, sl], preferred_element_type=F32))
        outs.append(o * (1.0 / l))
    o_ref[0] = jnp.concatenate(outs, axis=-1).astype(o_ref.dtype)


def _attn_sample(q, k, v, cache_kt, cache_vt, cache_logf, logf):
    B, T, A = q.shape
    P = cache_kt.shape[3]
    hg = SAMPLE_HEAD_GROUP
    ngrp = N_HEADS // hg
    clf = cache_logf.reshape(B, ngrp, hg, P)
    lf = logf.reshape(B, ngrp, hg, T)
    tok = lambda b, p: (b, 0, p)
    grp = lambda b, p: (b, p, 0, 0)
    return pl.pallas_call(
        _attn_sample_kernel,
        grid=(B, ngrp),
        in_specs=[
            pl.BlockSpec((1, T, hg * HEAD_DIM), tok),
            pl.BlockSpec((1, T, hg * HEAD_DIM), tok),
            pl.BlockSpec((1, T, hg * HEAD_DIM), tok),
            pl.BlockSpec((1, hg, HEAD_DIM, P), grp),
            pl.BlockSpec((1, hg, HEAD_DIM, P), grp),
            pl.BlockSpec((1, 1, hg, P), grp),
            pl.BlockSpec((1, 1, hg, T), grp),
        ],
        out_specs=pl.BlockSpec((1, T, hg * HEAD_DIM), tok),
        out_shape=jax.ShapeDtypeStruct((B, T, A), BF16),
        compiler_params=_params(("parallel", "parallel")),
        name="attn_sample",
    )(q, k, v, cache_kt, cache_vt, clf, lf)


def _mix_kernel(x_ref, o_ref, mod_ref, g_ref, w5_ref, cw_ref, pu_ref, wo_ref, y_ref, nu_ref, carry_scr):
    bb, tt, d = x_ref.shape
    tm = bb * tt
    t = pl.program_id(1)

    @pl.when(t == 0)
    def _():
        carry_scr[...] = pu_ref[...]

    x = x_ref[...]
    shift = mod_ref[:, 3:4, :]
    scale = mod_ref[:, 4:5, :]
    gate = mod_ref[:, 5:6, :]
    h = _rms_mod(x, g_ref[...], scale, shift).reshape(tm, d).astype(BF16)
    cdim = w5_ref.shape[1] // 5
    proj = lambda i: jnp.dot(h, w5_ref[:, i * cdim:(i + 1) * cdim], preferred_element_type=F32)

    u2 = proj(1) * proj(2)
    u = u2.reshape(bb, tt, cdim)
    tpos = lax.broadcasted_iota(jnp.int32, (1, tt, 1), 1)
    c0 = carry_scr[:, 0:1, :]
    c1 = carry_scr[:, 1:2, :]
    um1 = jnp.where(tpos == 0, c1, pltpu.roll(u2, 1, axis=0).reshape(bb, tt, cdim))
    um2 = jnp.where(tpos == 0, c0, jnp.where(tpos == 1, c1, pltpu.roll(u2, 2, axis=0).reshape(bb, tt, cdim)))
    conv = cw_ref[0:1, :] * um2 + cw_ref[1:2, :] * um1 + cw_ref[2:3, :] * u
    new_u = u[:, tt - (CONV_WIDTH - 1):, :]
    carry_scr[...] = new_u
    nu_ref[...] = new_u

    o_conv = proj(0) * conv.reshape(tm, cdim)
    m = _sigmoid(proj(3)) * o_ref[...].reshape(tm, cdim).astype(F32) + _sigmoid(proj(4)) * o_conv
    mo = jnp.dot(m.astype(BF16), wo_ref[...], preferred_element_type=F32)
    y_ref[...] = x + gate * mo.reshape(bb, tt, d)


def _mix(x, o_attn, mod, g, w5, conv_w, past_u, w_out, *, bb, tt):
    B, T, D = x.shape
    C = w5.shape[1] // 5
    tok = lambda b, t: (b, t, 0)
    per_b = lambda b, t: (b, 0, 0)
    return pl.pallas_call(
        _mix_kernel,
        grid=(B // bb, T // tt),
        in_specs=[
            pl.BlockSpec((bb, tt, D), tok),
            pl.BlockSpec((bb, tt, C), tok),
            pl.BlockSpec((bb, N_MOD, D), per_b),
            _resident((1, D)),
            _resident(w5.shape),
            _resident(conv_w.shape),
            pl.BlockSpec((bb, CONV_WIDTH - 1, C), per_b),
            _resident(w_out.shape),
        ],
        out_specs=[
            pl.BlockSpec((bb, tt, D), tok),
            pl.BlockSpec((bb, CONV_WIDTH - 1, C), per_b),
        ],
        out_shape=[
            jax.ShapeDtypeStruct((B, T, D), F32),
            jax.ShapeDtypeStruct((B, CONV_WIDTH - 1, C), F32),
        ],
        scratch_shapes=[pltpu.VMEM((bb, CONV_WIDTH - 1, C), F32)],
        compiler_params=_params(("parallel", "arbitrary")),
        name="mix",
    )(x, o_attn, mod, g, w5, conv_w, past_u, w_out)


def _layer(x, mod, past, weights, g_final, *, bb, tt):
    (g1, w1a, w1b, gm, wqkv, wf, bf, w5, conv_w, w_out, g2, w2a, w2b) = weights
    past_kt, past_vt, past_logf, past_u = past
    prompt = past_kt is None
    x1 = _ffn(x, mod, g1, w1a, w1b, g_final, bb=bb, tt=tt, mod_base=0, final_norm=False)
    q, kb, vb, k32, v32, logf = _qkv(x1, mod, gm, wqkv, wf, bf, bb=bb, tt=tt, kv_transposed=prompt)
    if prompt:
        o = _attn_prompt(q, kb, vb, logf, blk=ATTN_BLOCK)
        k32, v32 = jnp.swapaxes(k32, 2, 3), jnp.swapaxes(v32, 2, 3)
    else:
        o = _attn_sample(q, kb, vb, past_kt, past_vt, past_logf, logf)
    x2, new_u = _mix(x1, o, mod, gm, w5, conv_w, past_u, w_out, bb=bb, tt=tt)
    y = _ffn(x2, mod, g2, w2a, w2b, g_final, bb=bb, tt=tt, mod_base=6, final_norm=True)
    return y, k32, v32, logf, new_u


def kernel(x_prompt, x_sample, cache_k, cache_v, cache_logf, state_conv, c_prompt, c_sample, w_ada, b_ada, g_ffn1, w_ffn1_in, w_ffn1_out, g_mix, w_in, b_f, conv_w, w_out, g_ffn2, w_ffn2_in, w_ffn2_out, g_final):
    assert w_ada.shape[0] == 1, "single-layer encoder"
    Bp, Tp, D = x_prompt.shape
    Bs, Ts, _ = x_sample.shape
    A = N_HEADS * HEAD_DIM
    C = conv_w.shape[2]

    mod = _ada(jnp.concatenate([c_prompt, c_sample], axis=0), w_ada[0], b_ada[0])
    mod = mod.reshape(Bp + Bs, N_MOD, D)
    mod_p, mod_s = mod[:Bp], mod[Bp:]

    wi = w_in[0]
    off_f = 3 * A
    off_b = off_f + N_HEADS
    wqkv = wi[:, :off_f].astype(BF16)
    wf = jnp.pad(wi[:, off_f:off_b], ((0, 0), (0, LANES - N_HEADS))).astype(BF16)
    bf = jnp.pad(b_f[0], (0, LANES - N_HEADS)).reshape(1, LANES)
    w5 = wi[:, off_b:].astype(BF16)
    weights = (g_ffn1, w_ffn1_in[0].astype(BF16), w_ffn1_out[0].astype(BF16), g_mix, wqkv, wf, bf, w5,
               conv_w[0], w_out[0].astype(BF16), g_ffn2, w_ffn2_in[0].astype(BF16), w_ffn2_out[0].astype(BF16))
    gfin = g_final.reshape(1, D)

    zero_u = jnp.zeros((Bp, CONV_WIDTH - 1, C), F32)
    yp, kp, vp, fp, up = _layer(x_prompt, mod_p, (None, None, None, zero_u), weights, gfin,
                                bb=1, tt=ROW_TILE)
    past = (jnp.swapaxes(cache_k[0], 2, 3), jnp.swapaxes(cache_v[0], 2, 3), cache_logf[0], state_conv[0])
    ys, ks, vs, fs, us = _layer(x_sample, mod_s, past, weights, gfin, bb=ROW_TILE // Ts, tt=Ts)
    return (yp, ys, kp[None], vp[None], fp[None], up[None], ks[None], vs[None], fs[None], us[None])
```

```python
import functools

import jax
import jax.numpy as jnp
from jax import lax
from jax.experimental import pallas as pl
from jax.experimental.pallas import tpu as pltpu

F32 = jnp.float32
BF16 = jnp.bfloat16

EPS = 1e-6
N_HEADS = 16
HEAD_DIM = 64
N_MOD = 9
CONV_WIDTH = 3
LANES = 128
HEADS_PER_BLOCK = LANES // HEAD_DIM
V7X_VMEM_LIMIT_BYTES = 56 * 1024 * 1024
FFN_CHUNK = 256
ROW_TILE = 512
ATTN_BLOCK = 256
ATTN_ROWS = 64
LOG2E = 1.4426950408889634
Q_SCALE = HEAD_DIM ** -0.5 * LOG2E
SAMPLE_HEAD_GROUP = 8
SAMPLE_STACK = 4


def _params(semantics):
    return pltpu.CompilerParams(dimension_semantics=semantics, vmem_limit_bytes=V7X_VMEM_LIMIT_BYTES)


def _resident(shape):
    zeros = (0,) * len(shape)
    return pl.BlockSpec(shape, lambda *_: zeros, pipeline_mode=pl.Buffered(1))


def _rms_mod(x, g, scale, shift):
    ms = jnp.mean(x * x, axis=-1, keepdims=True)
    y = x * lax.rsqrt(ms + EPS) * g
    return y * (1.0 + scale) + shift


def _sigmoid(x):
    return 1.0 / (1.0 + jnp.exp(-x))


def _log_sigmoid(x):
    return jnp.minimum(x, 0.0) - jnp.log1p(jnp.exp(-jnp.abs(x)))


def _cumsum_lanes(x):
    n = x.shape[-1]
    lane = lax.broadcasted_iota(jnp.int32, x.shape, x.ndim - 1)
    step = 1
    while step < n:
        x = x + jnp.where(lane >= step, pltpu.roll(x, step, axis=x.ndim - 1), 0.0)
        step *= 2
    return x


def _ada_kernel(c_ref, w_ref, b_ref, o_ref):
    c = c_ref[...]
    a = (c * _sigmoid(c)).astype(BF16)
    o_ref[...] = jnp.dot(a, w_ref[...].astype(BF16), preferred_element_type=F32) + b_ref[...]


def _ada(c, w_ada, b_ada):
    n, d = c.shape
    cols = w_ada.shape[1]
    tn = d
    return pl.pallas_call(
        _ada_kernel,
        grid=(cols // tn,),
        in_specs=[
            pl.BlockSpec((n, d), lambda j: (0, 0)),
            pl.BlockSpec((d, tn), lambda j: (0, j)),
            pl.BlockSpec((1, tn), lambda j: (0, j)),
        ],
        out_specs=pl.BlockSpec((n, tn), lambda j: (0, j)),
        out_shape=jax.ShapeDtypeStruct((n, cols), F32),
        compiler_params=_params(("parallel",)),
        name="ada_mod",
    )(c, w_ada, b_ada.reshape(1, cols))


def _ffn_kernel(x_ref, mod_ref, g_ref, w1_ref, w2_ref, gfin_ref, o_ref, h_scr, *, mod_base, final_norm):
    bb, tt, d = x_ref.shape
    tm = bb * tt
    f = w2_ref.shape[0]
    fc = FFN_CHUNK
    x = x_ref[...]
    shift = mod_ref[:, mod_base:mod_base + 1, :]
    scale = mod_ref[:, mod_base + 1:mod_base + 2, :]
    gate = mod_ref[:, mod_base + 2:mod_base + 3, :]
    h_scr[...] = _rms_mod(x, g_ref[...], scale, shift).reshape(tm, d).astype(BF16)

    acc = None
    for lo in range(0, f, fc):
        h = h_scr[...]
        a = jnp.dot(h, w1_ref[:, lo:lo + fc], preferred_element_type=F32)
        b = jnp.dot(h, w1_ref[:, f + lo:f + lo + fc], preferred_element_type=F32)
        act = (a * _sigmoid(a) * b).astype(BF16)
        part = jnp.dot(act, w2_ref[lo:lo + fc, :], preferred_element_type=F32)
        acc = part if acc is None else acc + part

    y = x + 0.5 * gate * acc.reshape(bb, tt, d)
    if final_norm:
        ms = jnp.mean(y * y, axis=-1, keepdims=True)
        y = y * lax.rsqrt(ms + EPS) * gfin_ref[...]
    o_ref[...] = y


def _ffn(x, mod, g, w1c, w2c, g_final, *, bb, tt, mod_base, final_norm):
    B, T, D = x.shape
    assert w2c.shape[0] % FFN_CHUNK == 0
    tm = bb * tt
    kern = functools.partial(_ffn_kernel, mod_base=mod_base, final_norm=final_norm)
    return pl.pallas_call(
        kern,
        grid=(B // bb, T // tt),
        in_specs=[
            pl.BlockSpec((bb, tt, D), lambda b, t: (b, t, 0)),
            pl.BlockSpec((bb, N_MOD, D), lambda b, t: (b, 0, 0)),
            _resident((1, D)),
            _resident(w1c.shape),
            _resident(w2c.shape),
            _resident((1, D)),
        ],
        out_specs=pl.BlockSpec((bb, tt, D), lambda b, t: (b, t, 0)),
        out_shape=jax.ShapeDtypeStruct((B, T, D), F32),
        scratch_shapes=[pltpu.VMEM((tm, D), BF16)],
        compiler_params=_params(("parallel", "parallel")),
        name="ffn_final" if final_norm else "ffn",
    )(x, mod, g, w1c, w2c, g_final)


def _qkv_kernel(*refs, prompt):
    if prompt:
        (x_ref, mod_ref, g_ref, wqkv_ref, wf_ref, bf_ref,
         q_ref, kb_ref, vb_ref, k32_ref, v32_ref, lf_ref) = refs
    else:
        (x_ref, mod_ref, g_ref, wqkv_ref, wf_ref, bf_ref, clf_ref,
         q_ref, kb_ref, vb_ref, k32_ref, v32_ref, lf_ref, fp_ref, fn_ref) = refs
    bb, tt, d = x_ref.shape
    tm = bb * tt
    a_dim = wqkv_ref.shape[1] // 3
    shift = mod_ref[:, 3:4, :]
    scale = mod_ref[:, 4:5, :]
    h = _rms_mod(x_ref[...], g_ref[...], scale, shift).reshape(tm, d).astype(BF16)

    zq = jnp.dot(h, wqkv_ref[:, 0:a_dim], preferred_element_type=F32)
    q_ref[...] = (zq * Q_SCALE).astype(BF16).reshape(bb, tt, a_dim)

    for w_idx, (lo_ref, hi_ref) in ((1, (kb_ref, k32_ref)), (2, (vb_ref, v32_ref))):
        z = jnp.dot(h, wqkv_ref[:, w_idx * a_dim:(w_idx + 1) * a_dim], preferred_element_type=F32)
        if prompt:
            zt = z.T
            hi_ref[0] = zt.reshape(N_HEADS, HEAD_DIM, tt)
            if w_idx == 1:
                lo_ref[0] = zt.astype(BF16)
            else:
                lo_ref[...] = z.astype(BF16).reshape(bb, tt, a_dim)
        else:
            lo_ref[...] = z.astype(BF16).reshape(bb, tt, a_dim)
            for hd in range(N_HEADS):
                hi_ref[:, hd, :, :] = z[:, hd * HEAD_DIM:(hd + 1) * HEAD_DIM].reshape(bb, tt, HEAD_DIM)

    zf = jnp.dot(h, wf_ref[...], preferred_element_type=F32)
    lf = _log_sigmoid(zf + bf_ref[...]).T
    if not prompt:
        p_len = clf_ref.shape[2]
        f_past = LOG2E * _cumsum_lanes(clf_ref[...].reshape(bb * N_HEADS, p_len))
        fp_ref[...] = f_past.reshape(bb, N_HEADS, p_len)
    for b in range(bb):
        lf_b = lf[:N_HEADS, b * tt:(b + 1) * tt]
        lf_ref[b] = lf_b
        if not prompt:
            fn_ref[b] = f_past[b * N_HEADS:(b + 1) * N_HEADS, p_len - 1:p_len] + LOG2E * _cumsum_lanes(lf_b)


def _qkv(x, mod, g, wqkv, wf, bf, cache_logf, *, bb, tt):
    B, T, D = x.shape
    A = wqkv.shape[1] // 3
    prompt = cache_logf is None
    tok = lambda b, t: (b, t, 0)
    per_b = lambda b, t: (b, 0, 0)
    in_specs = [
        pl.BlockSpec((bb, tt, D), tok),
        pl.BlockSpec((bb, N_MOD, D), per_b),
        _resident((1, D)),
        _resident(wqkv.shape),
        _resident(wf.shape),
        _resident(bf.shape),
    ]
    tok_spec = pl.BlockSpec((bb, tt, A), tok)
    tok_shape = jax.ShapeDtypeStruct((B, T, A), BF16)
    lf_spec = pl.BlockSpec((bb, N_HEADS, tt), lambda b, t: (b, 0, t))
    lf_shape = jax.ShapeDtypeStruct((B, N_HEADS, T), F32)
    if prompt:
        assert bb == 1
        kv_spec = pl.BlockSpec((1, N_HEADS, HEAD_DIM, tt), lambda b, t: (b, 0, 0, t))
        kv_shape = jax.ShapeDtypeStruct((B, N_HEADS, HEAD_DIM, T), F32)
        out_specs = [tok_spec, pl.BlockSpec((1, A, tt), lambda b, t: (b, 0, t)), tok_spec, kv_spec, kv_spec, lf_spec]
        out_shape = [tok_shape, jax.ShapeDtypeStruct((B, A, T), BF16), tok_shape, kv_shape, kv_shape, lf_shape]
        args = (x, mod, g, wqkv, wf, bf)
    else:
        assert T == tt
        P = cache_logf.shape[2]
        kv_spec = pl.BlockSpec((bb, N_HEADS, tt, HEAD_DIM), lambda b, t: (b, 0, t, 0))
        kv_shape = jax.ShapeDtypeStruct((B, N_HEADS, T, HEAD_DIM), F32)
        in_specs.append(pl.BlockSpec((bb, N_HEADS, P), per_b))
        out_specs = [tok_spec, tok_spec, tok_spec, kv_spec, kv_spec, lf_spec,
                     pl.BlockSpec((bb, N_HEADS, P), per_b), lf_spec]
        out_shape = [tok_shape, tok_shape, tok_shape, kv_shape, kv_shape, lf_shape,
                     jax.ShapeDtypeStruct((B, N_HEADS, P), F32), lf_shape]
        args = (x, mod, g, wqkv, wf, bf, cache_logf)
    return pl.pallas_call(
        functools.partial(_qkv_kernel, prompt=prompt),
        grid=(B // bb, T // tt),
        in_specs=in_specs,
        out_specs=out_specs,
        out_shape=out_shape,
        compiler_params=_params(("parallel", "parallel")),
        name="qkv_proj",
    )(*args)


def _attn_rows(q2, kt_ref, v1_ref, negf_ref, s_scr, p_scr, nk, blk):
    rows = HEADS_PER_BLOCK * blk
    half = blk // 2
    nkeys = nk * blk
    keep = lax.broadcasted_iota(jnp.int32, (blk, blk), 1) <= lax.broadcasted_iota(jnp.int32, (blk, blk), 0)
    s = jnp.dot(q2, kt_ref[0, :, 0:nkeys], preferred_element_type=F32)
    for kb in range(nk):
        ks = slice(kb * blk, (kb + 1) * blk)
        for h in range(HEADS_PER_BLOCK):
            t = s[h * blk:(h + 1) * blk, ks] + negf_ref[h:h + 1, ks]
            if kb == nk - 1:
                t = jnp.where(keep, t, -jnp.inf)
            s_scr[h * blk:(h + 1) * blk, ks] = t

    for r0 in range(0, rows, ATTN_ROWS):
        rs = slice(r0, r0 + ATTN_ROWS)
        mx = None
        for kb in range(nk):
            t = s_scr[rs, kb * blk:(kb + 1) * blk]
            t = jnp.maximum(t[:, :half], t[:, half:])
            mx = t if mx is None else jnp.maximum(mx, t)
        mb = jnp.broadcast_to(jnp.max(mx, axis=-1, keepdims=True), (ATTN_ROWS, blk))
        for kb in range(nk):
            ks = slice(kb * blk, (kb + 1) * blk)
            p_scr[rs, ks] = jnp.exp2(s_scr[rs, ks] - mb).astype(BF16)

    acc = jnp.dot(p_scr[:, 0:nkeys], v1_ref[0:nkeys, :], preferred_element_type=F32)
    return acc[:, :LANES] * (1.0 / acc[:, LANES:])


def _attn_prompt_kernel(q_ref, kt_ref, v_ref, lf_ref, o_ref, negf_scr, v1_scr, s_scr, p_scr, *, blk):
    T = v_ref.shape[1]
    negf_scr[...] = -LOG2E * _cumsum_lanes(lf_ref[0, 0])
    v1_scr[:, :LANES] = v_ref[0]
    v1_scr[:, LANES:] = jnp.ones((T, LANES), v1_scr.dtype)
    lane = lax.broadcasted_iota(jnp.int32, (blk, LANES), 1)
    for c in range(T // blk):
        q = q_ref[0, c * blk:(c + 1) * blk, :]
        zero = jnp.zeros_like(q)
        q2 = jnp.concatenate([jnp.where(lane < HEAD_DIM, q, zero), jnp.where(lane >= HEAD_DIM, q, zero)], axis=0)
        o = _attn_rows(q2, kt_ref, v1_scr, negf_scr, s_scr.at[c % 2], p_scr.at[c % 2], c + 1, blk)
        o_ref[0, c * blk:(c + 1) * blk, :] = jnp.where(lane < HEAD_DIM, o[:blk], o[blk:]).astype(o_ref.dtype)


def _attn_prompt(q, kt, v, logf, *, blk):
    B, T, A = q.shape
    npair = A // LANES
    lf = logf.reshape(B, npair, HEADS_PER_BLOCK, T)
    kern = functools.partial(_attn_prompt_kernel, blk=blk)
    pair = pl.BlockSpec((1, T, LANES), lambda b, p: (b, 0, p))
    pair_t = pl.BlockSpec((1, LANES, T), lambda b, p: (b, p, 0))
    return pl.pallas_call(
        kern,
        grid=(B, npair),
        in_specs=[pair, pair_t, pair, pl.BlockSpec((1, 1, HEADS_PER_BLOCK, T), lambda b, p: (b, p, 0, 0))],
        out_specs=pair,
        out_shape=jax.ShapeDtypeStruct((B, T, A), BF16),
        scratch_shapes=[
            pltpu.VMEM((HEADS_PER_BLOCK, T), F32),
            pltpu.VMEM((T, 2 * LANES), BF16),
            pltpu.VMEM((2, HEADS_PER_BLOCK * blk, T), F32),
            pltpu.VMEM((2, HEADS_PER_BLOCK * blk, T), BF16),
        ],
        compiler_params=_params(("parallel", "parallel")),
        name="attn_prompt",
    )(q, kt, v, lf)


def _attn_sample_kernel(q_ref, kn_ref, vn_ref, ckt_ref, cvt_ref, fp_ref, fn_ref, o_ref):
    tt = q_ref.shape[1]
    p_len = ckt_ref.shape[3]
    width = SAMPLE_STACK * HEAD_DIM
    nt = (((1,), (1,)), ((), ()))
    lane_head = lax.broadcasted_iota(jnp.int32, (tt, width), 1) // HEAD_DIM
    keep = lax.broadcasted_iota(jnp.int32, (tt, tt), 1) <= lax.broadcasted_iota(jnp.int32, (tt, tt), 0)
    for g in range(ckt_ref.shape[1] // SAMPLE_STACK):
        heads = range(g * SAMPLE_STACK, (g + 1) * SAMPLE_STACK)
        lanes = slice(g * width, (g + 1) * width)
        q = q_ref[0][:, lanes]
        zero = jnp.zeros_like(q)
        qs = jnp.concatenate([jnp.where(lane_head == i, q, zero) for i in range(SAMPLE_STACK)], axis=0)
        kt = ckt_ref[0, heads.start:heads.stop].reshape(width, p_len).astype(BF16)
        vt = cvt_ref[0, heads.start:heads.stop].reshape(width, p_len).astype(BF16)
        s_p = jnp.dot(qs, kt, preferred_element_type=F32)
        s_n = lax.dot_general(qs, kn_ref[0][:, lanes], nt, preferred_element_type=F32)
        s_p = jnp.concatenate([s_p[i * tt:(i + 1) * tt] - fp_ref[0, 0, hd:hd + 1, :]
                               for i, hd in enumerate(heads)], axis=0)
        s_n = jnp.concatenate([jnp.where(keep, s_n[i * tt:(i + 1) * tt] - fn_ref[0, 0, hd:hd + 1, :], -jnp.inf)
                               for i, hd in enumerate(heads)], axis=0)
        m = jnp.maximum(jnp.max(s_p, axis=-1, keepdims=True), jnp.max(s_n, axis=-1, keepdims=True))
        p_p = jnp.exp2(s_p - m)
        p_n = jnp.exp2(s_n - m)
        l = jnp.sum(p_p, axis=-1, keepdims=True) + jnp.sum(p_n, axis=-1, keepdims=True)
        o = (lax.dot_general(p_p.astype(BF16), vt, nt, preferred_element_type=F32)
             + jnp.dot(p_n.astype(BF16), vn_ref[0][:, lanes], preferred_element_type=F32)) * (1.0 / l)
        out = jnp.zeros((tt, width), F32)
        for i in range(SAMPLE_STACK):
            out = jnp.where(lane_head == i, o[i * tt:(i + 1) * tt], out)
        o_ref[0, :, lanes] = out.astype(o_ref.dtype)


def _attn_sample(q, k, v, cache_kt, cache_vt, f_past, f_new):
    B, T, A = q.shape
    P = cache_kt.shape[3]
    hg = SAMPLE_HEAD_GROUP
    ngrp = N_HEADS // hg
    clf = f_past.reshape(B, ngrp, hg, P)
    lf = f_new.reshape(B, ngrp, hg, T)
    tok = lambda b, p: (b, 0, p)
    grp = lambda b, p: (b, p, 0, 0)
    return pl.pallas_call(
        _attn_sample_kernel,
        grid=(B, ngrp),
        in_specs=[
            pl.BlockSpec((1, T, hg * HEAD_DIM), tok),
            pl.BlockSpec((1, T, hg * HEAD_DIM), tok),
            pl.BlockSpec((1, T, hg * HEAD_DIM), tok),
            pl.BlockSpec((1, hg, HEAD_DIM, P), grp),
            pl.BlockSpec((1, hg, HEAD_DIM, P), grp),
            pl.BlockSpec((1, 1, hg, P), grp),
            pl.BlockSpec((1, 1, hg, T), grp),
        ],
        out_specs=pl.BlockSpec((1, T, hg * HEAD_DIM), tok),
        out_shape=jax.ShapeDtypeStruct((B, T, A), BF16),
        compiler_params=_params(("parallel", "parallel")),
        name="attn_sample",
    )(q, k, v, cache_kt, cache_vt, clf, lf)


def _mix_kernel(x_ref, o_ref, mod_ref, g_ref, w5_ref, cw_ref, pu_ref, wo_ref, y_ref, nu_ref, carry_scr):
    bb, tt, d = x_ref.shape
    tm = bb * tt
    t = pl.program_id(1)

    @pl.when(t == 0)
    def _():
        carry_scr[...] = pu_ref[...]

    x = x_ref[...]
    shift = mod_ref[:, 3:4, :]
    scale = mod_ref[:, 4:5, :]
    gate = mod_ref[:, 5:6, :]
    h = _rms_mod(x, g_ref[...], scale, shift).reshape(tm, d).astype(BF16)
    cdim = w5_ref.shape[1] // 5
    proj = lambda i: jnp.dot(h, w5_ref[:, i * cdim:(i + 1) * cdim], preferred_element_type=F32)

    u2 = proj(1) * proj(2)
    u = u2.reshape(bb, tt, cdim)
    tpos = lax.broadcasted_iota(jnp.int32, (1, tt, 1), 1)
    c0 = carry_scr[:, 0:1, :]
    c1 = carry_scr[:, 1:2, :]
    um1 = jnp.where(tpos == 0, c1, pltpu.roll(u2, 1, axis=0).reshape(bb, tt, cdim))
    um2 = jnp.where(tpos == 0, c0, jnp.where(tpos == 1, c1, pltpu.roll(u2, 2, axis=0).reshape(bb, tt, cdim)))
    conv = cw_ref[0:1, :] * um2 + cw_ref[1:2, :] * um1 + cw_ref[2:3, :] * u
    new_u = u[:, tt - (CONV_WIDTH - 1):, :]
    carry_scr[...] = new_u
    nu_ref[...] = new_u

    o_conv = proj(0) * conv.reshape(tm, cdim)
    m = _sigmoid(proj(3)) * o_ref[...].reshape(tm, cdim).astype(F32) + _sigmoid(proj(4)) * o_conv
    mo = jnp.dot(m.astype(BF16), wo_ref[...], preferred_element_type=F32)
    y_ref[...] = x + gate * mo.reshape(bb, tt, d)


def _mix(x, o_attn, mod, g, w5, conv_w, past_u, w_out, *, bb, tt):
    B, T, D = x.shape
    C = w5.shape[1] // 5
    tok = lambda b, t: (b, t, 0)
    per_b = lambda b, t: (b, 0, 0)
    return pl.pallas_call(
        _mix_kernel,
        grid=(B // bb, T // tt),
        in_specs=[
            pl.BlockSpec((bb, tt, D), tok),
            pl.BlockSpec((bb, tt, C), tok),
            pl.BlockSpec((bb, N_MOD, D), per_b),
            _resident((1, D)),
            _resident(w5.shape),
            _resident(conv_w.shape),
            pl.BlockSpec((bb, CONV_WIDTH - 1, C), per_b),
            _resident(w_out.shape),
        ],
        out_specs=[
            pl.BlockSpec((bb, tt, D), tok),
            pl.BlockSpec((bb, CONV_WIDTH - 1, C), per_b),
        ],
        out_shape=[
            jax.ShapeDtypeStruct((B, T, D), F32),
            jax.ShapeDtypeStruct((B, CONV_WIDTH - 1, C), F32),
        ],
        scratch_shapes=[pltpu.VMEM((bb, CONV_WIDTH - 1, C), F32)],
        compiler_params=_params(("parallel", "arbitrary")),
        name="mix",
    )(x, o_attn, mod, g, w5, conv_w, past_u, w_out)


def _layer(x, mod, past, weights, g_final, *, bb, tt):
    (g1, w1a, w1b, gm, wqkv, wf, bf, w5, conv_w, w_out, g2, w2a, w2b) = weights
    past_kt, past_vt, past_logf, past_u = past
    prompt = past_kt is None
    x1 = _ffn(x, mod, g1, w1a, w1b, g_final, bb=bb, tt=tt, mod_base=0, final_norm=False)
    if prompt:
        q, kbt, vb, k32, v32, logf = _qkv(x1, mod, gm, wqkv, wf, bf, None, bb=bb, tt=tt)
        o = _attn_prompt(q, kbt, vb, logf, blk=ATTN_BLOCK)
        k32, v32 = jnp.swapaxes(k32, 2, 3), jnp.swapaxes(v32, 2, 3)
    else:
        q, kb, vb, k32, v32, logf, f_past, f_new = _qkv(x1, mod, gm, wqkv, wf, bf, past_logf, bb=bb, tt=tt)
        o = _attn_sample(q, kb, vb, past_kt, past_vt, f_past, f_new)
    x2, new_u = _mix(x1, o, mod, gm, w5, conv_w, past_u, w_out, bb=bb, tt=tt)
    y = _ffn(x2, mod, g2, w2a, w2b, g_final, bb=bb, tt=tt, mod_base=6, final_norm=True)
    return y, k32, v32, logf, new_u


def kernel(x_prompt, x_sample, cache_k, cache_v, cache_logf, state_conv, c_prompt, c_sample, w_ada, b_ada, g_ffn1, w_ffn1_in, w_ffn1_out, g_mix, w_in, b_f, conv_w, w_out, g_ffn2, w_ffn2_in, w_ffn2_out, g_final):
    assert w_ada.shape[0] == 1, "single-layer encoder"
    Bp, Tp, D = x_prompt.shape
    Bs, Ts, _ = x_sample.shape
    A = N_HEADS * HEAD_DIM
    C = conv_w.shape[2]

    mod = _ada(jnp.concatenate([c_prompt, c_sample], axis=0), w_ada[0], b_ada[0])
    mod = mod.reshape(Bp + Bs, N_MOD, D)
    mod_p, mod_s = mod[:Bp], mod[Bp:]

    wi = w_in[0]
    off_f = 3 * A
    off_b = off_f + N_HEADS
    wqkv = wi[:, :off_f].astype(BF16)
    wf = jnp.pad(wi[:, off_f:off_b], ((0, 0), (0, LANES - N_HEADS))).astype(BF16)
    bf = jnp.pad(b_f[0], (0, LANES - N_HEADS)).reshape(1, LANES)
    w5 = wi[:, off_b:].astype(BF16)
    weights = (g_ffn1, w_ffn1_in[0].astype(BF16), w_ffn1_out[0].astype(BF16), g_mix, wqkv, wf, bf, w5,
               conv_w[0], w_out[0].astype(BF16), g_ffn2, w_ffn2_in[0].astype(BF16), w_ffn2_out[0].astype(BF16))
    gfin = g_final.reshape(1, D)

    zero_u = jnp.zeros((Bp, CONV_WIDTH - 1, C), F32)
    yp, kp, vp, fp, up = _layer(x_prompt, mod_p, (None, None, None, zero_u), weights, gfin,
                                bb=1, tt=ROW_TILE)
    past = (jnp.swapaxes(cache_k[0], 2, 3), jnp.swapaxes(cache_v[0], 2, 3), cache_logf[0], state_conv[0])
    ys, ks, vs, fs, us = _layer(x_sample, mod_s, past, weights, gfin, bb=ROW_TILE // Ts, tt=Ts)
    return (yp, ys, kp[None], vp[None], fp[None], up[None], ks[None], vs[None], fs[None], us[None])
```

```python
import functools

import jax
import jax.numpy as jnp
from jax import lax
from jax.experimental import pallas as pl
from jax.experimental.pallas import tpu as pltpu

F32 = jnp.float32
BF16 = jnp.bfloat16

EPS = 1e-6
N_HEADS = 16
HEAD_DIM = 64
N_MOD = 9
CONV_WIDTH = 3
LANES = 128
HEADS_PER_BLOCK = LANES // HEAD_DIM
V7X_VMEM_LIMIT_BYTES = 56 * 1024 * 1024
FFN_CHUNK = 256
ROW_TILE = 512
ATTN_BLOCK = 256
ATTN_ROWS = 64
ATTN_PAIRS = 2
ATTN_SLOTS = 4
LOG2E = 1.4426950408889634
Q_SCALE = HEAD_DIM ** -0.5 * LOG2E
SAMPLE_HEAD_GROUP = 8
SAMPLE_STACK = 4


def _params(semantics):
    return pltpu.CompilerParams(dimension_semantics=semantics, vmem_limit_bytes=V7X_VMEM_LIMIT_BYTES)


def _resident(shape):
    zeros = (0,) * len(shape)
    return pl.BlockSpec(shape, lambda *_: zeros, pipeline_mode=pl.Buffered(1))


def _rms_mod(x, g, scale, shift):
    ms = jnp.mean(x * x, axis=-1, keepdims=True)
    y = x * lax.rsqrt(ms + EPS) * g
    return y * (1.0 + scale) + shift


def _sigmoid(x):
    return 1.0 / (1.0 + jnp.exp(-x))


def _log_sigmoid(x):
    return jnp.minimum(x, 0.0) - jnp.log1p(jnp.exp(-jnp.abs(x)))


def _cumsum_lanes(x):
    n = x.shape[-1]
    lane = lax.broadcasted_iota(jnp.int32, x.shape, x.ndim - 1)
    step = 1
    while step < n:
        x = x + jnp.where(lane >= step, pltpu.roll(x, step, axis=x.ndim - 1), 0.0)
        step *= 2
    return x


def _ada_kernel(c_ref, w_ref, b_ref, o_ref):
    c = c_ref[...]
    a = (c * _sigmoid(c)).astype(BF16)
    o_ref[...] = jnp.dot(a, w_ref[...].astype(BF16), preferred_element_type=F32) + b_ref[...]


def _ada(c, w_ada, b_ada):
    n, d = c.shape
    cols = w_ada.shape[1]
    tn = d
    return pl.pallas_call(
        _ada_kernel,
        grid=(cols // tn,),
        in_specs=[
            pl.BlockSpec((n, d), lambda j: (0, 0)),
            pl.BlockSpec((d, tn), lambda j: (0, j)),
            pl.BlockSpec((1, tn), lambda j: (0, j)),
        ],
        out_specs=pl.BlockSpec((n, tn), lambda j: (0, j)),
        out_shape=jax.ShapeDtypeStruct((n, cols), F32),
        compiler_params=_params(("parallel",)),
        name="ada_mod",
    )(c, w_ada, b_ada.reshape(1, cols))


def _ffn_kernel(x_ref, mod_ref, g_ref, w1_ref, w2_ref, gfin_ref, o_ref, h_scr, *, mod_base, final_norm):
    bb, tt, d = x_ref.shape
    tm = bb * tt
    f = w2_ref.shape[0]
    fc = FFN_CHUNK
    x = x_ref[...]
    shift = mod_ref[:, mod_base:mod_base + 1, :]
    scale = mod_ref[:, mod_base + 1:mod_base + 2, :]
    gate = mod_ref[:, mod_base + 2:mod_base + 3, :]
    h_scr[...] = _rms_mod(x, g_ref[...], scale, shift).reshape(tm, d).astype(BF16)

    acc = None
    for lo in range(0, f, fc):
        h = h_scr[...]
        a = jnp.dot(h, w1_ref[:, lo:lo + fc], preferred_element_type=F32)
        b = jnp.dot(h, w1_ref[:, f + lo:f + lo + fc], preferred_element_type=F32)
        act = (a * _sigmoid(a) * b).astype(BF16)
        part = jnp.dot(act, w2_ref[lo:lo + fc, :], preferred_element_type=F32)
        acc = part if acc is None else acc + part

    y = x + 0.5 * gate * acc.reshape(bb, tt, d)
    if final_norm:
        ms = jnp.mean(y * y, axis=-1, keepdims=True)
        y = y * lax.rsqrt(ms + EPS) * gfin_ref[...]
    o_ref[...] = y


def _ffn(x, mod, g, w1c, w2c, g_final, *, bb, tt, mod_base, final_norm):
    B, T, D = x.shape
    assert w2c.shape[0] % FFN_CHUNK == 0
    tm = bb * tt
    kern = functools.partial(_ffn_kernel, mod_base=mod_base, final_norm=final_norm)
    return pl.pallas_call(
        kern,
        grid=(B // bb, T // tt),
        in_specs=[
            pl.BlockSpec((bb, tt, D), lambda b, t: (b, t, 0)),
            pl.BlockSpec((bb, N_MOD, D), lambda b, t: (b, 0, 0)),
            _resident((1, D)),
            _resident(w1c.shape),
            _resident(w2c.shape),
            _resident((1, D)),
        ],
        out_specs=pl.BlockSpec((bb, tt, D), lambda b, t: (b, t, 0)),
        out_shape=jax.ShapeDtypeStruct((B, T, D), F32),
        scratch_shapes=[pltpu.VMEM((tm, D), BF16)],
        compiler_params=_params(("parallel", "parallel")),
        name="ffn_final" if final_norm else "ffn",
    )(x, mod, g, w1c, w2c, g_final)


def _qkv_kernel(*refs, prompt):
    if prompt:
        (x_ref, mod_ref, g_ref, wqkv_ref, wf_ref, bf_ref,
         q_ref, kb_ref, vb_ref, k32_ref, v32_ref, lf_ref) = refs
    else:
        (x_ref, mod_ref, g_ref, wqkv_ref, wf_ref, bf_ref, clf_ref,
         q_ref, kb_ref, vb_ref, k32_ref, v32_ref, lf_ref, fp_ref, fn_ref) = refs
    bb, tt, d = x_ref.shape
    tm = bb * tt
    a_dim = wqkv_ref.shape[1] // 3
    shift = mod_ref[:, 3:4, :]
    scale = mod_ref[:, 4:5, :]
    h = _rms_mod(x_ref[...], g_ref[...], scale, shift).reshape(tm, d).astype(BF16)

    zq = jnp.dot(h, wqkv_ref[:, 0:a_dim], preferred_element_type=F32)
    q_ref[...] = (zq * Q_SCALE).astype(BF16).reshape(bb, tt, a_dim)

    for w_idx, (lo_ref, hi_ref) in ((1, (kb_ref, k32_ref)), (2, (vb_ref, v32_ref))):
        z = jnp.dot(h, wqkv_ref[:, w_idx * a_dim:(w_idx + 1) * a_dim], preferred_element_type=F32)
        if prompt:
            zt = z.T
            hi_ref[0] = zt.reshape(N_HEADS, HEAD_DIM, tt)
            if w_idx == 1:
                lo_ref[0] = zt.astype(BF16)
            else:
                lo_ref[...] = z.astype(BF16).reshape(bb, tt, a_dim)
        else:
            lo_ref[...] = z.astype(BF16).reshape(bb, tt, a_dim)
            for hd in range(N_HEADS):
                hi_ref[:, hd, :, :] = z[:, hd * HEAD_DIM:(hd + 1) * HEAD_DIM].reshape(bb, tt, HEAD_DIM)

    zf = jnp.dot(h, wf_ref[...], preferred_element_type=F32)
    lf = _log_sigmoid(zf + bf_ref[...]).T
    if not prompt:
        p_len = clf_ref.shape[2]
        f_past = LOG2E * _cumsum_lanes(clf_ref[...].reshape(bb * N_HEADS, p_len))
        fp_ref[...] = f_past.reshape(bb, N_HEADS, p_len)
    for b in range(bb):
        lf_b = lf[:N_HEADS, b * tt:(b + 1) * tt]
        lf_ref[b] = lf_b
        if not prompt:
            fn_ref[b] = f_past[b * N_HEADS:(b + 1) * N_HEADS, p_len - 1:p_len] + LOG2E * _cumsum_lanes(lf_b)


def _qkv(x, mod, g, wqkv, wf, bf, cache_logf, *, bb, tt):
    B, T, D = x.shape
    A = wqkv.shape[1] // 3
    prompt = cache_logf is None
    tok = lambda b, t: (b, t, 0)
    per_b = lambda b, t: (b, 0, 0)
    in_specs = [
        pl.BlockSpec((bb, tt, D), tok),
        pl.BlockSpec((bb, N_MOD, D), per_b),
        _resident((1, D)),
        _resident(wqkv.shape),
        _resident(wf.shape),
        _resident(bf.shape),
    ]
    tok_spec = pl.BlockSpec((bb, tt, A), tok)
    tok_shape = jax.ShapeDtypeStruct((B, T, A), BF16)
    lf_spec = pl.BlockSpec((bb, N_HEADS, tt), lambda b, t: (b, 0, t))
    lf_shape = jax.ShapeDtypeStruct((B, N_HEADS, T), F32)
    if prompt:
        assert bb == 1
        kv_spec = pl.BlockSpec((1, N_HEADS, HEAD_DIM, tt), lambda b, t: (b, 0, 0, t))
        kv_shape = jax.ShapeDtypeStruct((B, N_HEADS, HEAD_DIM, T), F32)
        out_specs = [tok_spec, pl.BlockSpec((1, A, tt), lambda b, t: (b, 0, t)), tok_spec, kv_spec, kv_spec, lf_spec]
        out_shape = [tok_shape, jax.ShapeDtypeStruct((B, A, T), BF16), tok_shape, kv_shape, kv_shape, lf_shape]
        args = (x, mod, g, wqkv, wf, bf)
    else:
        assert T == tt
        P = cache_logf.shape[2]
        kv_spec = pl.BlockSpec((bb, N_HEADS, tt, HEAD_DIM), lambda b, t: (b, 0, t, 0))
        kv_shape = jax.ShapeDtypeStruct((B, N_HEADS, T, HEAD_DIM), F32)
        in_specs.append(pl.BlockSpec((bb, N_HEADS, P), per_b))
        out_specs = [tok_spec, tok_spec, tok_spec, kv_spec, kv_spec, lf_spec,
                     pl.BlockSpec((bb, N_HEADS, P), per_b), lf_spec]
        out_shape = [tok_shape, tok_shape, tok_shape, kv_shape, kv_shape, lf_shape,
                     jax.ShapeDtypeStruct((B, N_HEADS, P), F32), lf_shape]
        args = (x, mod, g, wqkv, wf, bf, cache_logf)
    return pl.pallas_call(
        functools.partial(_qkv_kernel, prompt=prompt),
        grid=(B // bb, T // tt),
        in_specs=in_specs,
        out_specs=out_specs,
        out_shape=out_shape,
        compiler_params=_params(("parallel", "parallel")),
        name="qkv_proj",
    )(*args)


def _attn_rows(q2, kt_ref, v1_ref, negf_ref, s_scr, p_scr, nk, blk):
    rows = HEADS_PER_BLOCK * blk
    half = blk // 2
    nkeys = nk * blk
    keep = lax.broadcasted_iota(jnp.int32, (blk, blk), 1) <= lax.broadcasted_iota(jnp.int32, (blk, blk), 0)
    s = jnp.dot(q2, kt_ref[0, :, 0:nkeys], preferred_element_type=F32)
    for kb in range(nk):
        ks = slice(kb * blk, (kb + 1) * blk)
        for h in range(HEADS_PER_BLOCK):
            t = s[h * blk:(h + 1) * blk, ks] + negf_ref[h:h + 1, ks]
            if kb == nk - 1:
                t = jnp.where(keep, t, -jnp.inf)
            s_scr[h * blk:(h + 1) * blk, ks] = t

    for r0 in range(0, rows, ATTN_ROWS):
        rs = slice(r0, r0 + ATTN_ROWS)
        mx = None
        for kb in range(nk):
            t = s_scr[rs, kb * blk:(kb + 1) * blk]
            t = jnp.maximum(t[:, :half], t[:, half:])
            mx = t if mx is None else jnp.maximum(mx, t)
        mb = jnp.broadcast_to(jnp.max(mx, axis=-1, keepdims=True), (ATTN_ROWS, blk))
        for kb in range(nk):
            ks = slice(kb * blk, (kb + 1) * blk)
            p_scr[rs, ks] = jnp.exp2(s_scr[rs, ks] - mb).astype(BF16)

    acc = jnp.dot(p_scr[:, 0:nkeys], v1_ref[0:nkeys, :], preferred_element_type=F32)
    return acc[:, :LANES] * (1.0 / acc[:, LANES:])


def _attn_prompt_kernel(q_ref, kt_ref, v_ref, lf_ref, o_ref, negf_scr, v1_scr, s_scr, p_scr, *, blk):
    T = v_ref.shape[1]
    negf_scr[...] = -LOG2E * _cumsum_lanes(lf_ref[0, 0])
    for pr in range(ATTN_PAIRS):
        v1_scr[pr, :, :LANES] = v_ref[0, :, pr * LANES:(pr + 1) * LANES]
        v1_scr[pr, :, LANES:] = jnp.ones((T, LANES), v1_scr.dtype)
    lane = lax.broadcasted_iota(jnp.int32, (blk, LANES), 1)
    nq = T // blk
    order = list(range(0, nq, 2)) + list(range(nq - 1 - nq % 2, 0, -2))
    slot = 0
    for c in order:
        for pr in range(ATTN_PAIRS):
            lanes = slice(pr * LANES, (pr + 1) * LANES)
            q = q_ref[0, c * blk:(c + 1) * blk, lanes]
            zero = jnp.zeros_like(q)
            q2 = jnp.concatenate([jnp.where(lane < HEAD_DIM, q, zero), jnp.where(lane >= HEAD_DIM, q, zero)],
                                 axis=0)
            o = _attn_rows(q2, kt_ref.at[:, lanes, :], v1_scr.at[pr],
                           negf_scr.at[pr * HEADS_PER_BLOCK:(pr + 1) * HEADS_PER_BLOCK, :],
                           s_scr.at[slot % ATTN_SLOTS], p_scr.at[slot % ATTN_SLOTS], c + 1, blk)
            o_ref[0, c * blk:(c + 1) * blk, lanes] = jnp.where(lane < HEAD_DIM, o[:blk], o[blk:]).astype(o_ref.dtype)
            slot += 1


def _attn_prompt(q, kt, v, logf, *, blk):
    B, T, A = q.shape
    width = ATTN_PAIRS * LANES
    ngrp = A // width
    heads = ATTN_PAIRS * HEADS_PER_BLOCK
    lf = logf.reshape(B, ngrp, heads, T)
    kern = functools.partial(_attn_prompt_kernel, blk=blk)
    grp = pl.BlockSpec((1, T, width), lambda b, p: (b, 0, p))
    grp_t = pl.BlockSpec((1, width, T), lambda b, p: (b, p, 0))
    return pl.pallas_call(
        kern,
        grid=(B, ngrp),
        in_specs=[grp, grp_t, grp, pl.BlockSpec((1, 1, heads, T), lambda b, p: (b, p, 0, 0))],
        out_specs=grp,
        out_shape=jax.ShapeDtypeStruct((B, T, A), BF16),
        scratch_shapes=[
            pltpu.VMEM((heads, T), F32),
            pltpu.VMEM((ATTN_PAIRS, T, 2 * LANES), BF16),
            pltpu.VMEM((ATTN_SLOTS, HEADS_PER_BLOCK * blk, T), F32),
            pltpu.VMEM((ATTN_SLOTS, HEADS_PER_BLOCK * blk, T), BF16),
        ],
        compiler_params=_params(("parallel", "parallel")),
        name="attn_prompt",
    )(q, kt, v, lf)


def _attn_sample_kernel(q_ref, kn_ref, vn_ref, ckt_ref, cvt_ref, fp_ref, fn_ref, o_ref):
    tt = q_ref.shape[1]
    p_len = ckt_ref.shape[3]
    width = SAMPLE_STACK * HEAD_DIM
    nt = (((1,), (1,)), ((), ()))
    lane_head = lax.broadcasted_iota(jnp.int32, (tt, width), 1) // HEAD_DIM
    keep = lax.broadcasted_iota(jnp.int32, (tt, tt), 1) <= lax.broadcasted_iota(jnp.int32, (tt, tt), 0)
    for g in range(ckt_ref.shape[1] // SAMPLE_STACK):
        heads = range(g * SAMPLE_STACK, (g + 1) * SAMPLE_STACK)
        lanes = slice(g * width, (g + 1) * width)
        q = q_ref[0][:, lanes]
        zero = jnp.zeros_like(q)
        qs = jnp.concatenate([jnp.where(lane_head == i, q, zero) for i in range(SAMPLE_STACK)], axis=0)
        kt = ckt_ref[0, heads.start:heads.stop].reshape(width, p_len).astype(BF16)
        vt = cvt_ref[0, heads.start:heads.stop].reshape(width, p_len).astype(BF16)
        s_p = jnp.dot(qs, kt, preferred_element_type=F32)
        s_n = lax.dot_general(qs, kn_ref[0][:, lanes], nt, preferred_element_type=F32)
        s_p = jnp.concatenate([s_p[i * tt:(i + 1) * tt] - fp_ref[0, 0, hd:hd + 1, :]
                               for i, hd in enumerate(heads)], axis=0)
        s_n = jnp.concatenate([jnp.where(keep, s_n[i * tt:(i + 1) * tt] - fn_ref[0, 0, hd:hd + 1, :], -jnp.inf)
                               for i, hd in enumerate(heads)], axis=0)
        m = jnp.maximum(jnp.max(s_p, axis=-1, keepdims=True), jnp.max(s_n, axis=-1, keepdims=True))
        p_p = jnp.exp2(s_p - m)
        p_n = jnp.exp2(s_n - m)
        l = jnp.sum(p_p, axis=-1, keepdims=True) + jnp.sum(p_n, axis=-1, keepdims=True)
        o = (lax.dot_general(p_p.astype(BF16), vt, nt, preferred_element_type=F32)
             + jnp.dot(p_n.astype(BF16), vn_ref[0][:, lanes], preferred_element_type=F32)) * (1.0 / l)
        out = jnp.zeros((tt, width), F32)
        for i in range(SAMPLE_STACK):
            out = jnp.where(lane_head == i, o[i * tt:(i + 1) * tt], out)
        o_ref[0, :, lanes] = out.astype(o_ref.dtype)


def _attn_sample(q, k, v, cache_kt, cache_vt, f_past, f_new):
    B, T, A = q.shape
    P = cache_kt.shape[3]
    hg = SAMPLE_HEAD_GROUP
    ngrp = N_HEADS // hg
    clf = f_past.reshape(B, ngrp, hg, P)
    lf = f_new.reshape(B, ngrp, hg, T)
    tok = lambda b, p: (b, 0, p)
    grp = lambda b, p: (b, p, 0, 0)
    return pl.pallas_call(
        _attn_sample_kernel,
        grid=(B, ngrp),
        in_specs=[
            pl.BlockSpec((1, T, hg * HEAD_DIM), tok),
            pl.BlockSpec((1, T, hg * HEAD_DIM), tok),
            pl.BlockSpec((1, T, hg * HEAD_DIM), tok),
            pl.BlockSpec((1, hg, HEAD_DIM, P), grp),
            pl.BlockSpec((1, hg, HEAD_DIM, P), grp),
            pl.BlockSpec((1, 1, hg, P), grp),
            pl.BlockSpec((1, 1, hg, T), grp),
        ],
        out_specs=pl.BlockSpec((1, T, hg * HEAD_DIM), tok),
        out_shape=jax.ShapeDtypeStruct((B, T, A), BF16),
        compiler_params=_params(("parallel", "parallel")),
        name="attn_sample",
    )(q, k, v, cache_kt, cache_vt, clf, lf)


def _mix_kernel(x_ref, o_ref, mod_ref, g_ref, w5_ref, cw_ref, pu_ref, wo_ref, y_ref, nu_ref, carry_scr):
    bb, tt, d = x_ref.shape
    tm = bb * tt
    t = pl.program_id(1)

    @pl.when(t == 0)
    def _():
        carry_scr[...] = pu_ref[...]

    x = x_ref[...]
    shift = mod_ref[:, 3:4, :]
    scale = mod_ref[:, 4:5, :]
    gate = mod_ref[:, 5:6, :]
    h = _rms_mod(x, g_ref[...], scale, shift).reshape(tm, d).astype(BF16)
    cdim = w5_ref.shape[1] // 5
    proj = lambda i: jnp.dot(h, w5_ref[:, i * cdim:(i + 1) * cdim], preferred_element_type=F32)

    u2 = proj(1) * proj(2)
    u = u2.reshape(bb, tt, cdim)
    tpos = lax.broadcasted_iota(jnp.int32, (1, tt, 1), 1)
    c0 = carry_scr[:, 0:1, :]
    c1 = carry_scr[:, 1:2, :]
    um1 = jnp.where(tpos == 0, c1, pltpu.roll(u2, 1, axis=0).reshape(bb, tt, cdim))
    um2 = jnp.where(tpos == 0, c0, jnp.where(tpos == 1, c1, pltpu.roll(u2, 2, axis=0).reshape(bb, tt, cdim)))
    conv = cw_ref[0:1, :] * um2 + cw_ref[1:2, :] * um1 + cw_ref[2:3, :] * u
    new_u = u[:, tt - (CONV_WIDTH - 1):, :]
    carry_scr[...] = new_u
    nu_ref[...] = new_u

    o_conv = proj(0) * conv.reshape(tm, cdim)
    m = _sigmoid(proj(3)) * o_ref[...].reshape(tm, cdim).astype(F32) + _sigmoid(proj(4)) * o_conv
    mo = jnp.dot(m.astype(BF16), wo_ref[...], preferred_element_type=F32)
    y_ref[...] = x + gate * mo.reshape(bb, tt, d)


def _mix(x, o_attn, mod, g, w5, conv_w, past_u, w_out, *, bb, tt):
    B, T, D = x.shape
    C = w5.shape[1] // 5
    tok = lambda b, t: (b, t, 0)
    per_b = lambda b, t: (b, 0, 0)
    return pl.pallas_call(
        _mix_kernel,
        grid=(B // bb, T // tt),
        in_specs=[
            pl.BlockSpec((bb, tt, D), tok),
            pl.BlockSpec((bb, tt, C), tok),
            pl.BlockSpec((bb, N_MOD, D), per_b),
            _resident((1, D)),
            _resident(w5.shape),
            _resident(conv_w.shape),
            pl.BlockSpec((bb, CONV_WIDTH - 1, C), per_b),
            _resident(w_out.shape),
        ],
        out_specs=[
            pl.BlockSpec((bb, tt, D), tok),
            pl.BlockSpec((bb, CONV_WIDTH - 1, C), per_b),
        ],
        out_shape=[
            jax.ShapeDtypeStruct((B, T, D), F32),
            jax.ShapeDtypeStruct((B, CONV_WIDTH - 1, C), F32),
        ],
        scratch_shapes=[pltpu.VMEM((bb, CONV_WIDTH - 1, C), F32)],
        compiler_params=_params(("parallel", "arbitrary")),
        name="mix",
    )(x, o_attn, mod, g, w5, conv_w, past_u, w_out)


def _layer(x, mod, past, weights, g_final, *, bb, tt):
    (g1, w1a, w1b, gm, wqkv, wf, bf, w5, conv_w, w_out, g2, w2a, w2b) = weights
    past_kt, past_vt, past_logf, past_u = past
    prompt = past_kt is None
    x1 = _ffn(x, mod, g1, w1a, w1b, g_final, bb=bb, tt=tt, mod_base=0, final_norm=False)
    if prompt:
        q, kbt, vb, k32, v32, logf = _qkv(x1, mod, gm, wqkv, wf, bf, None, bb=bb, tt=tt)
        o = _attn_prompt(q, kbt, vb, logf, blk=ATTN_BLOCK)
        k32, v32 = jnp.swapaxes(k32, 2, 3), jnp.swapaxes(v32, 2, 3)
    else:
        q, kb, vb, k32, v32, logf, f_past, f_new = _qkv(x1, mod, gm, wqkv, wf, bf, past_logf, bb=bb, tt=tt)
        o = _attn_sample(q, kb, vb, past_kt, past_vt, f_past, f_new)
    x2, new_u = _mix(x1, o, mod, gm, w5, conv_w, past_u, w_out, bb=bb, tt=tt)
    y = _ffn(x2, mod, g2, w2a, w2b, g_final, bb=bb, tt=tt, mod_base=6, final_norm=True)
    return y, k32, v32, logf, new_u


def kernel(x_prompt, x_sample, cache_k, cache_v, cache_logf, state_conv, c_prompt, c_sample, w_ada, b_ada, g_ffn1, w_ffn1_in, w_ffn1_out, g_mix, w_in, b_f, conv_w, w_out, g_ffn2, w_ffn2_in, w_ffn2_out, g_final):
    assert w_ada.shape[0] == 1, "single-layer encoder"
    Bp, Tp, D = x_prompt.shape
    Bs, Ts, _ = x_sample.shape
    A = N_HEADS * HEAD_DIM
    C = conv_w.shape[2]

    mod = _ada(jnp.concatenate([c_prompt, c_sample], axis=0), w_ada[0], b_ada[0])
    mod = mod.reshape(Bp + Bs, N_MOD, D)
    mod_p, mod_s = mod[:Bp], mod[Bp:]

    wi = w_in[0]
    off_f = 3 * A
    off_b = off_f + N_HEADS
    wqkv = wi[:, :off_f].astype(BF16)
    wf = jnp.pad(wi[:, off_f:off_b], ((0, 0), (0, LANES - N_HEADS))).astype(BF16)
    bf = jnp.pad(b_f[0], (0, LANES - N_HEADS)).reshape(1, LANES)
    w5 = wi[:, off_b:].astype(BF16)
    weights = (g_ffn1, w_ffn1_in[0].astype(BF16), w_ffn1_out[0].astype(BF16), g_mix, wqkv, wf, bf, w5,
               conv_w[0], w_out[0].astype(BF16), g_ffn2, w_ffn2_in[0].astype(BF16), w_ffn2_out[0].astype(BF16))
    gfin = g_final.reshape(1, D)

    zero_u = jnp.zeros((Bp, CONV_WIDTH - 1, C), F32)
    yp, kp, vp, fp, up = _layer(x_prompt, mod_p, (None, None, None, zero_u), weights, gfin,
                                bb=1, tt=ROW_TILE)
    past = (jnp.swapaxes(cache_k[0], 2, 3), jnp.swapaxes(cache_v[0], 2, 3), cache_logf[0], state_conv[0])
    ys, ks, vs, fs, us = _layer(x_sample, mod_s, past, weights, gfin, bb=ROW_TILE // Ts, tt=Ts)
    return (yp, ys, kp[None], vp[None], fp[None], up[None], ks[None], vs[None], fs[None], us[None])
```

```python
import functools

import jax
import jax.numpy as jnp
from jax import lax
from jax.experimental import pallas as pl
from jax.experimental.pallas import tpu as pltpu

F32 = jnp.float32
BF16 = jnp.bfloat16

EPS = 1e-6
N_HEADS = 16
HEAD_DIM = 64
N_MOD = 9
CONV_WIDTH = 3
LANES = 128
HEADS_PER_BLOCK = LANES // HEAD_DIM
V7X_VMEM_LIMIT_BYTES = 56 * 1024 * 1024
FFN_CHUNK = 256
PROMPT_ROW_TILE = 1024
SAMPLE_ROW_TILE = 512
ATTN_BLOCK = 256
ATTN_ROWS = 64
ATTN_PAIRS = 2
ATTN_SLOTS = 4
LOG2E = 1.4426950408889634
Q_SCALE = HEAD_DIM ** -0.5 * LOG2E
SAMPLE_HEAD_GROUP = 8
SAMPLE_STACK = 4


def _params(semantics):
    return pltpu.CompilerParams(dimension_semantics=semantics, vmem_limit_bytes=V7X_VMEM_LIMIT_BYTES)


def _resident(shape):
    zeros = (0,) * len(shape)
    return pl.BlockSpec(shape, lambda *_: zeros, pipeline_mode=pl.Buffered(1))


def _rms_mod(x, g, scale, shift):
    ms = jnp.mean(x * x, axis=-1, keepdims=True)
    y = x * lax.rsqrt(ms + EPS) * g
    return y * (1.0 + scale) + shift


def _sigmoid(x):
    return 1.0 / (1.0 + jnp.exp(-x))


def _log_sigmoid(x):
    return jnp.minimum(x, 0.0) - jnp.log1p(jnp.exp(-jnp.abs(x)))


def _cumsum_lanes(x):
    n = x.shape[-1]
    lane = lax.broadcasted_iota(jnp.int32, x.shape, x.ndim - 1)
    step = 1
    while step < n:
        x = x + jnp.where(lane >= step, pltpu.roll(x, step, axis=x.ndim - 1), 0.0)
        step *= 2
    return x


def _ada_kernel(c_ref, w_ref, b_ref, o_ref):
    c = c_ref[...]
    a = (c * _sigmoid(c)).astype(BF16)
    o_ref[...] = jnp.dot(a, w_ref[...].astype(BF16), preferred_element_type=F32) + b_ref[...]


def _ada(c, w_ada, b_ada):
    n, d = c.shape
    cols = w_ada.shape[1]
    tn = d
    return pl.pallas_call(
        _ada_kernel,
        grid=(cols // tn,),
        in_specs=[
            pl.BlockSpec((n, d), lambda j: (0, 0)),
            pl.BlockSpec((d, tn), lambda j: (0, j)),
            pl.BlockSpec((1, tn), lambda j: (0, j)),
        ],
        out_specs=pl.BlockSpec((n, tn), lambda j: (0, j)),
        out_shape=jax.ShapeDtypeStruct((n, cols), F32),
        compiler_params=_params(("parallel",)),
        name="ada_mod",
    )(c, w_ada, b_ada.reshape(1, cols))


def _ffn_kernel(x_ref, mod_ref, g_ref, w1_ref, w2_ref, gfin_ref, o_ref, h_scr, *, mod_base, final_norm):
    bb, tt, d = x_ref.shape
    tm = bb * tt
    f = w2_ref.shape[0]
    fc = FFN_CHUNK
    x = x_ref[...]
    shift = mod_ref[:, mod_base:mod_base + 1, :]
    scale = mod_ref[:, mod_base + 1:mod_base + 2, :]
    gate = mod_ref[:, mod_base + 2:mod_base + 3, :]
    h_scr[...] = _rms_mod(x, g_ref[...], scale, shift).reshape(tm, d).astype(BF16)

    acc = None
    for lo in range(0, f, fc):
        h = h_scr[...]
        a = jnp.dot(h, w1_ref[:, lo:lo + fc], preferred_element_type=F32)
        b = jnp.dot(h, w1_ref[:, f + lo:f + lo + fc], preferred_element_type=F32)
        act = (a * _sigmoid(a) * b).astype(BF16)
        part = jnp.dot(act, w2_ref[lo:lo + fc, :], preferred_element_type=F32)
        acc = part if acc is None else acc + part

    y = x + 0.5 * gate * acc.reshape(bb, tt, d)
    if final_norm:
        ms = jnp.mean(y * y, axis=-1, keepdims=True)
        y = y * lax.rsqrt(ms + EPS) * gfin_ref[...]
    o_ref[...] = y


def _ffn(x, mod, g, w1c, w2c, g_final, *, bb, tt, mod_base, final_norm):
    B, T, D = x.shape
    assert w2c.shape[0] % FFN_CHUNK == 0
    tm = bb * tt
    kern = functools.partial(_ffn_kernel, mod_base=mod_base, final_norm=final_norm)
    return pl.pallas_call(
        kern,
        grid=(B // bb, T // tt),
        in_specs=[
            pl.BlockSpec((bb, tt, D), lambda b, t: (b, t, 0)),
            pl.BlockSpec((bb, N_MOD, D), lambda b, t: (b, 0, 0)),
            _resident((1, D)),
            _resident(w1c.shape),
            _resident(w2c.shape),
            _resident((1, D)),
        ],
        out_specs=pl.BlockSpec((bb, tt, D), lambda b, t: (b, t, 0)),
        out_shape=jax.ShapeDtypeStruct((B, T, D), F32),
        scratch_shapes=[pltpu.VMEM((tm, D), BF16)],
        compiler_params=_params(("parallel", "parallel")),
        name="ffn_final" if final_norm else "ffn",
    )(x, mod, g, w1c, w2c, g_final)


def _qkv_kernel(*refs, prompt):
    if prompt:
        (x_ref, mod_ref, g_ref, wqkv_ref, wf_ref, bf_ref,
         q_ref, kb_ref, vb_ref, k32_ref, v32_ref, lf_ref) = refs
    else:
        (x_ref, mod_ref, g_ref, wqkv_ref, wf_ref, bf_ref, clf_ref,
         q_ref, kb_ref, vb_ref, k32_ref, v32_ref, lf_ref, fp_ref, fn_ref) = refs
    bb, tt, d = x_ref.shape
    tm = bb * tt
    a_dim = wqkv_ref.shape[1] // 3
    shift = mod_ref[:, 3:4, :]
    scale = mod_ref[:, 4:5, :]
    h = _rms_mod(x_ref[...], g_ref[...], scale, shift).reshape(tm, d).astype(BF16)

    zq = jnp.dot(h, wqkv_ref[:, 0:a_dim], preferred_element_type=F32)
    q_ref[...] = (zq * Q_SCALE).astype(BF16).reshape(bb, tt, a_dim)

    for w_idx, (lo_ref, hi_ref) in ((1, (kb_ref, k32_ref)), (2, (vb_ref, v32_ref))):
        z = jnp.dot(h, wqkv_ref[:, w_idx * a_dim:(w_idx + 1) * a_dim], preferred_element_type=F32)
        if prompt:
            zt = z.T
            hi_ref[0] = zt.reshape(N_HEADS, HEAD_DIM, tt)
            if w_idx == 1:
                lo_ref[0] = zt.astype(BF16)
            else:
                lo_ref[...] = z.astype(BF16).reshape(bb, tt, a_dim)
        else:
            lo_ref[...] = z.astype(BF16).reshape(bb, tt, a_dim)
            for hd in range(N_HEADS):
                hi_ref[:, hd, :, :] = z[:, hd * HEAD_DIM:(hd + 1) * HEAD_DIM].reshape(bb, tt, HEAD_DIM)

    zf = jnp.dot(h, wf_ref[...], preferred_element_type=F32)
    lf = _log_sigmoid(zf + bf_ref[...]).T
    if not prompt:
        p_len = clf_ref.shape[2]
        f_past = LOG2E * _cumsum_lanes(clf_ref[...].reshape(bb * N_HEADS, p_len))
        fp_ref[...] = f_past.reshape(bb, N_HEADS, p_len)
    for b in range(bb):
        lf_b = lf[:N_HEADS, b * tt:(b + 1) * tt]
        lf_ref[b] = lf_b
        if not prompt:
            fn_ref[b] = f_past[b * N_HEADS:(b + 1) * N_HEADS, p_len - 1:p_len] + LOG2E * _cumsum_lanes(lf_b)


def _qkv(x, mod, g, wqkv, wf, bf, cache_logf, *, bb, tt):
    B, T, D = x.shape
    A = wqkv.shape[1] // 3
    prompt = cache_logf is None
    tok = lambda b, t: (b, t, 0)
    per_b = lambda b, t: (b, 0, 0)
    in_specs = [
        pl.BlockSpec((bb, tt, D), tok),
        pl.BlockSpec((bb, N_MOD, D), per_b),
        _resident((1, D)),
        _resident(wqkv.shape),
        _resident(wf.shape),
        _resident(bf.shape),
    ]
    tok_spec = pl.BlockSpec((bb, tt, A), tok)
    tok_shape = jax.ShapeDtypeStruct((B, T, A), BF16)
    lf_spec = pl.BlockSpec((bb, N_HEADS, tt), lambda b, t: (b, 0, t))
    lf_shape = jax.ShapeDtypeStruct((B, N_HEADS, T), F32)
    if prompt:
        assert bb == 1
        kv_spec = pl.BlockSpec((1, N_HEADS, HEAD_DIM, tt), lambda b, t: (b, 0, 0, t))
        kv_shape = jax.ShapeDtypeStruct((B, N_HEADS, HEAD_DIM, T), F32)
        out_specs = [tok_spec, pl.BlockSpec((1, A, tt), lambda b, t: (b, 0, t)), tok_spec, kv_spec, kv_spec, lf_spec]
        out_shape = [tok_shape, jax.ShapeDtypeStruct((B, A, T), BF16), tok_shape, kv_shape, kv_shape, lf_shape]
        args = (x, mod, g, wqkv, wf, bf)
    else:
        assert T == tt
        P = cache_logf.shape[2]
        kv_spec = pl.BlockSpec((bb, N_HEADS, tt, HEAD_DIM), lambda b, t: (b, 0, t, 0))
        kv_shape = jax.ShapeDtypeStruct((B, N_HEADS, T, HEAD_DIM), F32)
        in_specs.append(pl.BlockSpec((bb, N_HEADS, P), per_b))
        out_specs = [tok_spec, tok_spec, tok_spec, kv_spec, kv_spec, lf_spec,
                     pl.BlockSpec((bb, N_HEADS, P), per_b), lf_spec]
        out_shape = [tok_shape, tok_shape, tok_shape, kv_shape, kv_shape, lf_shape,
                     jax.ShapeDtypeStruct((B, N_HEADS, P), F32), lf_shape]
        args = (x, mod, g, wqkv, wf, bf, cache_logf)
    return pl.pallas_call(
        functools.partial(_qkv_kernel, prompt=prompt),
        grid=(B // bb, T // tt),
        in_specs=in_specs,
        out_specs=out_specs,
        out_shape=out_shape,
        compiler_params=_params(("parallel", "parallel")),
        name="qkv_proj",
    )(*args)


def _attn_rows(q2, kt_ref, v1_ref, negf_ref, s_scr, p_scr, nk, blk):
    rows = HEADS_PER_BLOCK * blk
    half = blk // 2
    nkeys = nk * blk
    keep = lax.broadcasted_iota(jnp.int32, (blk, blk), 1) <= lax.broadcasted_iota(jnp.int32, (blk, blk), 0)
    s = jnp.dot(q2, kt_ref[0, :, 0:nkeys], preferred_element_type=F32)
    for kb in range(nk):
        ks = slice(kb * blk, (kb + 1) * blk)
        for h in range(HEADS_PER_BLOCK):
            t = s[h * blk:(h + 1) * blk, ks] + negf_ref[h:h + 1, ks]
            if kb == nk - 1:
                t = jnp.where(keep, t, -jnp.inf)
            s_scr[h * blk:(h + 1) * blk, ks] = t

    for r0 in range(0, rows, ATTN_ROWS):
        rs = slice(r0, r0 + ATTN_ROWS)
        mx = None
        for kb in range(nk):
            t = s_scr[rs, kb * blk:(kb + 1) * blk]
            t = jnp.maximum(t[:, :half], t[:, half:])
            mx = t if mx is None else jnp.maximum(mx, t)
        mb = jnp.broadcast_to(jnp.max(mx, axis=-1, keepdims=True), (ATTN_ROWS, blk))
        for kb in range(nk):
            ks = slice(kb * blk, (kb + 1) * blk)
            p_scr[rs, ks] = jnp.exp2(s_scr[rs, ks] - mb).astype(BF16)

    acc = jnp.dot(p_scr[:, 0:nkeys], v1_ref[0:nkeys, :], preferred_element_type=F32)
    return acc[:, :LANES] * (1.0 / acc[:, LANES:])


def _attn_prompt_kernel(q_ref, kt_ref, v_ref, lf_ref, o_ref, negf_scr, v1_scr, s_scr, p_scr, *, blk):
    T = v_ref.shape[1]
    negf_scr[...] = -LOG2E * _cumsum_lanes(lf_ref[0, 0])
    for pr in range(ATTN_PAIRS):
        v1_scr[pr, :, :LANES] = v_ref[0, :, pr * LANES:(pr + 1) * LANES]
        v1_scr[pr, :, LANES:] = jnp.ones((T, LANES), v1_scr.dtype)
    lane = lax.broadcasted_iota(jnp.int32, (blk, LANES), 1)
    nq = T // blk
    order = list(range(0, nq, 2)) + list(range(nq - 1 - nq % 2, 0, -2))
    slot = 0
    for c in order:
        for pr in range(ATTN_PAIRS):
            lanes = slice(pr * LANES, (pr + 1) * LANES)
            q = q_ref[0, c * blk:(c + 1) * blk, lanes]
            zero = jnp.zeros_like(q)
            q2 = jnp.concatenate([jnp.where(lane < HEAD_DIM, q, zero), jnp.where(lane >= HEAD_DIM, q, zero)],
                                 axis=0)
            o = _attn_rows(q2, kt_ref.at[:, lanes, :], v1_scr.at[pr],
                           negf_scr.at[pr * HEADS_PER_BLOCK:(pr + 1) * HEADS_PER_BLOCK, :],
                           s_scr.at[slot % ATTN_SLOTS], p_scr.at[slot % ATTN_SLOTS], c + 1, blk)
            o_ref[0, c * blk:(c + 1) * blk, lanes] = jnp.where(lane < HEAD_DIM, o[:blk], o[blk:]).astype(o_ref.dtype)
            slot += 1


def _attn_prompt(q, kt, v, logf, *, blk):
    B, T, A = q.shape
    width = ATTN_PAIRS * LANES
    ngrp = A // width
    heads = ATTN_PAIRS * HEADS_PER_BLOCK
    lf = logf.reshape(B, ngrp, heads, T)
    kern = functools.partial(_attn_prompt_kernel, blk=blk)
    grp = pl.BlockSpec((1, T, width), lambda b, p: (b, 0, p))
    grp_t = pl.BlockSpec((1, width, T), lambda b, p: (b, p, 0))
    return pl.pallas_call(
        kern,
        grid=(B, ngrp),
        in_specs=[grp, grp_t, grp, pl.BlockSpec((1, 1, heads, T), lambda b, p: (b, p, 0, 0))],
        out_specs=grp,
        out_shape=jax.ShapeDtypeStruct((B, T, A), BF16),
        scratch_shapes=[
            pltpu.VMEM((heads, T), F32),
            pltpu.VMEM((ATTN_PAIRS, T, 2 * LANES), BF16),
            pltpu.VMEM((ATTN_SLOTS, HEADS_PER_BLOCK * blk, T), F32),
            pltpu.VMEM((ATTN_SLOTS, HEADS_PER_BLOCK * blk, T), BF16),
        ],
        compiler_params=_params(("parallel", "parallel")),
        name="attn_prompt",
    )(q, kt, v, lf)


def _attn_sample_kernel(q_ref, kn_ref, vn_ref, ckt_ref, cvt_ref, fp_ref, fn_ref, o_ref):
    tt = q_ref.shape[1]
    p_len = ckt_ref.shape[3]
    width = SAMPLE_STACK * HEAD_DIM
    nt = (((1,), (1,)), ((), ()))
    lane_head = lax.broadcasted_iota(jnp.int32, (tt, width), 1) // HEAD_DIM
    keep = lax.broadcasted_iota(jnp.int32, (tt, tt), 1) <= lax.broadcasted_iota(jnp.int32, (tt, tt), 0)
    for g in range(ckt_ref.shape[1] // SAMPLE_STACK):
        heads = range(g * SAMPLE_STACK, (g + 1) * SAMPLE_STACK)
        lanes = slice(g * width, (g + 1) * width)
        q = q_ref[0][:, lanes]
        zero = jnp.zeros_like(q)
        qs = jnp.concatenate([jnp.where(lane_head == i, q, zero) for i in range(SAMPLE_STACK)], axis=0)
        kt = ckt_ref[0, heads.start:heads.stop].reshape(width, p_len).astype(BF16)
        vt = cvt_ref[0, heads.start:heads.stop].reshape(width, p_len).astype(BF16)
        s_p = jnp.dot(qs, kt, preferred_element_type=F32)
        s_n = lax.dot_general(qs, kn_ref[0][:, lanes], nt, preferred_element_type=F32)
        s_p = jnp.concatenate([s_p[i * tt:(i + 1) * tt] - fp_ref[0, 0, hd:hd + 1, :]
                               for i, hd in enumerate(heads)], axis=0)
        s_n = jnp.concatenate([jnp.where(keep, s_n[i * tt:(i + 1) * tt] - fn_ref[0, 0, hd:hd + 1, :], -jnp.inf)
                               for i, hd in enumerate(heads)], axis=0)
        m = jnp.maximum(jnp.max(s_p, axis=-1, keepdims=True), jnp.max(s_n, axis=-1, keepdims=True))
        p_p = jnp.exp2(s_p - m)
        p_n = jnp.exp2(s_n - m)
        l = jnp.sum(p_p, axis=-1, keepdims=True) + jnp.sum(p_n, axis=-1, keepdims=True)
        o = (lax.dot_general(p_p.astype(BF16), vt, nt, preferred_element_type=F32)
             + jnp.dot(p_n.astype(BF16), vn_ref[0][:, lanes], preferred_element_type=F32)) * (1.0 / l)
        out = jnp.zeros((tt, width), F32)
        for i in range(SAMPLE_STACK):
            out = jnp.where(lane_head == i, o[i * tt:(i + 1) * tt], out)
        o_ref[0, :, lanes] = out.astype(o_ref.dtype)


def _attn_sample(q, k, v, cache_kt, cache_vt, f_past, f_new):
    B, T, A = q.shape
    P = cache_kt.shape[3]
    hg = SAMPLE_HEAD_GROUP
    ngrp = N_HEADS // hg
    clf = f_past.reshape(B, ngrp, hg, P)
    lf = f_new.reshape(B, ngrp, hg, T)
    tok = lambda b, p: (b, 0, p)
    grp = lambda b, p: (b, p, 0, 0)
    return pl.pallas_call(
        _attn_sample_kernel,
        grid=(B, ngrp),
        in_specs=[
            pl.BlockSpec((1, T, hg * HEAD_DIM), tok),
            pl.BlockSpec((1, T, hg * HEAD_DIM), tok),
            pl.BlockSpec((1, T, hg * HEAD_DIM), tok),
            pl.BlockSpec((1, hg, HEAD_DIM, P), grp),
            pl.BlockSpec((1, hg, HEAD_DIM, P), grp),
            pl.BlockSpec((1, 1, hg, P), grp),
            pl.BlockSpec((1, 1, hg, T), grp),
        ],
        out_specs=pl.BlockSpec((1, T, hg * HEAD_DIM), tok),
        out_shape=jax.ShapeDtypeStruct((B, T, A), BF16),
        compiler_params=_params(("parallel", "parallel")),
        name="attn_sample",
    )(q, k, v, cache_kt, cache_vt, clf, lf)


def _mix_kernel(x_ref, o_ref, mod_ref, g_ref, w5_ref, cw_ref, pu_ref, wo_ref, y_ref, nu_ref, carry_scr):
    bb, tt, d = x_ref.shape
    tm = bb * tt
    t = pl.program_id(1)

    @pl.when(t == 0)
    def _():
        carry_scr[...] = pu_ref[...]

    x = x_ref[...]
    shift = mod_ref[:, 3:4, :]
    scale = mod_ref[:, 4:5, :]
    gate = mod_ref[:, 5:6, :]
    h = _rms_mod(x, g_ref[...], scale, shift).reshape(tm, d).astype(BF16)
    cdim = w5_ref.shape[1] // 5
    proj = lambda i: jnp.dot(h, w5_ref[:, i * cdim:(i + 1) * cdim], preferred_element_type=F32)

    u2 = proj(1) * proj(2)
    u = u2.reshape(bb, tt, cdim)
    tpos = lax.broadcasted_iota(jnp.int32, (1, tt, 1), 1)
    c0 = carry_scr[:, 0:1, :]
    c1 = carry_scr[:, 1:2, :]
    um1 = jnp.where(tpos == 0, c1, pltpu.roll(u2, 1, axis=0).reshape(bb, tt, cdim))
    um2 = jnp.where(tpos == 0, c0, jnp.where(tpos == 1, c1, pltpu.roll(u2, 2, axis=0).reshape(bb, tt, cdim)))
    conv = cw_ref[0:1, :] * um2 + cw_ref[1:2, :] * um1 + cw_ref[2:3, :] * u
    new_u = u[:, tt - (CONV_WIDTH - 1):, :]
    carry_scr[...] = new_u
    nu_ref[...] = new_u

    o_conv = proj(0) * conv.reshape(tm, cdim)
    m = _sigmoid(proj(3)) * o_ref[...].reshape(tm, cdim).astype(F32) + _sigmoid(proj(4)) * o_conv
    mo = jnp.dot(m.astype(BF16), wo_ref[...], preferred_element_type=F32)
    y_ref[...] = x + gate * mo.reshape(bb, tt, d)


def _mix(x, o_attn, mod, g, w5, conv_w, past_u, w_out, *, bb, tt):
    B, T, D = x.shape
    C = w5.shape[1] // 5
    tok = lambda b, t: (b, t, 0)
    per_b = lambda b, t: (b, 0, 0)
    return pl.pallas_call(
        _mix_kernel,
        grid=(B // bb, T // tt),
        in_specs=[
            pl.BlockSpec((bb, tt, D), tok),
            pl.BlockSpec((bb, tt, C), tok),
            pl.BlockSpec((bb, N_MOD, D), per_b),
            _resident((1, D)),
            _resident(w5.shape),
            _resident(conv_w.shape),
            pl.BlockSpec((bb, CONV_WIDTH - 1, C), per_b),
            _resident(w_out.shape),
        ],
        out_specs=[
            pl.BlockSpec((bb, tt, D), tok),
            pl.BlockSpec((bb, CONV_WIDTH - 1, C), per_b),
        ],
        out_shape=[
            jax.ShapeDtypeStruct((B, T, D), F32),
            jax.ShapeDtypeStruct((B, CONV_WIDTH - 1, C), F32),
        ],
        scratch_shapes=[pltpu.VMEM((bb, CONV_WIDTH - 1, C), F32)],
        compiler_params=_params(("parallel", "arbitrary")),
        name="mix",
    )(x, o_attn, mod, g, w5, conv_w, past_u, w_out)


def _layer(x, mod, past, weights, g_final, *, bb, tt):
    (g1, w1a, w1b, gm, wqkv, wf, bf, w5, conv_w, w_out, g2, w2a, w2b) = weights
    past_kt, past_vt, past_logf, past_u = past
    prompt = past_kt is None
    x1 = _ffn(x, mod, g1, w1a, w1b, g_final, bb=bb, tt=tt, mod_base=0, final_norm=False)
    if prompt:
        q, kbt, vb, k32, v32, logf = _qkv(x1, mod, gm, wqkv, wf, bf, None, bb=bb, tt=tt)
        o = _attn_prompt(q, kbt, vb, logf, blk=ATTN_BLOCK)
        k32, v32 = jnp.swapaxes(k32, 2, 3), jnp.swapaxes(v32, 2, 3)
    else:
        q, kb, vb, k32, v32, logf, f_past, f_new = _qkv(x1, mod, gm, wqkv, wf, bf, past_logf, bb=bb, tt=tt)
        o = _attn_sample(q, kb, vb, past_kt, past_vt, f_past, f_new)
    x2, new_u = _mix(x1, o, mod, gm, w5, conv_w, past_u, w_out, bb=bb, tt=tt)
    y = _ffn(x2, mod, g2, w2a, w2b, g_final, bb=bb, tt=tt, mod_base=6, final_norm=True)
    return y, k32, v32, logf, new_u


def kernel(x_prompt, x_sample, cache_k, cache_v, cache_logf, state_conv, c_prompt, c_sample, w_ada, b_ada, g_ffn1, w_ffn1_in, w_ffn1_out, g_mix, w_in, b_f, conv_w, w_out, g_ffn2, w_ffn2_in, w_ffn2_out, g_final):
    assert w_ada.shape[0] == 1, "single-layer encoder"
    Bp, Tp, D = x_prompt.shape
    Bs, Ts, _ = x_sample.shape
    A = N_HEADS * HEAD_DIM
    C = conv_w.shape[2]

    mod = _ada(jnp.concatenate([c_prompt, c_sample], axis=0), w_ada[0], b_ada[0])
    mod = mod.reshape(Bp + Bs, N_MOD, D)
    mod_p, mod_s = mod[:Bp], mod[Bp:]

    wi = w_in[0]
    off_f = 3 * A
    off_b = off_f + N_HEADS
    wqkv = wi[:, :off_f].astype(BF16)
    wf = jnp.pad(wi[:, off_f:off_b], ((0, 0), (0, LANES - N_HEADS))).astype(BF16)
    bf = jnp.pad(b_f[0], (0, LANES - N_HEADS)).reshape(1, LANES)
    w5 = wi[:, off_b:].astype(BF16)
    weights = (g_ffn1, w_ffn1_in[0].astype(BF16), w_ffn1_out[0].astype(BF16), g_mix, wqkv, wf, bf, w5,
               conv_w[0], w_out[0].astype(BF16), g_ffn2, w_ffn2_in[0].astype(BF16), w_ffn2_out[0].astype(BF16))
    gfin = g_final.reshape(1, D)

    zero_u = jnp.zeros((Bp, CONV_WIDTH - 1, C), F32)
    yp, kp, vp, fp, up = _layer(x_prompt, mod_p, (None, None, None, zero_u), weights, gfin,
                                bb=1, tt=PROMPT_ROW_TILE)
    past = (jnp.swapaxes(cache_k[0], 2, 3), jnp.swapaxes(cache_v[0], 2, 3), cache_logf[0], state_conv[0])
    ys, ks, vs, fs, us = _layer(x_sample, mod_s, past, weights, gfin, bb=SAMPLE_ROW_TILE // Ts, tt=Ts)
    return (yp, ys, kp[None], vp[None], fp[None], up[None], ks[None], vs[None], fs[None], us[None])
```

```python
import functools

import jax
import jax.numpy as jnp
from jax import lax
from jax.experimental import pallas as pl
from jax.experimental.pallas import tpu as pltpu

F32 = jnp.float32
BF16 = jnp.bfloat16

EPS = 1e-6
N_HEADS = 16
HEAD_DIM = 64
N_MOD = 9
CONV_WIDTH = 3
LANES = 128
HEADS_PER_BLOCK = LANES // HEAD_DIM
V7X_VMEM_LIMIT_BYTES = 56 * 1024 * 1024
FFN_CHUNK = 256
PROMPT_ROW_TILE = 1024
SAMPLE_ROW_TILE = 512
ATTN_BLOCK = 256
ATTN_ROWS = 64
ATTN_PAIRS = 2
ATTN_SLOTS = 4
LOG2E = 1.4426950408889634
Q_SCALE = HEAD_DIM ** -0.5 * LOG2E
SAMPLE_HEAD_GROUP = 8
SAMPLE_STACK = 4


def _params(semantics):
    return pltpu.CompilerParams(dimension_semantics=semantics, vmem_limit_bytes=V7X_VMEM_LIMIT_BYTES)


def _resident(shape):
    zeros = (0,) * len(shape)
    return pl.BlockSpec(shape, lambda *_: zeros, pipeline_mode=pl.Buffered(1))


def _rms_mod(x, g, scale, shift):
    ms = jnp.mean(x * x, axis=-1, keepdims=True)
    y = x * lax.rsqrt(ms + EPS) * g
    return y * (1.0 + scale) + shift


def _dot_t(x, w_t):
    return lax.dot_general(x, w_t, (((1,), (1,)), ((), ())), preferred_element_type=F32)


def _sigmoid(x):
    return 1.0 / (1.0 + jnp.exp(-x))


def _log_sigmoid(x):
    return jnp.minimum(x, 0.0) - jnp.log1p(jnp.exp(-jnp.abs(x)))


def _cumsum_lanes(x):
    n = x.shape[-1]
    lane = lax.broadcasted_iota(jnp.int32, x.shape, x.ndim - 1)
    step = 1
    while step < n:
        x = x + jnp.where(lane >= step, pltpu.roll(x, step, axis=x.ndim - 1), 0.0)
        step *= 2
    return x


def _ada_kernel(c_ref, w_ref, b_ref, o_ref):
    c = c_ref[...]
    a = (c * _sigmoid(c)).astype(BF16)
    o_ref[...] = jnp.dot(a, w_ref[...].astype(BF16), preferred_element_type=F32) + b_ref[...]


def _ada(c, w_ada, b_ada):
    n, d = c.shape
    cols = w_ada.shape[1]
    tn = d
    return pl.pallas_call(
        _ada_kernel,
        grid=(cols // tn,),
        in_specs=[
            pl.BlockSpec((n, d), lambda j: (0, 0)),
            pl.BlockSpec((d, tn), lambda j: (0, j)),
            pl.BlockSpec((1, tn), lambda j: (0, j)),
        ],
        out_specs=pl.BlockSpec((n, tn), lambda j: (0, j)),
        out_shape=jax.ShapeDtypeStruct((n, cols), F32),
        compiler_params=_params(("parallel",)),
        name="ada_mod",
    )(c, w_ada, b_ada.reshape(1, cols))


def _ffn_kernel(x_ref, mod_ref, g_ref, w1_ref, w2_ref, gfin_ref, o_ref, h_scr, *, mod_base, final_norm):
    bb, tt, d = x_ref.shape
    tm = bb * tt
    f = w2_ref.shape[0]
    fc = FFN_CHUNK
    x = x_ref[...]
    shift = mod_ref[:, mod_base:mod_base + 1, :]
    scale = mod_ref[:, mod_base + 1:mod_base + 2, :]
    gate = mod_ref[:, mod_base + 2:mod_base + 3, :]
    h_scr[...] = _rms_mod(x, g_ref[...], scale, shift).reshape(tm, d).astype(BF16)

    acc = None
    for lo in range(0, f, fc):
        h = h_scr[...]
        a = jnp.dot(h, w1_ref[:, lo:lo + fc], preferred_element_type=F32)
        b = jnp.dot(h, w1_ref[:, f + lo:f + lo + fc], preferred_element_type=F32)
        act = (a * _sigmoid(a) * b).astype(BF16)
        part = jnp.dot(act, w2_ref[lo:lo + fc, :], preferred_element_type=F32)
        acc = part if acc is None else acc + part

    y = x + 0.5 * gate * acc.reshape(bb, tt, d)
    if final_norm:
        ms = jnp.mean(y * y, axis=-1, keepdims=True)
        y = y * lax.rsqrt(ms + EPS) * gfin_ref[...]
    o_ref[...] = y


def _ffn(x, mod, g, w1c, w2c, g_final, *, bb, tt, mod_base, final_norm):
    B, T, D = x.shape
    assert w2c.shape[0] % FFN_CHUNK == 0
    tm = bb * tt
    kern = functools.partial(_ffn_kernel, mod_base=mod_base, final_norm=final_norm)
    return pl.pallas_call(
        kern,
        grid=(B // bb, T // tt),
        in_specs=[
            pl.BlockSpec((bb, tt, D), lambda b, t: (b, t, 0)),
            pl.BlockSpec((bb, N_MOD, D), lambda b, t: (b, 0, 0)),
            _resident((1, D)),
            _resident(w1c.shape),
            _resident(w2c.shape),
            _resident((1, D)),
        ],
        out_specs=pl.BlockSpec((bb, tt, D), lambda b, t: (b, t, 0)),
        out_shape=jax.ShapeDtypeStruct((B, T, D), F32),
        scratch_shapes=[pltpu.VMEM((tm, D), BF16)],
        compiler_params=_params(("parallel", "parallel")),
        name="ffn_final" if final_norm else "ffn",
    )(x, mod, g, w1c, w2c, g_final)


def _qkv_kernel(*refs, prompt):
    if prompt:
        (x_ref, mod_ref, g_ref, wqkv_ref, wf_ref, bf_ref,
         q_ref, kb_ref, vb_ref, k32_ref, v32_ref, lf_ref) = refs
    else:
        (x_ref, mod_ref, g_ref, wqkv_ref, wf_ref, bf_ref, clf_ref,
         q_ref, kb_ref, vb_ref, k32_ref, v32_ref, lf_ref, fp_ref, fn_ref) = refs
    bb, tt, d = x_ref.shape
    tm = bb * tt
    a_dim = wqkv_ref.shape[0] // 3
    shift = mod_ref[:, 3:4, :]
    scale = mod_ref[:, 4:5, :]
    h = _rms_mod(x_ref[...], g_ref[...], scale, shift).reshape(tm, d).astype(BF16)

    zq = _dot_t(h, wqkv_ref[0:a_dim, :])
    q_ref[...] = (zq * Q_SCALE).astype(BF16).reshape(bb, tt, a_dim)

    for w_idx, (lo_ref, hi_ref) in ((1, (kb_ref, k32_ref)), (2, (vb_ref, v32_ref))):
        z = _dot_t(h, wqkv_ref[w_idx * a_dim:(w_idx + 1) * a_dim, :])
        if prompt:
            zt = z.T
            hi_ref[0] = zt.reshape(N_HEADS, HEAD_DIM, tt)
            if w_idx == 1:
                lo_ref[0] = zt.astype(BF16)
            else:
                lo_ref[...] = z.astype(BF16).reshape(bb, tt, a_dim)
        else:
            lo_ref[...] = z.astype(BF16).reshape(bb, tt, a_dim)
            for hd in range(N_HEADS):
                hi_ref[:, hd, :, :] = z[:, hd * HEAD_DIM:(hd + 1) * HEAD_DIM].reshape(bb, tt, HEAD_DIM)

    zf = _dot_t(h, wf_ref[...])
    lf = _log_sigmoid(zf + bf_ref[...]).T
    if not prompt:
        p_len = clf_ref.shape[2]
        f_past = LOG2E * _cumsum_lanes(clf_ref[...].reshape(bb * N_HEADS, p_len))
        fp_ref[...] = f_past.reshape(bb, N_HEADS, p_len)
    for b in range(bb):
        lf_b = lf[:N_HEADS, b * tt:(b + 1) * tt]
        lf_ref[b] = lf_b
        if not prompt:
            fn_ref[b] = f_past[b * N_HEADS:(b + 1) * N_HEADS, p_len - 1:p_len] + LOG2E * _cumsum_lanes(lf_b)


def _qkv(x, mod, g, wqkv, wf, bf, cache_logf, *, bb, tt):
    B, T, D = x.shape
    A = wqkv.shape[0] // 3
    prompt = cache_logf is None
    tok = lambda b, t: (b, t, 0)
    per_b = lambda b, t: (b, 0, 0)
    in_specs = [
        pl.BlockSpec((bb, tt, D), tok),
        pl.BlockSpec((bb, N_MOD, D), per_b),
        _resident((1, D)),
        _resident(wqkv.shape),
        _resident(wf.shape),
        _resident(bf.shape),
    ]
    tok_spec = pl.BlockSpec((bb, tt, A), tok)
    tok_shape = jax.ShapeDtypeStruct((B, T, A), BF16)
    lf_spec = pl.BlockSpec((bb, N_HEADS, tt), lambda b, t: (b, 0, t))
    lf_shape = jax.ShapeDtypeStruct((B, N_HEADS, T), F32)
    if prompt:
        assert bb == 1
        kv_spec = pl.BlockSpec((1, N_HEADS, HEAD_DIM, tt), lambda b, t: (b, 0, 0, t))
        kv_shape = jax.ShapeDtypeStruct((B, N_HEADS, HEAD_DIM, T), F32)
        out_specs = [tok_spec, pl.BlockSpec((1, A, tt), lambda b, t: (b, 0, t)), tok_spec, kv_spec, kv_spec, lf_spec]
        out_shape = [tok_shape, jax.ShapeDtypeStruct((B, A, T), BF16), tok_shape, kv_shape, kv_shape, lf_shape]
        args = (x, mod, g, wqkv, wf, bf)
    else:
        assert T == tt
        P = cache_logf.shape[2]
        kv_spec = pl.BlockSpec((bb, N_HEADS, tt, HEAD_DIM), lambda b, t: (b, 0, t, 0))
        kv_shape = jax.ShapeDtypeStruct((B, N_HEADS, T, HEAD_DIM), F32)
        in_specs.append(pl.BlockSpec((bb, N_HEADS, P), per_b))
        out_specs = [tok_spec, tok_spec, tok_spec, kv_spec, kv_spec, lf_spec,
                     pl.BlockSpec((bb, N_HEADS, P), per_b), lf_spec]
        out_shape = [tok_shape, tok_shape, tok_shape, kv_shape, kv_shape, lf_shape,
                     jax.ShapeDtypeStruct((B, N_HEADS, P), F32), lf_shape]
        args = (x, mod, g, wqkv, wf, bf, cache_logf)
    return pl.pallas_call(
        functools.partial(_qkv_kernel, prompt=prompt),
        grid=(B // bb, T // tt),
        in_specs=in_specs,
        out_specs=out_specs,
        out_shape=out_shape,
        compiler_params=_params(("parallel", "parallel")),
        name="qkv_proj",
    )(*args)


def _attn_rows(q2, kt_ref, v1_ref, negf_ref, s_scr, p_scr, nk, blk):
    rows = HEADS_PER_BLOCK * blk
    half = blk // 2
    nkeys = nk * blk
    keep = lax.broadcasted_iota(jnp.int32, (blk, blk), 1) <= lax.broadcasted_iota(jnp.int32, (blk, blk), 0)
    s = jnp.dot(q2, kt_ref[0, :, 0:nkeys], preferred_element_type=F32)
    for kb in range(nk):
        ks = slice(kb * blk, (kb + 1) * blk)
        for h in range(HEADS_PER_BLOCK):
            t = s[h * blk:(h + 1) * blk, ks] + negf_ref[h:h + 1, ks]
            if kb == nk - 1:
                t = jnp.where(keep, t, -jnp.inf)
            s_scr[h * blk:(h + 1) * blk, ks] = t

    for r0 in range(0, rows, ATTN_ROWS):
        rs = slice(r0, r0 + ATTN_ROWS)
        mx = None
        for kb in range(nk):
            t = s_scr[rs, kb * blk:(kb + 1) * blk]
            t = jnp.maximum(t[:, :half], t[:, half:])
            mx = t if mx is None else jnp.maximum(mx, t)
        mb = jnp.broadcast_to(jnp.max(mx, axis=-1, keepdims=True), (ATTN_ROWS, blk))
        for kb in range(nk):
            ks = slice(kb * blk, (kb + 1) * blk)
            p_scr[rs, ks] = jnp.exp2(s_scr[rs, ks] - mb).astype(BF16)

    acc = jnp.dot(p_scr[:, 0:nkeys], v1_ref[0:nkeys, :], preferred_element_type=F32)
    return acc[:, :LANES] * (1.0 / acc[:, LANES:])


def _attn_prompt_kernel(q_ref, kt_ref, v_ref, lf_ref, o_ref, negf_scr, v1_scr, s_scr, p_scr, *, blk):
    T = v_ref.shape[1]
    negf_scr[...] = -LOG2E * _cumsum_lanes(lf_ref[0, 0])
    for pr in range(ATTN_PAIRS):
        v1_scr[pr, :, :LANES] = v_ref[0, :, pr * LANES:(pr + 1) * LANES]
        v1_scr[pr, :, LANES:] = jnp.ones((T, LANES), v1_scr.dtype)
    lane = lax.broadcasted_iota(jnp.int32, (blk, LANES), 1)
    nq = T // blk
    order = list(range(0, nq, 2)) + list(range(nq - 1 - nq % 2, 0, -2))
    slot = 0
    for c in order:
        for pr in range(ATTN_PAIRS):
            lanes = slice(pr * LANES, (pr + 1) * LANES)
            q = q_ref[0, c * blk:(c + 1) * blk, lanes]
            zero = jnp.zeros_like(q)
            q2 = jnp.concatenate([jnp.where(lane < HEAD_DIM, q, zero), jnp.where(lane >= HEAD_DIM, q, zero)],
                                 axis=0)
            o = _attn_rows(q2, kt_ref.at[:, lanes, :], v1_scr.at[pr],
                           negf_scr.at[pr * HEADS_PER_BLOCK:(pr + 1) * HEADS_PER_BLOCK, :],
                           s_scr.at[slot % ATTN_SLOTS], p_scr.at[slot % ATTN_SLOTS], c + 1, blk)
            o_ref[0, c * blk:(c + 1) * blk, lanes] = jnp.where(lane < HEAD_DIM, o[:blk], o[blk:]).astype(o_ref.dtype)
            slot += 1


def _attn_prompt(q, kt, v, logf, *, blk):
    B, T, A = q.shape
    width = ATTN_PAIRS * LANES
    ngrp = A // width
    heads = ATTN_PAIRS * HEADS_PER_BLOCK
    lf = logf.reshape(B, ngrp, heads, T)
    kern = functools.partial(_attn_prompt_kernel, blk=blk)
    grp = pl.BlockSpec((1, T, width), lambda b, p: (b, 0, p))
    grp_t = pl.BlockSpec((1, width, T), lambda b, p: (b, p, 0))
    return pl.pallas_call(
        kern,
        grid=(B, ngrp),
        in_specs=[grp, grp_t, grp, pl.BlockSpec((1, 1, heads, T), lambda b, p: (b, p, 0, 0))],
        out_specs=grp,
        out_shape=jax.ShapeDtypeStruct((B, T, A), BF16),
        scratch_shapes=[
            pltpu.VMEM((heads, T), F32),
            pltpu.VMEM((ATTN_PAIRS, T, 2 * LANES), BF16),
            pltpu.VMEM((ATTN_SLOTS, HEADS_PER_BLOCK * blk, T), F32),
            pltpu.VMEM((ATTN_SLOTS, HEADS_PER_BLOCK * blk, T), BF16),
        ],
        compiler_params=_params(("parallel", "parallel")),
        name="attn_prompt",
    )(q, kt, v, lf)


def _attn_sample_kernel(q_ref, kn_ref, vn_ref, ckt_ref, cvt_ref, fp_ref, fn_ref, o_ref):
    tt = q_ref.shape[1]
    p_len = ckt_ref.shape[3]
    width = SAMPLE_STACK * HEAD_DIM
    nt = (((1,), (1,)), ((), ()))
    lane_head = lax.broadcasted_iota(jnp.int32, (tt, width), 1) // HEAD_DIM
    keep = lax.broadcasted_iota(jnp.int32, (tt, tt), 1) <= lax.broadcasted_iota(jnp.int32, (tt, tt), 0)
    for g in range(ckt_ref.shape[1] // SAMPLE_STACK):
        heads = range(g * SAMPLE_STACK, (g + 1) * SAMPLE_STACK)
        lanes = slice(g * width, (g + 1) * width)
        q = q_ref[0][:, lanes]
        zero = jnp.zeros_like(q)
        qs = jnp.concatenate([jnp.where(lane_head == i, q, zero) for i in range(SAMPLE_STACK)], axis=0)
        kt = ckt_ref[0, heads.start:heads.stop].reshape(width, p_len).astype(BF16)
        vt = cvt_ref[0, heads.start:heads.stop].reshape(width, p_len).astype(BF16)
        s_p = jnp.dot(qs, kt, preferred_element_type=F32)
        s_n = lax.dot_general(qs, kn_ref[0][:, lanes], nt, preferred_element_type=F32)
        s_p = jnp.concatenate([s_p[i * tt:(i + 1) * tt] - fp_ref[0, 0, hd:hd + 1, :]
                               for i, hd in enumerate(heads)], axis=0)
        s_n = jnp.concatenate([jnp.where(keep, s_n[i * tt:(i + 1) * tt] - fn_ref[0, 0, hd:hd + 1, :], -jnp.inf)
                               for i, hd in enumerate(heads)], axis=0)
        m = jnp.maximum(jnp.max(s_p, axis=-1, keepdims=True), jnp.max(s_n, axis=-1, keepdims=True))
        p_p = jnp.exp2(s_p - m)
        p_n = jnp.exp2(s_n - m)
        l = jnp.sum(p_p, axis=-1, keepdims=True) + jnp.sum(p_n, axis=-1, keepdims=True)
        o = (lax.dot_general(p_p.astype(BF16), vt, nt, preferred_element_type=F32)
             + jnp.dot(p_n.astype(BF16), vn_ref[0][:, lanes], preferred_element_type=F32)) * (1.0 / l)
        out = jnp.zeros((tt, width), F32)
        for i in range(SAMPLE_STACK):
            out = jnp.where(lane_head == i, o[i * tt:(i + 1) * tt], out)
        o_ref[0, :, lanes] = out.astype(o_ref.dtype)


def _attn_sample(q, k, v, cache_kt, cache_vt, f_past, f_new):
    B, T, A = q.shape
    P = cache_kt.shape[3]
    hg = SAMPLE_HEAD_GROUP
    ngrp = N_HEADS // hg
    clf = f_past.reshape(B, ngrp, hg, P)
    lf = f_new.reshape(B, ngrp, hg, T)
    tok = lambda b, p: (b, 0, p)
    grp = lambda b, p: (b, p, 0, 0)
    return pl.pallas_call(
        _attn_sample_kernel,
        grid=(B, ngrp),
        in_specs=[
            pl.BlockSpec((1, T, hg * HEAD_DIM), tok),
            pl.BlockSpec((1, T, hg * HEAD_DIM), tok),
            pl.BlockSpec((1, T, hg * HEAD_DIM), tok),
            pl.BlockSpec((1, hg, HEAD_DIM, P), grp),
            pl.BlockSpec((1, hg, HEAD_DIM, P), grp),
            pl.BlockSpec((1, 1, hg, P), grp),
            pl.BlockSpec((1, 1, hg, T), grp),
        ],
        out_specs=pl.BlockSpec((1, T, hg * HEAD_DIM), tok),
        out_shape=jax.ShapeDtypeStruct((B, T, A), BF16),
        compiler_params=_params(("parallel", "parallel")),
        name="attn_sample",
    )(q, k, v, cache_kt, cache_vt, clf, lf)


def _mix_kernel(x_ref, o_ref, mod_ref, g_ref, w5_ref, cw_ref, pu_ref, wo_ref, y_ref, nu_ref, carry_scr):
    bb, tt, d = x_ref.shape
    tm = bb * tt
    t = pl.program_id(1)

    @pl.when(t == 0)
    def _():
        carry_scr[...] = pu_ref[...]

    x = x_ref[...]
    shift = mod_ref[:, 3:4, :]
    scale = mod_ref[:, 4:5, :]
    gate = mod_ref[:, 5:6, :]
    h = _rms_mod(x, g_ref[...], scale, shift).reshape(tm, d).astype(BF16)
    cdim = w5_ref.shape[0] // 5
    proj = lambda i: _dot_t(h, w5_ref[i * cdim:(i + 1) * cdim, :])

    u2 = proj(1) * proj(2)
    u = u2.reshape(bb, tt, cdim)
    tpos = lax.broadcasted_iota(jnp.int32, (1, tt, 1), 1)
    c0 = carry_scr[:, 0:1, :]
    c1 = carry_scr[:, 1:2, :]
    um1 = jnp.where(tpos == 0, c1, pltpu.roll(u2, 1, axis=0).reshape(bb, tt, cdim))
    um2 = jnp.where(tpos == 0, c0, jnp.where(tpos == 1, c1, pltpu.roll(u2, 2, axis=0).reshape(bb, tt, cdim)))
    conv = cw_ref[0:1, :] * um2 + cw_ref[1:2, :] * um1 + cw_ref[2:3, :] * u
    new_u = u[:, tt - (CONV_WIDTH - 1):, :]
    carry_scr[...] = new_u
    nu_ref[...] = new_u

    o_conv = proj(0) * conv.reshape(tm, cdim)
    m = _sigmoid(proj(3)) * o_ref[...].reshape(tm, cdim).astype(F32) + _sigmoid(proj(4)) * o_conv
    mo = jnp.dot(m.astype(BF16), wo_ref[...], preferred_element_type=F32)
    y_ref[...] = x + gate * mo.reshape(bb, tt, d)


def _mix(x, o_attn, mod, g, w5, conv_w, past_u, w_out, *, bb, tt):
    B, T, D = x.shape
    C = w5.shape[0] // 5
    tok = lambda b, t: (b, t, 0)
    per_b = lambda b, t: (b, 0, 0)
    return pl.pallas_call(
        _mix_kernel,
        grid=(B // bb, T // tt),
        in_specs=[
            pl.BlockSpec((bb, tt, D), tok),
            pl.BlockSpec((bb, tt, C), tok),
            pl.BlockSpec((bb, N_MOD, D), per_b),
            _resident((1, D)),
            _resident(w5.shape),
            _resident(conv_w.shape),
            pl.BlockSpec((bb, CONV_WIDTH - 1, C), per_b),
            _resident(w_out.shape),
        ],
        out_specs=[
            pl.BlockSpec((bb, tt, D), tok),
            pl.BlockSpec((bb, CONV_WIDTH - 1, C), per_b),
        ],
        out_shape=[
            jax.ShapeDtypeStruct((B, T, D), F32),
            jax.ShapeDtypeStruct((B, CONV_WIDTH - 1, C), F32),
        ],
        scratch_shapes=[pltpu.VMEM((bb, CONV_WIDTH - 1, C), F32)],
        compiler_params=_params(("parallel", "arbitrary")),
        name="mix",
    )(x, o_attn, mod, g, w5, conv_w, past_u, w_out)


def _layer(x, mod, past, weights, g_final, *, bb, tt):
    (g1, w1a, w1b, gm, wqkv, wf, bf, w5, conv_w, w_out, g2, w2a, w2b) = weights
    past_kt, past_vt, past_logf, past_u = past
    prompt = past_kt is None
    x1 = _ffn(x, mod, g1, w1a, w1b, g_final, bb=bb, tt=tt, mod_base=0, final_norm=False)
    if prompt:
        q, kbt, vb, k32, v32, logf = _qkv(x1, mod, gm, wqkv, wf, bf, None, bb=bb, tt=tt)
        o = _attn_prompt(q, kbt, vb, logf, blk=ATTN_BLOCK)
        k32, v32 = jnp.swapaxes(k32, 2, 3), jnp.swapaxes(v32, 2, 3)
    else:
        q, kb, vb, k32, v32, logf, f_past, f_new = _qkv(x1, mod, gm, wqkv, wf, bf, past_logf, bb=bb, tt=tt)
        o = _attn_sample(q, kb, vb, past_kt, past_vt, f_past, f_new)
    x2, new_u = _mix(x1, o, mod, gm, w5, conv_w, past_u, w_out, bb=bb, tt=tt)
    y = _ffn(x2, mod, g2, w2a, w2b, g_final, bb=bb, tt=tt, mod_base=6, final_norm=True)
    return y, k32, v32, logf, new_u


def kernel(x_prompt, x_sample, cache_k, cache_v, cache_logf, state_conv, c_prompt, c_sample, w_ada, b_ada, g_ffn1, w_ffn1_in, w_ffn1_out, g_mix, w_in, b_f, conv_w, w_out, g_ffn2, w_ffn2_in, w_ffn2_out, g_final):
    assert w_ada.shape[0] == 1, "single-layer encoder"
    Bp, Tp, D = x_prompt.shape
    Bs, Ts, _ = x_sample.shape
    A = N_HEADS * HEAD_DIM
    C = conv_w.shape[2]

    mod = _ada(jnp.concatenate([c_prompt, c_sample], axis=0), w_ada[0], b_ada[0])
    mod = mod.reshape(Bp + Bs, N_MOD, D)
    mod_p, mod_s = mod[:Bp], mod[Bp:]

    wi_t = jnp.swapaxes(w_in[0], 0, 1)
    off_f = 3 * A
    off_b = off_f + N_HEADS
    wqkv = wi_t[:off_f].astype(BF16)
    wf = jnp.pad(wi_t[off_f:off_b], ((0, LANES - N_HEADS), (0, 0))).astype(BF16)
    bf = jnp.pad(b_f[0], (0, LANES - N_HEADS)).reshape(1, LANES)
    w5 = wi_t[off_b:].astype(BF16)
    weights = (g_ffn1, w_ffn1_in[0].astype(BF16), w_ffn1_out[0].astype(BF16), g_mix, wqkv, wf, bf, w5,
               conv_w[0], w_out[0].astype(BF16), g_ffn2, w_ffn2_in[0].astype(BF16), w_ffn2_out[0].astype(BF16))
    gfin = g_final.reshape(1, D)

    zero_u = jnp.zeros((Bp, CONV_WIDTH - 1, C), F32)
    yp, kp, vp, fp, up = _layer(x_prompt, mod_p, (None, None, None, zero_u), weights, gfin,
                                bb=1, tt=PROMPT_ROW_TILE)
    past = (jnp.swapaxes(cache_k[0], 2, 3), jnp.swapaxes(cache_v[0], 2, 3), cache_logf[0], state_conv[0])
    ys, ks, vs, fs, us = _layer(x_sample, mod_s, past, weights, gfin, bb=SAMPLE_ROW_TILE // Ts, tt=Ts)
    return (yp, ys, kp[None], vp[None], fp[None], up[None], ks[None], vs[None], fs[None], us[None])
```

```python
import functools

import jax
import jax.numpy as jnp
from jax import lax
from jax.experimental import pallas as pl
from jax.experimental.pallas import tpu as pltpu

F32 = jnp.float32
BF16 = jnp.bfloat16

EPS = 1e-6
N_HEADS = 16
HEAD_DIM = 64
N_MOD = 9
CONV_WIDTH = 3
LANES = 128
HEADS_PER_BLOCK = LANES // HEAD_DIM
V7X_VMEM_LIMIT_BYTES = 56 * 1024 * 1024
FFN_CHUNK = 256
PROMPT_ROW_TILE = 1024
SAMPLE_ROW_TILE = 512
ATTN_BLOCK = 256
ATTN_ROWS = 64
ATTN_PAIRS = 2
ATTN_SLOTS = 4
LOG2E = 1.4426950408889634
Q_SCALE = HEAD_DIM ** -0.5 * LOG2E
SAMPLE_HEAD_GROUP = 8
SAMPLE_STACK = 4


def _params(semantics):
    return pltpu.CompilerParams(dimension_semantics=semantics, vmem_limit_bytes=V7X_VMEM_LIMIT_BYTES)


def _resident(shape):
    zeros = (0,) * len(shape)
    return pl.BlockSpec(shape, lambda *_: zeros, pipeline_mode=pl.Buffered(1))


def _resident_rows(w, row0, nrows):
    assert row0 + nrows <= w.shape[0]
    return pl.BlockSpec((pl.Element(nrows), pl.Element(w.shape[1])), lambda *_: (row0, 0),
                        pipeline_mode=pl.Buffered(1))


def _rms_mod(x, g, scale, shift):
    ms = jnp.mean(x * x, axis=-1, keepdims=True)
    y = x * lax.rsqrt(ms + EPS) * g
    return y * (1.0 + scale) + shift


def _dot_t(x, w_t):
    return lax.dot_general(x, w_t, (((1,), (1,)), ((), ())), preferred_element_type=F32)


def _sigmoid(x):
    return 1.0 / (1.0 + jnp.exp(-x))


def _log_sigmoid(x):
    return jnp.minimum(x, 0.0) - jnp.log1p(jnp.exp(-jnp.abs(x)))


def _cumsum_lanes(x):
    n = x.shape[-1]
    lane = lax.broadcasted_iota(jnp.int32, x.shape, x.ndim - 1)
    step = 1
    while step < n:
        x = x + jnp.where(lane >= step, pltpu.roll(x, step, axis=x.ndim - 1), 0.0)
        step *= 2
    return x


def _ada_kernel(c_ref, w_ref, b_ref, o_ref):
    c = c_ref[...]
    a = (c * _sigmoid(c)).astype(BF16)
    o_ref[...] = jnp.dot(a, w_ref[...].astype(BF16), preferred_element_type=F32) + b_ref[...]


def _ada(c, w_ada, b_ada):
    n, d = c.shape
    cols = w_ada.shape[1]
    tn = d
    return pl.pallas_call(
        _ada_kernel,
        grid=(cols // tn,),
        in_specs=[
            pl.BlockSpec((n, d), lambda j: (0, 0)),
            pl.BlockSpec((d, tn), lambda j: (0, j)),
            pl.BlockSpec((1, tn), lambda j: (0, j)),
        ],
        out_specs=pl.BlockSpec((n, tn), lambda j: (0, j)),
        out_shape=jax.ShapeDtypeStruct((n, cols), F32),
        compiler_params=_params(("parallel",)),
        name="ada_mod",
    )(c, w_ada, b_ada.reshape(1, cols))


def _ffn_kernel(x_ref, mod_ref, g_ref, w1_ref, w2_ref, gfin_ref, o_ref, h_scr, *, mod_base, final_norm):
    bb, tt, d = x_ref.shape
    tm = bb * tt
    f = w2_ref.shape[0]
    fc = FFN_CHUNK
    x = x_ref[...]
    shift = mod_ref[:, mod_base:mod_base + 1, :]
    scale = mod_ref[:, mod_base + 1:mod_base + 2, :]
    gate = mod_ref[:, mod_base + 2:mod_base + 3, :]
    h_scr[...] = _rms_mod(x, g_ref[...], scale, shift).reshape(tm, d).astype(BF16)

    acc = None
    for lo in range(0, f, fc):
        h = h_scr[...]
        a = jnp.dot(h, w1_ref[:, lo:lo + fc], preferred_element_type=F32)
        b = jnp.dot(h, w1_ref[:, f + lo:f + lo + fc], preferred_element_type=F32)
        act = (a * _sigmoid(a) * b).astype(BF16)
        part = jnp.dot(act, w2_ref[lo:lo + fc, :], preferred_element_type=F32)
        acc = part if acc is None else acc + part

    y = x + 0.5 * gate * acc.reshape(bb, tt, d)
    if final_norm:
        ms = jnp.mean(y * y, axis=-1, keepdims=True)
        y = y * lax.rsqrt(ms + EPS) * gfin_ref[...]
    o_ref[...] = y


def _ffn(x, mod, g, w1c, w2c, g_final, *, bb, tt, mod_base, final_norm):
    B, T, D = x.shape
    assert w2c.shape[0] % FFN_CHUNK == 0
    tm = bb * tt
    kern = functools.partial(_ffn_kernel, mod_base=mod_base, final_norm=final_norm)
    return pl.pallas_call(
        kern,
        grid=(B // bb, T // tt),
        in_specs=[
            pl.BlockSpec((bb, tt, D), lambda b, t: (b, t, 0)),
            pl.BlockSpec((bb, N_MOD, D), lambda b, t: (b, 0, 0)),
            _resident((1, D)),
            _resident(w1c.shape),
            _resident(w2c.shape),
            _resident((1, D)),
        ],
        out_specs=pl.BlockSpec((bb, tt, D), lambda b, t: (b, t, 0)),
        out_shape=jax.ShapeDtypeStruct((B, T, D), F32),
        scratch_shapes=[pltpu.VMEM((tm, D), BF16)],
        compiler_params=_params(("parallel", "parallel")),
        name="ffn_final" if final_norm else "ffn",
    )(x, mod, g, w1c, w2c, g_final)


def _qkv_kernel(*refs, prompt):
    if prompt:
        (x_ref, mod_ref, g_ref, wqkv_ref, bf_ref,
         q_ref, kb_ref, vb_ref, k32_ref, v32_ref, lf_ref) = refs
    else:
        (x_ref, mod_ref, g_ref, wqkv_ref, bf_ref, clf_ref,
         q_ref, kb_ref, vb_ref, k32_ref, v32_ref, lf_ref, fp_ref, fn_ref) = refs
    bb, tt, d = x_ref.shape
    tm = bb * tt
    a_dim = (wqkv_ref.shape[0] - LANES) // 3
    shift = mod_ref[:, 3:4, :]
    scale = mod_ref[:, 4:5, :]
    h = _rms_mod(x_ref[...], g_ref[...], scale, shift).reshape(tm, d).astype(BF16)

    zq = _dot_t(h, wqkv_ref[0:a_dim, :])
    q_ref[...] = (zq * Q_SCALE).astype(BF16).reshape(bb, tt, a_dim)

    for w_idx, (lo_ref, hi_ref) in ((1, (kb_ref, k32_ref)), (2, (vb_ref, v32_ref))):
        z = _dot_t(h, wqkv_ref[w_idx * a_dim:(w_idx + 1) * a_dim, :])
        if prompt:
            zt = z.T
            hi_ref[0] = zt.reshape(N_HEADS, HEAD_DIM, tt)
            if w_idx == 1:
                lo_ref[0] = zt.astype(BF16)
            else:
                lo_ref[...] = z.astype(BF16).reshape(bb, tt, a_dim)
        else:
            lo_ref[...] = z.astype(BF16).reshape(bb, tt, a_dim)
            for hd in range(N_HEADS):
                hi_ref[:, hd, :, :] = z[:, hd * HEAD_DIM:(hd + 1) * HEAD_DIM].reshape(bb, tt, HEAD_DIM)

    zf = _dot_t(h, wqkv_ref[3 * a_dim:, :])
    lf = _log_sigmoid(zf + bf_ref[...]).T
    if not prompt:
        p_len = clf_ref.shape[2]
        f_past = LOG2E * _cumsum_lanes(clf_ref[...].reshape(bb * N_HEADS, p_len))
        fp_ref[...] = f_past.reshape(bb, N_HEADS, p_len)
    for b in range(bb):
        lf_b = lf[:N_HEADS, b * tt:(b + 1) * tt]
        lf_ref[b] = lf_b
        if not prompt:
            fn_ref[b] = f_past[b * N_HEADS:(b + 1) * N_HEADS, p_len - 1:p_len] + LOG2E * _cumsum_lanes(lf_b)


def _qkv(x, mod, g, w_t, bf, cache_logf, *, bb, tt):
    B, T, D = x.shape
    A = N_HEADS * HEAD_DIM
    prompt = cache_logf is None
    tok = lambda b, t: (b, t, 0)
    per_b = lambda b, t: (b, 0, 0)
    in_specs = [
        pl.BlockSpec((bb, tt, D), tok),
        pl.BlockSpec((bb, N_MOD, D), per_b),
        _resident((1, D)),
        _resident_rows(w_t, 0, 3 * A + LANES),
        _resident(bf.shape),
    ]
    tok_spec = pl.BlockSpec((bb, tt, A), tok)
    tok_shape = jax.ShapeDtypeStruct((B, T, A), BF16)
    lf_spec = pl.BlockSpec((bb, N_HEADS, tt), lambda b, t: (b, 0, t))
    lf_shape = jax.ShapeDtypeStruct((B, N_HEADS, T), F32)
    if prompt:
        assert bb == 1
        kv_spec = pl.BlockSpec((1, N_HEADS, HEAD_DIM, tt), lambda b, t: (b, 0, 0, t))
        kv_shape = jax.ShapeDtypeStruct((B, N_HEADS, HEAD_DIM, T), F32)
        out_specs = [tok_spec, pl.BlockSpec((1, A, tt), lambda b, t: (b, 0, t)), tok_spec, kv_spec, kv_spec, lf_spec]
        out_shape = [tok_shape, jax.ShapeDtypeStruct((B, A, T), BF16), tok_shape, kv_shape, kv_shape, lf_shape]
        args = (x, mod, g, w_t, bf)
    else:
        assert T == tt
        P = cache_logf.shape[2]
        kv_spec = pl.BlockSpec((bb, N_HEADS, tt, HEAD_DIM), lambda b, t: (b, 0, t, 0))
        kv_shape = jax.ShapeDtypeStruct((B, N_HEADS, T, HEAD_DIM), F32)
        in_specs.append(pl.BlockSpec((bb, N_HEADS, P), per_b))
        out_specs = [tok_spec, tok_spec, tok_spec, kv_spec, kv_spec, lf_spec,
                     pl.BlockSpec((bb, N_HEADS, P), per_b), lf_spec]
        out_shape = [tok_shape, tok_shape, tok_shape, kv_shape, kv_shape, lf_shape,
                     jax.ShapeDtypeStruct((B, N_HEADS, P), F32), lf_shape]
        args = (x, mod, g, w_t, bf, cache_logf)
    return pl.pallas_call(
        functools.partial(_qkv_kernel, prompt=prompt),
        grid=(B // bb, T // tt),
        in_specs=in_specs,
        out_specs=out_specs,
        out_shape=out_shape,
        compiler_params=_params(("parallel", "parallel")),
        name="qkv_proj",
    )(*args)


def _attn_rows(q2, kt_ref, v1_ref, negf_ref, s_scr, p_scr, nk, blk):
    rows = HEADS_PER_BLOCK * blk
    half = blk // 2
    nkeys = nk * blk
    keep = lax.broadcasted_iota(jnp.int32, (blk, blk), 1) <= lax.broadcasted_iota(jnp.int32, (blk, blk), 0)
    s = jnp.dot(q2, kt_ref[0, :, 0:nkeys], preferred_element_type=F32)
    for kb in range(nk):
        ks = slice(kb * blk, (kb + 1) * blk)
        for h in range(HEADS_PER_BLOCK):
            t = s[h * blk:(h + 1) * blk, ks] + negf_ref[h:h + 1, ks]
            if kb == nk - 1:
                t = jnp.where(keep, t, -jnp.inf)
            s_scr[h * blk:(h + 1) * blk, ks] = t

    for r0 in range(0, rows, ATTN_ROWS):
        rs = slice(r0, r0 + ATTN_ROWS)
        mx = None
        for kb in range(nk):
            t = s_scr[rs, kb * blk:(kb + 1) * blk]
            t = jnp.maximum(t[:, :half], t[:, half:])
            mx = t if mx is None else jnp.maximum(mx, t)
        mb = jnp.broadcast_to(jnp.max(mx, axis=-1, keepdims=True), (ATTN_ROWS, blk))
        for kb in range(nk):
            ks = slice(kb * blk, (kb + 1) * blk)
            p_scr[rs, ks] = jnp.exp2(s_scr[rs, ks] - mb).astype(BF16)

    acc = jnp.dot(p_scr[:, 0:nkeys], v1_ref[0:nkeys, :], preferred_element_type=F32)
    return acc[:, :LANES] * (1.0 / acc[:, LANES:])


def _attn_prompt_kernel(q_ref, kt_ref, v_ref, lf_ref, o_ref, negf_scr, v1_scr, s_scr, p_scr, *, blk):
    T = v_ref.shape[1]
    negf_scr[...] = -LOG2E * _cumsum_lanes(lf_ref[0, 0])
    for pr in range(ATTN_PAIRS):
        v1_scr[pr, :, :LANES] = v_ref[0, :, pr * LANES:(pr + 1) * LANES]
        v1_scr[pr, :, LANES:] = jnp.ones((T, LANES), v1_scr.dtype)
    lane = lax.broadcasted_iota(jnp.int32, (blk, LANES), 1)
    nq = T // blk
    order = list(range(0, nq, 2)) + list(range(nq - 1 - nq % 2, 0, -2))
    slot = 0
    for c in order:
        for pr in range(ATTN_PAIRS):
            lanes = slice(pr * LANES, (pr + 1) * LANES)
            q = q_ref[0, c * blk:(c + 1) * blk, lanes]
            zero = jnp.zeros_like(q)
            q2 = jnp.concatenate([jnp.where(lane < HEAD_DIM, q, zero), jnp.where(lane >= HEAD_DIM, q, zero)],
                                 axis=0)
            o = _attn_rows(q2, kt_ref.at[:, lanes, :], v1_scr.at[pr],
                           negf_scr.at[pr * HEADS_PER_BLOCK:(pr + 1) * HEADS_PER_BLOCK, :],
                           s_scr.at[slot % ATTN_SLOTS], p_scr.at[slot % ATTN_SLOTS], c + 1, blk)
            o_ref[0, c * blk:(c + 1) * blk, lanes] = jnp.where(lane < HEAD_DIM, o[:blk], o[blk:]).astype(o_ref.dtype)
            slot += 1


def _attn_prompt(q, kt, v, logf, *, blk):
    B, T, A = q.shape
    width = ATTN_PAIRS * LANES
    ngrp = A // width
    heads = ATTN_PAIRS * HEADS_PER_BLOCK
    lf = logf.reshape(B, ngrp, heads, T)
    kern = functools.partial(_attn_prompt_kernel, blk=blk)
    grp = pl.BlockSpec((1, T, width), lambda b, p: (b, 0, p))
    grp_t = pl.BlockSpec((1, width, T), lambda b, p: (b, p, 0))
    return pl.pallas_call(
        kern,
        grid=(B, ngrp),
        in_specs=[grp, grp_t, grp, pl.BlockSpec((1, 1, heads, T), lambda b, p: (b, p, 0, 0))],
        out_specs=grp,
        out_shape=jax.ShapeDtypeStruct((B, T, A), BF16),
        scratch_shapes=[
            pltpu.VMEM((heads, T), F32),
            pltpu.VMEM((ATTN_PAIRS, T, 2 * LANES), BF16),
            pltpu.VMEM((ATTN_SLOTS, HEADS_PER_BLOCK * blk, T), F32),
            pltpu.VMEM((ATTN_SLOTS, HEADS_PER_BLOCK * blk, T), BF16),
        ],
        compiler_params=_params(("parallel", "parallel")),
        name="attn_prompt",
    )(q, kt, v, lf)


def _attn_sample_kernel(q_ref, kn_ref, vn_ref, ckt_ref, cvt_ref, fp_ref, fn_ref, o_ref):
    tt = q_ref.shape[1]
    p_len = ckt_ref.shape[3]
    width = SAMPLE_STACK * HEAD_DIM
    nt = (((1,), (1,)), ((), ()))
    lane_head = lax.broadcasted_iota(jnp.int32, (tt, width), 1) // HEAD_DIM
    keep = lax.broadcasted_iota(jnp.int32, (tt, tt), 1) <= lax.broadcasted_iota(jnp.int32, (tt, tt), 0)
    for g in range(ckt_ref.shape[1] // SAMPLE_STACK):
        heads = range(g * SAMPLE_STACK, (g + 1) * SAMPLE_STACK)
        lanes = slice(g * width, (g + 1) * width)
        q = q_ref[0][:, lanes]
        zero = jnp.zeros_like(q)
        qs = jnp.concatenate([jnp.where(lane_head == i, q, zero) for i in range(SAMPLE_STACK)], axis=0)
        kt = ckt_ref[0, heads.start:heads.stop].reshape(width, p_len).astype(BF16)
        vt = cvt_ref[0, heads.start:heads.stop].reshape(width, p_len).astype(BF16)
        s_p = jnp.dot(qs, kt, preferred_element_type=F32)
        s_n = lax.dot_general(qs, kn_ref[0][:, lanes], nt, preferred_element_type=F32)
        s_p = jnp.concatenate([s_p[i * tt:(i + 1) * tt] - fp_ref[0, 0, hd:hd + 1, :]
                               for i, hd in enumerate(heads)], axis=0)
        s_n = jnp.concatenate([jnp.where(keep, s_n[i * tt:(i + 1) * tt] - fn_ref[0, 0, hd:hd + 1, :], -jnp.inf)
                               for i, hd in enumerate(heads)], axis=0)
        m = jnp.maximum(jnp.max(s_p, axis=-1, keepdims=True), jnp.max(s_n, axis=-1, keepdims=True))
        p_p = jnp.exp2(s_p - m)
        p_n = jnp.exp2(s_n - m)
        l = jnp.sum(p_p, axis=-1, keepdims=True) + jnp.sum(p_n, axis=-1, keepdims=True)
        o = (lax.dot_general(p_p.astype(BF16), vt, nt, preferred_element_type=F32)
             + jnp.dot(p_n.astype(BF16), vn_ref[0][:, lanes], preferred_element_type=F32)) * (1.0 / l)
        out = jnp.zeros((tt, width), F32)
        for i in range(SAMPLE_STACK):
            out = jnp.where(lane_head == i, o[i * tt:(i + 1) * tt], out)
        o_ref[0, :, lanes] = out.astype(o_ref.dtype)


def _attn_sample(q, k, v, cache_kt, cache_vt, f_past, f_new):
    B, T, A = q.shape
    P = cache_kt.shape[3]
    hg = SAMPLE_HEAD_GROUP
    ngrp = N_HEADS // hg
    clf = f_past.reshape(B, ngrp, hg, P)
    lf = f_new.reshape(B, ngrp, hg, T)
    tok = lambda b, p: (b, 0, p)
    grp = lambda b, p: (b, p, 0, 0)
    return pl.pallas_call(
        _attn_sample_kernel,
        grid=(B, ngrp),
        in_specs=[
            pl.BlockSpec((1, T, hg * HEAD_DIM), tok),
            pl.BlockSpec((1, T, hg * HEAD_DIM), tok),
            pl.BlockSpec((1, T, hg * HEAD_DIM), tok),
            pl.BlockSpec((1, hg, HEAD_DIM, P), grp),
            pl.BlockSpec((1, hg, HEAD_DIM, P), grp),
            pl.BlockSpec((1, 1, hg, P), grp),
            pl.BlockSpec((1, 1, hg, T), grp),
        ],
        out_specs=pl.BlockSpec((1, T, hg * HEAD_DIM), tok),
        out_shape=jax.ShapeDtypeStruct((B, T, A), BF16),
        compiler_params=_params(("parallel", "parallel")),
        name="attn_sample",
    )(q, k, v, cache_kt, cache_vt, clf, lf)


def _mix_kernel(x_ref, o_ref, mod_ref, g_ref, w5_ref, cw_ref, pu_ref, wo_ref, y_ref, nu_ref, carry_scr):
    bb, tt, d = x_ref.shape
    tm = bb * tt
    t = pl.program_id(1)

    @pl.when(t == 0)
    def _():
        carry_scr[...] = pu_ref[...]

    x = x_ref[...]
    shift = mod_ref[:, 3:4, :]
    scale = mod_ref[:, 4:5, :]
    gate = mod_ref[:, 5:6, :]
    h = _rms_mod(x, g_ref[...], scale, shift).reshape(tm, d).astype(BF16)
    cdim = w5_ref.shape[0] // 5
    proj = lambda i: _dot_t(h, w5_ref[i * cdim:(i + 1) * cdim, :])

    u2 = proj(1) * proj(2)
    u = u2.reshape(bb, tt, cdim)
    tpos = lax.broadcasted_iota(jnp.int32, (1, tt, 1), 1)
    c0 = carry_scr[:, 0:1, :]
    c1 = carry_scr[:, 1:2, :]
    um1 = jnp.where(tpos == 0, c1, pltpu.roll(u2, 1, axis=0).reshape(bb, tt, cdim))
    um2 = jnp.where(tpos == 0, c0, jnp.where(tpos == 1, c1, pltpu.roll(u2, 2, axis=0).reshape(bb, tt, cdim)))
    conv = cw_ref[0:1, :] * um2 + cw_ref[1:2, :] * um1 + cw_ref[2:3, :] * u
    new_u = u[:, tt - (CONV_WIDTH - 1):, :]
    carry_scr[...] = new_u
    nu_ref[...] = new_u

    o_conv = proj(0) * conv.reshape(tm, cdim)
    m = _sigmoid(proj(3)) * o_ref[...].reshape(tm, cdim).astype(F32) + _sigmoid(proj(4)) * o_conv
    mo = jnp.dot(m.astype(BF16), wo_ref[...], preferred_element_type=F32)
    y_ref[...] = x + gate * mo.reshape(bb, tt, d)


def _mix(x, o_attn, mod, g, w_t, conv_w, past_u, w_out, *, bb, tt):
    B, T, D = x.shape
    C = conv_w.shape[1]
    tok = lambda b, t: (b, t, 0)
    per_b = lambda b, t: (b, 0, 0)
    return pl.pallas_call(
        _mix_kernel,
        grid=(B // bb, T // tt),
        in_specs=[
            pl.BlockSpec((bb, tt, D), tok),
            pl.BlockSpec((bb, tt, C), tok),
            pl.BlockSpec((bb, N_MOD, D), per_b),
            _resident((1, D)),
            _resident_rows(w_t, w_t.shape[0] - 5 * C, 5 * C),
            _resident(conv_w.shape),
            pl.BlockSpec((bb, CONV_WIDTH - 1, C), per_b),
            _resident(w_out.shape),
        ],
        out_specs=[
            pl.BlockSpec((bb, tt, D), tok),
            pl.BlockSpec((bb, CONV_WIDTH - 1, C), per_b),
        ],
        out_shape=[
            jax.ShapeDtypeStruct((B, T, D), F32),
            jax.ShapeDtypeStruct((B, CONV_WIDTH - 1, C), F32),
        ],
        scratch_shapes=[pltpu.VMEM((bb, CONV_WIDTH - 1, C), F32)],
        compiler_params=_params(("parallel", "arbitrary")),
        name="mix",
    )(x, o_attn, mod, g, w_t, conv_w, past_u, w_out)


def _layer(x, mod, past, weights, g_final, *, bb, tt):
    (g1, w1a, w1b, gm, w_t, bf, conv_w, w_out, g2, w2a, w2b) = weights
    past_kt, past_vt, past_logf, past_u = past
    prompt = past_kt is None
    x1 = _ffn(x, mod, g1, w1a, w1b, g_final, bb=bb, tt=tt, mod_base=0, final_norm=False)
    if prompt:
        q, kbt, vb, k32, v32, logf = _qkv(x1, mod, gm, w_t, bf, None, bb=bb, tt=tt)
        o = _attn_prompt(q, kbt, vb, logf, blk=ATTN_BLOCK)
        k32, v32 = jnp.swapaxes(k32, 2, 3), jnp.swapaxes(v32, 2, 3)
    else:
        q, kb, vb, k32, v32, logf, f_past, f_new = _qkv(x1, mod, gm, w_t, bf, past_logf, bb=bb, tt=tt)
        o = _attn_sample(q, kb, vb, past_kt, past_vt, f_past, f_new)
    x2, new_u = _mix(x1, o, mod, gm, w_t, conv_w, past_u, w_out, bb=bb, tt=tt)
    y = _ffn(x2, mod, g2, w2a, w2b, g_final, bb=bb, tt=tt, mod_base=6, final_norm=True)
    return y, k32, v32, logf, new_u


def kernel(x_prompt, x_sample, cache_k, cache_v, cache_logf, state_conv, c_prompt, c_sample, w_ada, b_ada, g_ffn1, w_ffn1_in, w_ffn1_out, g_mix, w_in, b_f, conv_w, w_out, g_ffn2, w_ffn2_in, w_ffn2_out, g_final):
    assert w_ada.shape[0] == 1, "single-layer encoder"
    Bp, Tp, D = x_prompt.shape
    Bs, Ts, _ = x_sample.shape
    A = N_HEADS * HEAD_DIM
    C = conv_w.shape[2]

    mod = _ada(jnp.concatenate([c_prompt, c_sample], axis=0), w_ada[0], b_ada[0])
    mod = mod.reshape(Bp + Bs, N_MOD, D)
    mod_p, mod_s = mod[:Bp], mod[Bp:]

    assert w_in.shape[2] == 3 * A + N_HEADS + 5 * C
    w_t = jnp.swapaxes(w_in[0], 0, 1).astype(BF16)
    bf = jnp.pad(b_f[0], (0, LANES - N_HEADS)).reshape(1, LANES)
    weights = (g_ffn1, w_ffn1_in[0].astype(BF16), w_ffn1_out[0].astype(BF16), g_mix, w_t, bf,
               conv_w[0], w_out[0].astype(BF16), g_ffn2, w_ffn2_in[0].astype(BF16), w_ffn2_out[0].astype(BF16))
    gfin = g_final.reshape(1, D)

    zero_u = jnp.zeros((Bp, CONV_WIDTH - 1, C), F32)
    yp, kp, vp, fp, up = _layer(x_prompt, mod_p, (None, None, None, zero_u), weights, gfin,
                                bb=1, tt=PROMPT_ROW_TILE)
    past = (jnp.swapaxes(cache_k[0], 2, 3), jnp.swapaxes(cache_v[0], 2, 3), cache_logf[0], state_conv[0])
    ys, ks, vs, fs, us = _layer(x_sample, mod_s, past, weights, gfin, bb=SAMPLE_ROW_TILE // Ts, tt=Ts)
    return (yp, ys, kp[None], vp[None], fp[None], up[None], ks[None], vs[None], fs[None], us[None])
```

```python
import functools

import jax
import jax.numpy as jnp
from jax import lax
from jax.experimental import pallas as pl
from jax.experimental.pallas import tpu as pltpu

F32 = jnp.float32
BF16 = jnp.bfloat16

EPS = 1e-6
N_HEADS = 16
HEAD_DIM = 64
N_MOD = 9
CONV_WIDTH = 3
LANES = 128
HEADS_PER_BLOCK = LANES // HEAD_DIM
V7X_VMEM_LIMIT_BYTES = 56 * 1024 * 1024
FFN_CHUNK = 256
PROMPT_ROW_TILE = 1024
SAMPLE_ROW_TILE = 512
ATTN_BLOCK = 256
ATTN_ROWS = 64
ATTN_PAIRS = 4
ATTN_SLOTS = 4
LOG2E = 1.4426950408889634
Q_SCALE = HEAD_DIM ** -0.5 * LOG2E
SAMPLE_HEAD_GROUP = 16
SAMPLE_STACK = 4


def _params(semantics):
    return pltpu.CompilerParams(dimension_semantics=semantics, vmem_limit_bytes=V7X_VMEM_LIMIT_BYTES)


def _resident(shape):
    zeros = (0,) * len(shape)
    return pl.BlockSpec(shape, lambda *_: zeros, pipeline_mode=pl.Buffered(1))


def _resident_rows(w, row0, nrows):
    assert row0 + nrows <= w.shape[0]
    return pl.BlockSpec((pl.Element(nrows), pl.Element(w.shape[1])), lambda *_: (row0, 0),
                        pipeline_mode=pl.Buffered(1))


def _rms_mod(x, g, scale, shift):
    ms = jnp.mean(x * x, axis=-1, keepdims=True)
    y = x * lax.rsqrt(ms + EPS) * g
    return y * (1.0 + scale) + shift


def _dot_t(x, w_t):
    return lax.dot_general(x, w_t, (((1,), (1,)), ((), ())), preferred_element_type=F32)


def _sigmoid(x):
    return 1.0 / (1.0 + jnp.exp(-x))


def _log_sigmoid(x):
    return jnp.minimum(x, 0.0) - jnp.log1p(jnp.exp(-jnp.abs(x)))


def _cumsum_lanes(x):
    n = x.shape[-1]
    lane = lax.broadcasted_iota(jnp.int32, x.shape, x.ndim - 1)
    step = 1
    while step < n:
        x = x + jnp.where(lane >= step, pltpu.roll(x, step, axis=x.ndim - 1), 0.0)
        step *= 2
    return x


def _ada_kernel(c_ref, w_ref, b_ref, o_ref):
    c = c_ref[...]
    a = (c * _sigmoid(c)).astype(BF16)
    o_ref[...] = jnp.dot(a, w_ref[...].astype(BF16), preferred_element_type=F32) + b_ref[...]


def _ada(c, w_ada, b_ada):
    n, d = c.shape
    cols = w_ada.shape[1]
    tn = d
    return pl.pallas_call(
        _ada_kernel,
        grid=(cols // tn,),
        in_specs=[
            pl.BlockSpec((n, d), lambda j: (0, 0)),
            pl.BlockSpec((d, tn), lambda j: (0, j)),
            pl.BlockSpec((1, tn), lambda j: (0, j)),
        ],
        out_specs=pl.BlockSpec((n, tn), lambda j: (0, j)),
        out_shape=jax.ShapeDtypeStruct((n, cols), F32),
        compiler_params=_params(("parallel",)),
        name="ada_mod",
    )(c, w_ada, b_ada.reshape(1, cols))


def _ffn_tile(x_ref, mod_ref, g_ref, w1_ref, w2_ref, gfin_ref, o_ref, h_scr, *, mod_base, final_norm):
    bb, tt, d = x_ref.shape
    tm = bb * tt
    f = w2_ref.shape[0]
    fc = FFN_CHUNK
    x = x_ref[...]
    shift = mod_ref[:, mod_base:mod_base + 1, :]
    scale = mod_ref[:, mod_base + 1:mod_base + 2, :]
    gate = mod_ref[:, mod_base + 2:mod_base + 3, :]
    h_scr[0:tm, :] = _rms_mod(x, g_ref[...], scale, shift).reshape(tm, d).astype(BF16)

    acc = None
    for lo in range(0, f, fc):
        h = h_scr[0:tm, :]
        a = jnp.dot(h, w1_ref[:, lo:lo + fc], preferred_element_type=F32)
        b = jnp.dot(h, w1_ref[:, f + lo:f + lo + fc], preferred_element_type=F32)
        act = (a * _sigmoid(a) * b).astype(BF16)
        part = jnp.dot(act, w2_ref[lo:lo + fc, :], preferred_element_type=F32)
        acc = part if acc is None else acc + part

    y = x + 0.5 * gate * acc.reshape(bb, tt, d)
    if final_norm:
        ms = jnp.mean(y * y, axis=-1, keepdims=True)
        y = y * lax.rsqrt(ms + EPS) * gfin_ref[...]
    o_ref[...] = y


def _ffn_kernel(xp_ref, modp_ref, xs_ref, mods_ref, g_ref, w1_ref, w2_ref, gfin_ref, op_ref, os_ref, h_scr, *,
                n_prompt, mod_base, final_norm):
    step = pl.program_id(0)
    tile = functools.partial(_ffn_tile, mod_base=mod_base, final_norm=final_norm)

    @pl.when(step < n_prompt)
    def _():
        tile(xp_ref, modp_ref, g_ref, w1_ref, w2_ref, gfin_ref, op_ref, h_scr)

    @pl.when(step >= n_prompt)
    def _():
        tile(xs_ref, mods_ref, g_ref, w1_ref, w2_ref, gfin_ref, os_ref, h_scr)


def _ffn(xp, modp, xs, mods, g, w1c, w2c, g_final, *, tt_p, bb_s, mod_base, final_norm):
    Bp, Tp, D = xp.shape
    Bs, Ts, _ = xs.shape
    assert w2c.shape[0] % FFN_CHUNK == 0 and Tp % tt_p == 0 and Bs % bb_s == 0
    per_b = Tp // tt_p
    n_prompt = Bp * per_b
    n_sample = Bs // bb_s
    p_tile = lambda i: jnp.minimum(i, n_prompt - 1)
    s_tile = lambda i: jnp.maximum(i - n_prompt, 0)
    kern = functools.partial(_ffn_kernel, n_prompt=n_prompt, mod_base=mod_base, final_norm=final_norm)
    return pl.pallas_call(
        kern,
        grid=(n_prompt + n_sample,),
        in_specs=[
            pl.BlockSpec((1, tt_p, D), lambda i: (p_tile(i) // per_b, p_tile(i) % per_b, 0)),
            pl.BlockSpec((1, N_MOD, D), lambda i: (p_tile(i) // per_b, 0, 0)),
            pl.BlockSpec((bb_s, Ts, D), lambda i: (s_tile(i), 0, 0)),
            pl.BlockSpec((bb_s, N_MOD, D), lambda i: (s_tile(i), 0, 0)),
            _resident((1, D)),
            _resident(w1c.shape),
            _resident(w2c.shape),
            _resident((1, D)),
        ],
        out_specs=[
            pl.BlockSpec((1, tt_p, D), lambda i: (p_tile(i) // per_b, p_tile(i) % per_b, 0)),
            pl.BlockSpec((bb_s, Ts, D), lambda i: (s_tile(i), 0, 0)),
        ],
        out_shape=[jax.ShapeDtypeStruct(xp.shape, F32), jax.ShapeDtypeStruct(xs.shape, F32)],
        scratch_shapes=[pltpu.VMEM((max(tt_p, bb_s * Ts), D), BF16)],
        compiler_params=_params(("arbitrary",)),
        name="ffn_final" if final_norm else "ffn",
    )(xp, modp, xs, mods, g, w1c, w2c, g_final)


def _qkv_kernel(*refs, prompt):
    if prompt:
        (x_ref, mod_ref, g_ref, wqkv_ref, bf_ref,
         q_ref, kb_ref, vb_ref, k32_ref, v32_ref, lf_ref) = refs
    else:
        (x_ref, mod_ref, g_ref, wqkv_ref, bf_ref, clf_ref,
         q_ref, kb_ref, vb_ref, k32_ref, v32_ref, lf_ref, fp_ref, fn_ref) = refs
    bb, tt, d = x_ref.shape
    tm = bb * tt
    a_dim = (wqkv_ref.shape[0] - LANES) // 3
    shift = mod_ref[:, 3:4, :]
    scale = mod_ref[:, 4:5, :]
    h = _rms_mod(x_ref[...], g_ref[...], scale, shift).reshape(tm, d).astype(BF16)

    zq = _dot_t(h, wqkv_ref[0:a_dim, :])
    q_ref[...] = (zq * Q_SCALE).astype(BF16).reshape(bb, tt, a_dim)

    for w_idx, (lo_ref, hi_ref) in ((1, (kb_ref, k32_ref)), (2, (vb_ref, v32_ref))):
        z = _dot_t(h, wqkv_ref[w_idx * a_dim:(w_idx + 1) * a_dim, :])
        if prompt:
            zt = z.T
            hi_ref[0] = zt.reshape(N_HEADS, HEAD_DIM, tt)
            if w_idx == 1:
                lo_ref[0] = zt.astype(BF16)
            else:
                lo_ref[...] = z.astype(BF16).reshape(bb, tt, a_dim)
        else:
            lo_ref[...] = z.astype(BF16).reshape(bb, tt, a_dim)
            for hd in range(N_HEADS):
                hi_ref[:, hd, :, :] = z[:, hd * HEAD_DIM:(hd + 1) * HEAD_DIM].reshape(bb, tt, HEAD_DIM)

    zf = _dot_t(h, wqkv_ref[3 * a_dim:, :])
    lf = _log_sigmoid(zf + bf_ref[...]).T
    if not prompt:
        p_len = clf_ref.shape[2]
        f_past = LOG2E * _cumsum_lanes(clf_ref[...].reshape(bb * N_HEADS, p_len))
        fp_ref[...] = f_past.reshape(bb, N_HEADS, p_len)
    for b in range(bb):
        lf_b = lf[:N_HEADS, b * tt:(b + 1) * tt]
        lf_ref[b] = lf_b
        if not prompt:
            fn_ref[b] = f_past[b * N_HEADS:(b + 1) * N_HEADS, p_len - 1:p_len] + LOG2E * _cumsum_lanes(lf_b)


def _qkv(x, mod, g, w_t, bf, cache_logf, *, bb, tt):
    B, T, D = x.shape
    A = N_HEADS * HEAD_DIM
    prompt = cache_logf is None
    tok = lambda b, t: (b, t, 0)
    per_b = lambda b, t: (b, 0, 0)
    in_specs = [
        pl.BlockSpec((bb, tt, D), tok),
        pl.BlockSpec((bb, N_MOD, D), per_b),
        _resident((1, D)),
        _resident_rows(w_t, 0, 3 * A + LANES),
        _resident(bf.shape),
    ]
    tok_spec = pl.BlockSpec((bb, tt, A), tok)
    tok_shape = jax.ShapeDtypeStruct((B, T, A), BF16)
    lf_spec = pl.BlockSpec((bb, N_HEADS, tt), lambda b, t: (b, 0, t))
    lf_shape = jax.ShapeDtypeStruct((B, N_HEADS, T), F32)
    if prompt:
        assert bb == 1
        kv_spec = pl.BlockSpec((1, N_HEADS, HEAD_DIM, tt), lambda b, t: (b, 0, 0, t))
        kv_shape = jax.ShapeDtypeStruct((B, N_HEADS, HEAD_DIM, T), F32)
        out_specs = [tok_spec, pl.BlockSpec((1, A, tt), lambda b, t: (b, 0, t)), tok_spec, kv_spec, kv_spec, lf_spec]
        out_shape = [tok_shape, jax.ShapeDtypeStruct((B, A, T), BF16), tok_shape, kv_shape, kv_shape, lf_shape]
        args = (x, mod, g, w_t, bf)
    else:
        assert T == tt
        P = cache_logf.shape[2]
        kv_spec = pl.BlockSpec((bb, N_HEADS, tt, HEAD_DIM), lambda b, t: (b, 0, t, 0))
        kv_shape = jax.ShapeDtypeStruct((B, N_HEADS, T, HEAD_DIM), F32)
        in_specs.append(pl.BlockSpec((bb, N_HEADS, P), per_b))
        out_specs = [tok_spec, tok_spec, tok_spec, kv_spec, kv_spec, lf_spec,
                     pl.BlockSpec((bb, N_HEADS, P), per_b), lf_spec]
        out_shape = [tok_shape, tok_shape, tok_shape, kv_shape, kv_shape, lf_shape,
                     jax.ShapeDtypeStruct((B, N_HEADS, P), F32), lf_shape]
        args = (x, mod, g, w_t, bf, cache_logf)
    return pl.pallas_call(
        functools.partial(_qkv_kernel, prompt=prompt),
        grid=(B // bb, T // tt),
        in_specs=in_specs,
        out_specs=out_specs,
        out_shape=out_shape,
        compiler_params=_params(("parallel", "parallel")),
        name="qkv_proj",
    )(*args)


def _attn_rows(q2, kt_ref, v1_ref, negf_ref, s_scr, p_scr, nk, blk):
    rows = HEADS_PER_BLOCK * blk
    half = blk // 2
    nkeys = nk * blk
    keep = lax.broadcasted_iota(jnp.int32, (blk, blk), 1) <= lax.broadcasted_iota(jnp.int32, (blk, blk), 0)
    s = jnp.dot(q2, kt_ref[0, :, 0:nkeys], preferred_element_type=F32)
    for kb in range(nk):
        ks = slice(kb * blk, (kb + 1) * blk)
        for h in range(HEADS_PER_BLOCK):
            t = s[h * blk:(h + 1) * blk, ks] + negf_ref[h:h + 1, ks]
            if kb == nk - 1:
                t = jnp.where(keep, t, -jnp.inf)
            s_scr[h * blk:(h + 1) * blk, ks] = t

    for r0 in range(0, rows, ATTN_ROWS):
        rs = slice(r0, r0 + ATTN_ROWS)
        mx = None
        for kb in range(nk):
            t = s_scr[rs, kb * blk:(kb + 1) * blk]
            t = jnp.maximum(t[:, :half], t[:, half:])
            mx = t if mx is None else jnp.maximum(mx, t)
        mb = jnp.broadcast_to(jnp.max(mx, axis=-1, keepdims=True), (ATTN_ROWS, blk))
        for kb in range(nk):
            ks = slice(kb * blk, (kb + 1) * blk)
            p_scr[rs, ks] = jnp.exp2(s_scr[rs, ks] - mb).astype(BF16)

    acc = jnp.dot(p_scr[:, 0:nkeys], v1_ref[0:nkeys, :], preferred_element_type=F32)
    return acc[:, :LANES] * (1.0 / acc[:, LANES:])


def _attn_prompt_kernel(q_ref, kt_ref, v_ref, lf_ref, o_ref, negf_scr, v1_scr, s_scr, p_scr, *, blk):
    T = v_ref.shape[1]
    negf_scr[...] = -LOG2E * _cumsum_lanes(lf_ref[0, 0])
    for pr in range(ATTN_PAIRS):
        v1_scr[pr, :, :LANES] = v_ref[0, :, pr * LANES:(pr + 1) * LANES]
        v1_scr[pr, :, LANES:] = jnp.ones((T, LANES), v1_scr.dtype)
    lane = lax.broadcasted_iota(jnp.int32, (blk, LANES), 1)
    nq = T // blk
    order = list(range(0, nq, 2)) + list(range(nq - 1 - nq % 2, 0, -2))
    slot = 0
    for c in order:
        for pr in range(ATTN_PAIRS):
            lanes = slice(pr * LANES, (pr + 1) * LANES)
            q = q_ref[0, c * blk:(c + 1) * blk, lanes]
            zero = jnp.zeros_like(q)
            q2 = jnp.concatenate([jnp.where(lane < HEAD_DIM, q, zero), jnp.where(lane >= HEAD_DIM, q, zero)],
                                 axis=0)
            o = _attn_rows(q2, kt_ref.at[:, lanes, :], v1_scr.at[pr],
                           negf_scr.at[pr * HEADS_PER_BLOCK:(pr + 1) * HEADS_PER_BLOCK, :],
                           s_scr.at[slot % ATTN_SLOTS], p_scr.at[slot % ATTN_SLOTS], c + 1, blk)
            o_ref[0, c * blk:(c + 1) * blk, lanes] = jnp.where(lane < HEAD_DIM, o[:blk], o[blk:]).astype(o_ref.dtype)
            slot += 1


def _attn_prompt(q, kt, v, logf, *, blk):
    B, T, A = q.shape
    width = ATTN_PAIRS * LANES
    ngrp = A // width
    heads = ATTN_PAIRS * HEADS_PER_BLOCK
    lf = logf.reshape(B, ngrp, heads, T)
    kern = functools.partial(_attn_prompt_kernel, blk=blk)
    grp = pl.BlockSpec((1, T, width), lambda b, p: (b, 0, p))
    grp_t = pl.BlockSpec((1, width, T), lambda b, p: (b, p, 0))
    return pl.pallas_call(
        kern,
        grid=(B, ngrp),
        in_specs=[grp, grp_t, grp, pl.BlockSpec((1, 1, heads, T), lambda b, p: (b, p, 0, 0))],
        out_specs=grp,
        out_shape=jax.ShapeDtypeStruct((B, T, A), BF16),
        scratch_shapes=[
            pltpu.VMEM((heads, T), F32),
            pltpu.VMEM((ATTN_PAIRS, T, 2 * LANES), BF16),
            pltpu.VMEM((ATTN_SLOTS, HEADS_PER_BLOCK * blk, T), F32),
            pltpu.VMEM((ATTN_SLOTS, HEADS_PER_BLOCK * blk, T), BF16),
        ],
        compiler_params=_params(("parallel", "parallel")),
        name="attn_prompt",
    )(q, kt, v, lf)


def _attn_sample_kernel(q_ref, kn_ref, vn_ref, ckt_ref, cvt_ref, fp_ref, fn_ref, o_ref):
    tt = q_ref.shape[1]
    p_len = ckt_ref.shape[3]
    width = SAMPLE_STACK * HEAD_DIM
    nt = (((1,), (1,)), ((), ()))
    lane_head = lax.broadcasted_iota(jnp.int32, (tt, width), 1) // HEAD_DIM
    keep = lax.broadcasted_iota(jnp.int32, (tt, tt), 1) <= lax.broadcasted_iota(jnp.int32, (tt, tt), 0)
    for g in range(ckt_ref.shape[1] // SAMPLE_STACK):
        heads = range(g * SAMPLE_STACK, (g + 1) * SAMPLE_STACK)
        lanes = slice(g * width, (g + 1) * width)
        q = q_ref[0][:, lanes]
        zero = jnp.zeros_like(q)
        qs = jnp.concatenate([jnp.where(lane_head == i, q, zero) for i in range(SAMPLE_STACK)], axis=0)
        kt = ckt_ref[0, heads.start:heads.stop].reshape(width, p_len).astype(BF16)
        vt = cvt_ref[0, heads.start:heads.stop].reshape(width, p_len).astype(BF16)
        s_p = jnp.dot(qs, kt, preferred_element_type=F32)
        s_n = lax.dot_general(qs, kn_ref[0][:, lanes], nt, preferred_element_type=F32)
        s_p = jnp.concatenate([s_p[i * tt:(i + 1) * tt] - fp_ref[0, 0, hd:hd + 1, :]
                               for i, hd in enumerate(heads)], axis=0)
        s_n = jnp.concatenate([jnp.where(keep, s_n[i * tt:(i + 1) * tt] - fn_ref[0, 0, hd:hd + 1, :], -jnp.inf)
                               for i, hd in enumerate(heads)], axis=0)
        m = jnp.maximum(jnp.max(s_p, axis=-1, keepdims=True), jnp.max(s_n, axis=-1, keepdims=True))
        p_p = jnp.exp2(s_p - m)
        p_n = jnp.exp2(s_n - m)
        l = jnp.sum(p_p, axis=-1, keepdims=True) + jnp.sum(p_n, axis=-1, keepdims=True)
        o = (lax.dot_general(p_p.astype(BF16), vt, nt, preferred_element_type=F32)
             + jnp.dot(p_n.astype(BF16), vn_ref[0][:, lanes], preferred_element_type=F32)) * (1.0 / l)
        out = jnp.zeros((tt, width), F32)
        for i in range(SAMPLE_STACK):
            out = jnp.where(lane_head == i, o[i * tt:(i + 1) * tt], out)
        o_ref[0, :, lanes] = out.astype(o_ref.dtype)


def _attn_sample(q, k, v, cache_kt, cache_vt, f_past, f_new):
    B, T, A = q.shape
    P = cache_kt.shape[3]
    hg = SAMPLE_HEAD_GROUP
    ngrp = N_HEADS // hg
    clf = f_past.reshape(B, ngrp, hg, P)
    lf = f_new.reshape(B, ngrp, hg, T)
    tok = lambda b, p: (b, 0, p)
    grp = lambda b, p: (b, p, 0, 0)
    return pl.pallas_call(
        _attn_sample_kernel,
        grid=(B, ngrp),
        in_specs=[
            pl.BlockSpec((1, T, hg * HEAD_DIM), tok),
            pl.BlockSpec((1, T, hg * HEAD_DIM), tok),
            pl.BlockSpec((1, T, hg * HEAD_DIM), tok),
            pl.BlockSpec((1, hg, HEAD_DIM, P), grp),
            pl.BlockSpec((1, hg, HEAD_DIM, P), grp),
            pl.BlockSpec((1, 1, hg, P), grp),
            pl.BlockSpec((1, 1, hg, T), grp),
        ],
        out_specs=pl.BlockSpec((1, T, hg * HEAD_DIM), tok),
        out_shape=jax.ShapeDtypeStruct((B, T, A), BF16),
        compiler_params=_params(("parallel", "parallel")),
        name="attn_sample",
    )(q, k, v, cache_kt, cache_vt, clf, lf)


def _mix_kernel(x_ref, o_ref, mod_ref, g_ref, w5_ref, cw_ref, pu_ref, wo_ref, y_ref, nu_ref, carry_scr):
    bb, tt, d = x_ref.shape
    tm = bb * tt
    t = pl.program_id(1)

    @pl.when(t == 0)
    def _():
        carry_scr[...] = pu_ref[...]

    x = x_ref[...]
    shift = mod_ref[:, 3:4, :]
    scale = mod_ref[:, 4:5, :]
    gate = mod_ref[:, 5:6, :]
    h = _rms_mod(x, g_ref[...], scale, shift).reshape(tm, d).astype(BF16)
    cdim = w5_ref.shape[0] // 5
    proj = lambda i: _dot_t(h, w5_ref[i * cdim:(i + 1) * cdim, :])

    u2 = proj(1) * proj(2)
    u = u2.reshape(bb, tt, cdim)
    tpos = lax.broadcasted_iota(jnp.int32, (1, tt, 1), 1)
    c0 = carry_scr[:, 0:1, :]
    c1 = carry_scr[:, 1:2, :]
    um1 = jnp.where(tpos == 0, c1, pltpu.roll(u2, 1, axis=0).reshape(bb, tt, cdim))
    um2 = jnp.where(tpos == 0, c0, jnp.where(tpos == 1, c1, pltpu.roll(u2, 2, axis=0).reshape(bb, tt, cdim)))
    conv = cw_ref[0:1, :] * um2 + cw_ref[1:2, :] * um1 + cw_ref[2:3, :] * u
    new_u = u[:, tt - (CONV_WIDTH - 1):, :]
    carry_scr[...] = new_u
    nu_ref[...] = new_u

    o_conv = proj(0) * conv.reshape(tm, cdim)
    m = _sigmoid(proj(3)) * o_ref[...].reshape(tm, cdim).astype(F32) + _sigmoid(proj(4)) * o_conv
    mo = jnp.dot(m.astype(BF16), wo_ref[...], preferred_element_type=F32)
    y_ref[...] = x + gate * mo.reshape(bb, tt, d)


def _mix(x, o_attn, mod, g, w_t, conv_w, past_u, w_out, *, bb, tt):
    B, T, D = x.shape
    C = conv_w.shape[1]
    tok = lambda b, t: (b, t, 0)
    per_b = lambda b, t: (b, 0, 0)
    return pl.pallas_call(
        _mix_kernel,
        grid=(B // bb, T // tt),
        in_specs=[
            pl.BlockSpec((bb, tt, D), tok),
            pl.BlockSpec((bb, tt, C), tok),
            pl.BlockSpec((bb, N_MOD, D), per_b),
            _resident((1, D)),
            _resident_rows(w_t, w_t.shape[0] - 5 * C, 5 * C),
            _resident(conv_w.shape),
            pl.BlockSpec((bb, CONV_WIDTH - 1, C), per_b),
            _resident(w_out.shape),
        ],
        out_specs=[
            pl.BlockSpec((bb, tt, D), tok),
            pl.BlockSpec((bb, CONV_WIDTH - 1, C), per_b),
        ],
        out_shape=[
            jax.ShapeDtypeStruct((B, T, D), F32),
            jax.ShapeDtypeStruct((B, CONV_WIDTH - 1, C), F32),
        ],
        scratch_shapes=[pltpu.VMEM((bb, CONV_WIDTH - 1, C), F32)],
        compiler_params=_params(("parallel", "arbitrary")),
        name="mix",
    )(x, o_attn, mod, g, w_t, conv_w, past_u, w_out)


def _layer(xp, xs, mod_p, mod_s, past_s, past_u_p, weights, g_final):
    (g1, w1a, w1b, gm, w_t, bf, conv_w, w_out, g2, w2a, w2b) = weights
    past_kt, past_vt, past_logf, past_u_s = past_s
    Ts = xs.shape[1]
    tt_p, bb_s = PROMPT_ROW_TILE, SAMPLE_ROW_TILE // Ts
    ffn_tiles = dict(tt_p=tt_p, bb_s=bb_s)
    xp1, xs1 = _ffn(xp, mod_p, xs, mod_s, g1, w1a, w1b, g_final, mod_base=0, final_norm=False, **ffn_tiles)

    q, kbt, vb, kp, vp, fp = _qkv(xp1, mod_p, gm, w_t, bf, None, bb=1, tt=tt_p)
    op = _attn_prompt(q, kbt, vb, fp, blk=ATTN_BLOCK)
    kp, vp = jnp.swapaxes(kp, 2, 3), jnp.swapaxes(vp, 2, 3)
    q, kb, vb, ks, vs, fs, f_past, f_new = _qkv(xs1, mod_s, gm, w_t, bf, past_logf, bb=bb_s, tt=Ts)
    os_ = _attn_sample(q, kb, vb, past_kt, past_vt, f_past, f_new)

    xp2, up = _mix(xp1, op, mod_p, gm, w_t, conv_w, past_u_p, w_out, bb=1, tt=tt_p)
    xs2, us = _mix(xs1, os_, mod_s, gm, w_t, conv_w, past_u_s, w_out, bb=bb_s, tt=Ts)
    yp, ys = _ffn(xp2, mod_p, xs2, mod_s, g2, w2a, w2b, g_final, mod_base=6, final_norm=True, **ffn_tiles)
    return (yp, kp, vp, fp, up), (ys, ks, vs, fs, us)


def kernel(x_prompt, x_sample, cache_k, cache_v, cache_logf, state_conv, c_prompt, c_sample, w_ada, b_ada, g_ffn1, w_ffn1_in, w_ffn1_out, g_mix, w_in, b_f, conv_w, w_out, g_ffn2, w_ffn2_in, w_ffn2_out, g_final):
    assert w_ada.shape[0] == 1, "single-layer encoder"
    Bp, Tp, D = x_prompt.shape
    Bs, Ts, _ = x_sample.shape
    A = N_HEADS * HEAD_DIM
    C = conv_w.shape[2]

    mod = _ada(jnp.concatenate([c_prompt, c_sample], axis=0), w_ada[0], b_ada[0])
    mod = mod.reshape(Bp + Bs, N_MOD, D)
    mod_p, mod_s = mod[:Bp], mod[Bp:]

    assert w_in.shape[2] == 3 * A + N_HEADS + 5 * C
    w_t = jnp.swapaxes(w_in[0], 0, 1).astype(BF16)
    bf = jnp.pad(b_f[0], (0, LANES - N_HEADS)).reshape(1, LANES)
    weights = (g_ffn1, w_ffn1_in[0].astype(BF16), w_ffn1_out[0].astype(BF16), g_mix, w_t, bf,
               conv_w[0], w_out[0].astype(BF16), g_ffn2, w_ffn2_in[0].astype(BF16), w_ffn2_out[0].astype(BF16))
    gfin = g_final.reshape(1, D)

    zero_u = jnp.zeros((Bp, CONV_WIDTH - 1, C), F32)
    past = (jnp.swapaxes(cache_k[0], 2, 3), jnp.swapaxes(cache_v[0], 2, 3), cache_logf[0], state_conv[0])
    (yp, kp, vp, fp, up), (ys, ks, vs, fs, us) = _layer(x_prompt, x_sample, mod_p, mod_s, past, zero_u,
                                                          weights, gfin)
    return (yp, ys, kp[None], vp[None], fp[None], up[None], ks[None], vs[None], fs[None], us[None])
```

```python
import functools

import jax
import jax.numpy as jnp
from jax import lax
from jax.experimental import pallas as pl
from jax.experimental.pallas import tpu as pltpu

F32 = jnp.float32
BF16 = jnp.bfloat16

EPS = 1e-6
N_HEADS = 16
HEAD_DIM = 64
N_MOD = 9
CONV_WIDTH = 3
LANES = 128
HEADS_PER_BLOCK = LANES // HEAD_DIM
V7X_VMEM_LIMIT_BYTES = 56 * 1024 * 1024
FFN_CHUNK = 256
PROMPT_ROW_TILE = 1024
SUB_TILES = 2
SAMPLE_ROW_TILE = 512
ATTN_BLOCK = 256
ATTN_ROWS = 64
ATTN_PAIRS = 4
ATTN_SLOTS = 4
LOG2E = 1.4426950408889634
Q_SCALE = HEAD_DIM ** -0.5 * LOG2E
SAMPLE_HEAD_GROUP = 16
SAMPLE_STACK = 4


def _params(semantics):
    return pltpu.CompilerParams(dimension_semantics=semantics, vmem_limit_bytes=V7X_VMEM_LIMIT_BYTES)


def _resident(shape):
    zeros = (0,) * len(shape)
    return pl.BlockSpec(shape, lambda *_: zeros, pipeline_mode=pl.Buffered(1))


def _resident_rows(w, row0, nrows):
    assert row0 + nrows <= w.shape[0]
    return pl.BlockSpec((pl.Element(nrows), pl.Element(w.shape[1])), lambda *_: (row0, 0),
                        pipeline_mode=pl.Buffered(1))


def _rms_mod(x, g, scale, shift):
    ms = jnp.mean(x * x, axis=-1, keepdims=True)
    y = x * lax.rsqrt(ms + EPS) * g
    return y * (1.0 + scale) + shift


def _norm_tile(x_ref, mod_ref, g_ref, shift_idx, h_scr):
    bb, tt, d = x_ref.shape
    shift = mod_ref[:, shift_idx:shift_idx + 1, :]
    scale = mod_ref[:, shift_idx + 1:shift_idx + 2, :]
    h_scr[0:bb * tt, :] = _rms_mod(x_ref[...], g_ref[...], scale, shift).reshape(bb * tt, d).astype(BF16)


def _dot_t(x, w_t):
    return lax.dot_general(x, w_t, (((1,), (1,)), ((), ())), preferred_element_type=F32)


def _sigmoid(x):
    return 1.0 / (1.0 + jnp.exp(-x))


def _log_sigmoid(x):
    return jnp.minimum(x, 0.0) - jnp.log1p(jnp.exp(-jnp.abs(x)))


def _cumsum_lanes(x):
    n = x.shape[-1]
    lane = lax.broadcasted_iota(jnp.int32, x.shape, x.ndim - 1)
    step = 1
    while step < n:
        x = x + jnp.where(lane >= step, pltpu.roll(x, step, axis=x.ndim - 1), 0.0)
        step *= 2
    return x


def _ada_kernel(c_ref, w_ref, b_ref, o_ref):
    c = c_ref[...]
    a = (c * _sigmoid(c)).astype(BF16)
    o_ref[...] = jnp.dot(a, w_ref[...].astype(BF16), preferred_element_type=F32) + b_ref[...]


def _ada(c, w_ada, b_ada):
    n, d = c.shape
    cols = w_ada.shape[1]
    tn = d
    return pl.pallas_call(
        _ada_kernel,
        grid=(cols // tn,),
        in_specs=[
            pl.BlockSpec((n, d), lambda j: (0, 0)),
            pl.BlockSpec((d, tn), lambda j: (0, j)),
            pl.BlockSpec((1, tn), lambda j: (0, j)),
        ],
        out_specs=pl.BlockSpec((n, tn), lambda j: (0, j)),
        out_shape=jax.ShapeDtypeStruct((n, cols), F32),
        compiler_params=_params(("parallel",)),
        name="ada_mod",
    )(c, w_ada, b_ada.reshape(1, cols))


def _ffn_tile(x_ref, mod_ref, g_ref, w1_ref, w2_ref, gfin_ref, o_ref, h_scr, *, mod_base, final_norm):
    bb, tt, d = x_ref.shape
    tm = bb * tt
    f = w2_ref.shape[0]
    fc = FFN_CHUNK
    gate = mod_ref[:, mod_base + 2:mod_base + 3, :]
    _norm_tile(x_ref, mod_ref, g_ref, mod_base, h_scr)

    acc = None
    for lo in range(0, f, fc):
        h = h_scr[0:tm, :]
        a = jnp.dot(h, w1_ref[:, lo:lo + fc], preferred_element_type=F32)
        b = jnp.dot(h, w1_ref[:, f + lo:f + lo + fc], preferred_element_type=F32)
        act = (a * _sigmoid(a) * b).astype(BF16)
        part = jnp.dot(act, w2_ref[lo:lo + fc, :], preferred_element_type=F32)
        acc = part if acc is None else acc + part

    y = x_ref[...] + 0.5 * gate * acc.reshape(bb, tt, d)
    if final_norm:
        ms = jnp.mean(y * y, axis=-1, keepdims=True)
        y = y * lax.rsqrt(ms + EPS) * gfin_ref[...]
    o_ref[...] = y


def _ffn_kernel(xp_ref, modp_ref, xs_ref, mods_ref, g_ref, w1_ref, w2_ref, gfin_ref, op_ref, os_ref, h_scr, *,
                n_prompt, mod_base, final_norm):
    step = pl.program_id(0)
    tile = functools.partial(_ffn_tile, mod_base=mod_base, final_norm=final_norm)

    @pl.when(step < n_prompt)
    def _():
        tile(xp_ref, modp_ref, g_ref, w1_ref, w2_ref, gfin_ref, op_ref, h_scr)

    @pl.when(step >= n_prompt)
    def _():
        tile(xs_ref, mods_ref, g_ref, w1_ref, w2_ref, gfin_ref, os_ref, h_scr)


def _ffn(xp, modp, xs, mods, g, w1c, w2c, g_final, *, tt_p, bb_s, mod_base, final_norm):
    Bp, Tp, D = xp.shape
    Bs, Ts, _ = xs.shape
    assert w2c.shape[0] % FFN_CHUNK == 0 and Tp % tt_p == 0 and Bs % bb_s == 0
    per_b = Tp // tt_p
    n_prompt = Bp * per_b
    n_sample = Bs // bb_s
    p_tile = lambda i: jnp.minimum(i, n_prompt - 1)
    s_tile = lambda i: jnp.maximum(i - n_prompt, 0)
    kern = functools.partial(_ffn_kernel, n_prompt=n_prompt, mod_base=mod_base, final_norm=final_norm)
    return pl.pallas_call(
        kern,
        grid=(n_prompt + n_sample,),
        in_specs=[
            pl.BlockSpec((1, tt_p, D), lambda i: (p_tile(i) // per_b, p_tile(i) % per_b, 0)),
            pl.BlockSpec((1, N_MOD, D), lambda i: (p_tile(i) // per_b, 0, 0)),
            pl.BlockSpec((bb_s, Ts, D), lambda i: (s_tile(i), 0, 0)),
            pl.BlockSpec((bb_s, N_MOD, D), lambda i: (s_tile(i), 0, 0)),
            _resident((1, D)),
            _resident(w1c.shape),
            _resident(w2c.shape),
            _resident((1, D)),
        ],
        out_specs=[
            pl.BlockSpec((1, tt_p, D), lambda i: (p_tile(i) // per_b, p_tile(i) % per_b, 0)),
            pl.BlockSpec((bb_s, Ts, D), lambda i: (s_tile(i), 0, 0)),
        ],
        out_shape=[jax.ShapeDtypeStruct(xp.shape, F32), jax.ShapeDtypeStruct(xs.shape, F32)],
        scratch_shapes=[pltpu.VMEM((max(tt_p, bb_s * Ts), D), BF16)],
        compiler_params=_params(("arbitrary",)),
        name="ffn_final" if final_norm else "ffn",
    )(xp, modp, xs, mods, g, w1c, w2c, g_final)


def _qkv_kernel(*refs, prompt):
    if not prompt:
        _qkv_tile(*refs, prompt=False)
        return
    (x_ref, mod_ref, g_ref, wqkv_ref, bf_ref, q_ref, kb_ref, vb_ref, k32_ref, v32_ref, lf_ref, h_scr) = refs
    rows = x_ref.shape[1] // SUB_TILES
    for s in range(SUB_TILES):
        ts = slice(s * rows, (s + 1) * rows)
        _qkv_tile(x_ref.at[:, ts, :], mod_ref, g_ref, wqkv_ref, bf_ref,
                  q_ref.at[:, ts, :], kb_ref.at[:, :, ts], vb_ref.at[:, ts, :], k32_ref.at[:, :, :, ts],
                  v32_ref.at[:, :, :, ts], lf_ref.at[:, :, ts], h_scr.at[ts, :], prompt=True)


def _qkv_tile(*refs, prompt):
    if prompt:
        (x_ref, mod_ref, g_ref, wqkv_ref, bf_ref,
         q_ref, kb_ref, vb_ref, k32_ref, v32_ref, lf_ref, h_scr) = refs
    else:
        (x_ref, mod_ref, g_ref, wqkv_ref, bf_ref, clf_ref,
         q_ref, kb_ref, vb_ref, k32_ref, v32_ref, lf_ref, fp_ref, fn_ref, h_scr) = refs
    bb, tt, d = x_ref.shape
    tm = bb * tt
    a_dim = (wqkv_ref.shape[0] - LANES) // 3
    _norm_tile(x_ref, mod_ref, g_ref, 3, h_scr)
    h = h_scr[...]

    zq = _dot_t(h, wqkv_ref[0:a_dim, :])
    q_ref[...] = (zq * Q_SCALE).astype(BF16).reshape(bb, tt, a_dim)

    for w_idx, (lo_ref, hi_ref) in ((1, (kb_ref, k32_ref)), (2, (vb_ref, v32_ref))):
        z = _dot_t(h, wqkv_ref[w_idx * a_dim:(w_idx + 1) * a_dim, :])
        if prompt:
            zt = z.T
            hi_ref[0] = zt.reshape(N_HEADS, HEAD_DIM, tt)
            if w_idx == 1:
                lo_ref[0] = zt.astype(BF16)
            else:
                lo_ref[...] = z.astype(BF16).reshape(bb, tt, a_dim)
        else:
            lo_ref[...] = z.astype(BF16).reshape(bb, tt, a_dim)
            for hd in range(N_HEADS):
                hi_ref[:, hd, :, :] = z[:, hd * HEAD_DIM:(hd + 1) * HEAD_DIM].reshape(bb, tt, HEAD_DIM)

    zf = _dot_t(h, wqkv_ref[3 * a_dim:, :])
    lf = _log_sigmoid(zf + bf_ref[...]).T
    if not prompt:
        p_len = clf_ref.shape[2]
        f_past = LOG2E * _cumsum_lanes(clf_ref[...].reshape(bb * N_HEADS, p_len))
        fp_ref[...] = f_past.reshape(bb, N_HEADS, p_len)
    for b in range(bb):
        lf_b = lf[:N_HEADS, b * tt:(b + 1) * tt]
        lf_ref[b] = lf_b
        if not prompt:
            fn_ref[b] = f_past[b * N_HEADS:(b + 1) * N_HEADS, p_len - 1:p_len] + LOG2E * _cumsum_lanes(lf_b)


def _qkv(x, mod, g, w_t, bf, cache_logf, *, bb, tt):
    B, T, D = x.shape
    A = N_HEADS * HEAD_DIM
    prompt = cache_logf is None
    tok = lambda b, t: (b, t, 0)
    per_b = lambda b, t: (b, 0, 0)
    in_specs = [
        pl.BlockSpec((bb, tt, D), tok),
        pl.BlockSpec((bb, N_MOD, D), per_b),
        _resident((1, D)),
        _resident_rows(w_t, 0, 3 * A + LANES),
        _resident(bf.shape),
    ]
    tok_spec = pl.BlockSpec((bb, tt, A), tok)
    tok_shape = jax.ShapeDtypeStruct((B, T, A), BF16)
    lf_spec = pl.BlockSpec((bb, N_HEADS, tt), lambda b, t: (b, 0, t))
    lf_shape = jax.ShapeDtypeStruct((B, N_HEADS, T), F32)
    if prompt:
        assert bb == 1
        kv_spec = pl.BlockSpec((1, N_HEADS, HEAD_DIM, tt), lambda b, t: (b, 0, 0, t))
        kv_shape = jax.ShapeDtypeStruct((B, N_HEADS, HEAD_DIM, T), F32)
        out_specs = [tok_spec, pl.BlockSpec((1, A, tt), lambda b, t: (b, 0, t)), tok_spec, kv_spec, kv_spec, lf_spec]
        out_shape = [tok_shape, jax.ShapeDtypeStruct((B, A, T), BF16), tok_shape, kv_shape, kv_shape, lf_shape]
        args = (x, mod, g, w_t, bf)
    else:
        assert T == tt
        P = cache_logf.shape[2]
        kv_spec = pl.BlockSpec((bb, N_HEADS, tt, HEAD_DIM), lambda b, t: (b, 0, t, 0))
        kv_shape = jax.ShapeDtypeStruct((B, N_HEADS, T, HEAD_DIM), F32)
        in_specs.append(pl.BlockSpec((bb, N_HEADS, P), per_b))
        out_specs = [tok_spec, tok_spec, tok_spec, kv_spec, kv_spec, lf_spec,
                     pl.BlockSpec((bb, N_HEADS, P), per_b), lf_spec]
        out_shape = [tok_shape, tok_shape, tok_shape, kv_shape, kv_shape, lf_shape,
                     jax.ShapeDtypeStruct((B, N_HEADS, P), F32), lf_shape]
        args = (x, mod, g, w_t, bf, cache_logf)
    return pl.pallas_call(
        functools.partial(_qkv_kernel, prompt=prompt),
        grid=(B // bb, T // tt),
        in_specs=in_specs,
        out_specs=out_specs,
        out_shape=out_shape,
        scratch_shapes=[pltpu.VMEM((bb * tt, D), BF16)],
        compiler_params=_params(("parallel", "parallel")),
        name="qkv_proj",
    )(*args)


def _attn_rows(q2, kt_ref, v1_ref, negf_ref, s_scr, p_scr, nk, blk):
    rows = HEADS_PER_BLOCK * blk
    half = blk // 2
    nkeys = nk * blk
    keep = lax.broadcasted_iota(jnp.int32, (blk, blk), 1) <= lax.broadcasted_iota(jnp.int32, (blk, blk), 0)
    s = jnp.dot(q2, kt_ref[0, :, 0:nkeys], preferred_element_type=F32)
    for kb in range(nk):
        ks = slice(kb * blk, (kb + 1) * blk)
        for h in range(HEADS_PER_BLOCK):
            t = s[h * blk:(h + 1) * blk, ks] + negf_ref[h:h + 1, ks]
            if kb == nk - 1:
                t = jnp.where(keep, t, -jnp.inf)
            s_scr[h * blk:(h + 1) * blk, ks] = t

    for r0 in range(0, rows, ATTN_ROWS):
        rs = slice(r0, r0 + ATTN_ROWS)
        mx = None
        for kb in range(nk):
            t = s_scr[rs, kb * blk:(kb + 1) * blk]
            t = jnp.maximum(t[:, :half], t[:, half:])
            mx = t if mx is None else jnp.maximum(mx, t)
        mb = jnp.broadcast_to(jnp.max(mx, axis=-1, keepdims=True), (ATTN_ROWS, blk))
        for kb in range(nk):
            ks = slice(kb * blk, (kb + 1) * blk)
            p_scr[rs, ks] = jnp.exp2(s_scr[rs, ks] - mb).astype(BF16)

    acc = jnp.dot(p_scr[:, 0:nkeys], v1_ref[0:nkeys, :], preferred_element_type=F32)
    return acc[:, :LANES] * (1.0 / acc[:, LANES:])


def _attn_prompt_kernel(q_ref, kt_ref, v_ref, lf_ref, o_ref, negf_scr, v1_scr, s_scr, p_scr, *, blk):
    T = v_ref.shape[1]
    negf_scr[...] = -LOG2E * _cumsum_lanes(lf_ref[0, 0])
    for pr in range(ATTN_PAIRS):
        v1_scr[pr, :, :LANES] = v_ref[0, :, pr * LANES:(pr + 1) * LANES]
        v1_scr[pr, :, LANES:] = jnp.ones((T, LANES), v1_scr.dtype)
    lane = lax.broadcasted_iota(jnp.int32, (blk, LANES), 1)
    nq = T // blk
    order = list(range(0, nq, 2)) + list(range(nq - 1 - nq % 2, 0, -2))
    slot = 0
    for c in order:
        for pr in range(ATTN_PAIRS):
            lanes = slice(pr * LANES, (pr + 1) * LANES)
            q = q_ref[0, c * blk:(c + 1) * blk, lanes]
            zero = jnp.zeros_like(q)
            q2 = jnp.concatenate([jnp.where(lane < HEAD_DIM, q, zero), jnp.where(lane >= HEAD_DIM, q, zero)],
                                 axis=0)
            o = _attn_rows(q2, kt_ref.at[:, lanes, :], v1_scr.at[pr],
                           negf_scr.at[pr * HEADS_PER_BLOCK:(pr + 1) * HEADS_PER_BLOCK, :],
                           s_scr.at[slot % ATTN_SLOTS], p_scr.at[slot % ATTN_SLOTS], c + 1, blk)
            o_ref[0, c * blk:(c + 1) * blk, lanes] = jnp.where(lane < HEAD_DIM, o[:blk], o[blk:]).astype(o_ref.dtype)
            slot += 1


def _attn_prompt(q, kt, v, logf, *, blk):
    B, T, A = q.shape
    width = ATTN_PAIRS * LANES
    ngrp = A // width
    heads = ATTN_PAIRS * HEADS_PER_BLOCK
    lf = logf.reshape(B, ngrp, heads, T)
    kern = functools.partial(_attn_prompt_kernel, blk=blk)
    grp = pl.BlockSpec((1, T, width), lambda b, p: (b, 0, p))
    grp_t = pl.BlockSpec((1, width, T), lambda b, p: (b, p, 0))
    return pl.pallas_call(
        kern,
        grid=(B, ngrp),
        in_specs=[grp, grp_t, grp, pl.BlockSpec((1, 1, heads, T), lambda b, p: (b, p, 0, 0))],
        out_specs=grp,
        out_shape=jax.ShapeDtypeStruct((B, T, A), BF16),
        scratch_shapes=[
            pltpu.VMEM((heads, T), F32),
            pltpu.VMEM((ATTN_PAIRS, T, 2 * LANES), BF16),
            pltpu.VMEM((ATTN_SLOTS, HEADS_PER_BLOCK * blk, T), F32),
            pltpu.VMEM((ATTN_SLOTS, HEADS_PER_BLOCK * blk, T), BF16),
        ],
        compiler_params=_params(("parallel", "parallel")),
        name="attn_prompt",
    )(q, kt, v, lf)


def _attn_sample_kernel(q_ref, kn_ref, vn_ref, ckt_ref, cvt_ref, fp_ref, fn_ref, o_ref):
    tt = q_ref.shape[1]
    p_len = ckt_ref.shape[3]
    width = SAMPLE_STACK * HEAD_DIM
    nt = (((1,), (1,)), ((), ()))
    lane_head = lax.broadcasted_iota(jnp.int32, (tt, width), 1) // HEAD_DIM
    keep = lax.broadcasted_iota(jnp.int32, (tt, tt), 1) <= lax.broadcasted_iota(jnp.int32, (tt, tt), 0)
    for g in range(ckt_ref.shape[1] // SAMPLE_STACK):
        heads = range(g * SAMPLE_STACK, (g + 1) * SAMPLE_STACK)
        lanes = slice(g * width, (g + 1) * width)
        q = q_ref[0][:, lanes]
        zero = jnp.zeros_like(q)
        qs = jnp.concatenate([jnp.where(lane_head == i, q, zero) for i in range(SAMPLE_STACK)], axis=0)
        kt = ckt_ref[0, heads.start:heads.stop].reshape(width, p_len).astype(BF16)
        vt = cvt_ref[0, heads.start:heads.stop].reshape(width, p_len).astype(BF16)
        s_p = jnp.dot(qs, kt, preferred_element_type=F32)
        s_n = lax.dot_general(qs, kn_ref[0][:, lanes], nt, preferred_element_type=F32)
        s_p = jnp.concatenate([s_p[i * tt:(i + 1) * tt] - fp_ref[0, 0, hd:hd + 1, :]
                               for i, hd in enumerate(heads)], axis=0)
        s_n = jnp.concatenate([jnp.where(keep, s_n[i * tt:(i + 1) * tt] - fn_ref[0, 0, hd:hd + 1, :], -jnp.inf)
                               for i, hd in enumerate(heads)], axis=0)
        m = jnp.maximum(jnp.max(s_p, axis=-1, keepdims=True), jnp.max(s_n, axis=-1, keepdims=True))
        p_p = jnp.exp2(s_p - m)
        p_n = jnp.exp2(s_n - m)
        l = jnp.sum(p_p, axis=-1, keepdims=True) + jnp.sum(p_n, axis=-1, keepdims=True)
        o = (lax.dot_general(p_p.astype(BF16), vt, nt, preferred_element_type=F32)
             + jnp.dot(p_n.astype(BF16), vn_ref[0][:, lanes], preferred_element_type=F32)) * (1.0 / l)
        out = jnp.zeros((tt, width), F32)
        for i in range(SAMPLE_STACK):
            out = jnp.where(lane_head == i, o[i * tt:(i + 1) * tt], out)
        o_ref[0, :, lanes] = out.astype(o_ref.dtype)


def _attn_sample(q, k, v, cache_kt, cache_vt, f_past, f_new):
    B, T, A = q.shape
    P = cache_kt.shape[3]
    hg = SAMPLE_HEAD_GROUP
    ngrp = N_HEADS // hg
    clf = f_past.reshape(B, ngrp, hg, P)
    lf = f_new.reshape(B, ngrp, hg, T)
    tok = lambda b, p: (b, 0, p)
    grp = lambda b, p: (b, p, 0, 0)
    return pl.pallas_call(
        _attn_sample_kernel,
        grid=(B, ngrp),
        in_specs=[
            pl.BlockSpec((1, T, hg * HEAD_DIM), tok),
            pl.BlockSpec((1, T, hg * HEAD_DIM), tok),
            pl.BlockSpec((1, T, hg * HEAD_DIM), tok),
            pl.BlockSpec((1, hg, HEAD_DIM, P), grp),
            pl.BlockSpec((1, hg, HEAD_DIM, P), grp),
            pl.BlockSpec((1, 1, hg, P), grp),
            pl.BlockSpec((1, 1, hg, T), grp),
        ],
        out_specs=pl.BlockSpec((1, T, hg * HEAD_DIM), tok),
        out_shape=jax.ShapeDtypeStruct((B, T, A), BF16),
        compiler_params=_params(("parallel", "parallel")),
        name="attn_sample",
    )(q, k, v, cache_kt, cache_vt, clf, lf)


def _mix_kernel(x_ref, o_ref, mod_ref, g_ref, w5_ref, cw_ref, pu_ref, wo_ref, y_ref, nu_ref, carry_scr, h_scr):
    bb, tt, d = x_ref.shape
    tm = bb * tt

    @pl.when(pl.program_id(1) == 0)
    def _():
        carry_scr[...] = pu_ref[...]

    gate = mod_ref[:, 5:6, :]
    _norm_tile(x_ref, mod_ref, g_ref, 3, h_scr)
    h = h_scr[...]
    cdim = w5_ref.shape[0] // 5
    proj = lambda i: _dot_t(h, w5_ref[i * cdim:(i + 1) * cdim, :])

    u2 = proj(1) * proj(2)
    u = u2.reshape(bb, tt, cdim)
    tpos = lax.broadcasted_iota(jnp.int32, (1, tt, 1), 1)
    c0 = carry_scr[:, 0:1, :]
    c1 = carry_scr[:, 1:2, :]
    um1 = jnp.where(tpos == 0, c1, pltpu.roll(u2, 1, axis=0).reshape(bb, tt, cdim))
    um2 = jnp.where(tpos == 0, c0, jnp.where(tpos == 1, c1, pltpu.roll(u2, 2, axis=0).reshape(bb, tt, cdim)))
    conv = cw_ref[0:1, :] * um2 + cw_ref[1:2, :] * um1 + cw_ref[2:3, :] * u
    new_u = u[:, tt - (CONV_WIDTH - 1):, :]
    carry_scr[...] = new_u
    nu_ref[...] = new_u

    o_conv = proj(0) * conv.reshape(tm, cdim)
    m = _sigmoid(proj(3)) * o_ref[...].reshape(tm, cdim).astype(F32) + _sigmoid(proj(4)) * o_conv
    mo = jnp.dot(m.astype(BF16), wo_ref[...], preferred_element_type=F32)
    y_ref[...] = x_ref[...] + gate * mo.reshape(bb, tt, d)


def _mix(x, o_attn, mod, g, w_t, conv_w, past_u, w_out, *, bb, tt):
    B, T, D = x.shape
    C = conv_w.shape[1]
    tok = lambda b, t: (b, t, 0)
    per_b = lambda b, t: (b, 0, 0)
    return pl.pallas_call(
        _mix_kernel,
        grid=(B // bb, T // tt),
        in_specs=[
            pl.BlockSpec((bb, tt, D), tok),
            pl.BlockSpec((bb, tt, C), tok),
            pl.BlockSpec((bb, N_MOD, D), per_b),
            _resident((1, D)),
            _resident_rows(w_t, w_t.shape[0] - 5 * C, 5 * C),
            _resident(conv_w.shape),
            pl.BlockSpec((bb, CONV_WIDTH - 1, C), per_b),
            _resident(w_out.shape),
        ],
        out_specs=[
            pl.BlockSpec((bb, tt, D), tok),
            pl.BlockSpec((bb, CONV_WIDTH - 1, C), per_b),
        ],
        out_shape=[
            jax.ShapeDtypeStruct((B, T, D), F32),
            jax.ShapeDtypeStruct((B, CONV_WIDTH - 1, C), F32),
        ],
        scratch_shapes=[pltpu.VMEM((bb, CONV_WIDTH - 1, C), F32), pltpu.VMEM((bb * tt, D), BF16)],
        compiler_params=_params(("parallel", "arbitrary")),
        name="mix",
    )(x, o_attn, mod, g, w_t, conv_w, past_u, w_out)


def _layer(xp, xs, mod_p, mod_s, past_s, past_u_p, weights, g_final):
    (g1, w1a, w1b, gm, w_t, bf, conv_w, w_out, g2, w2a, w2b) = weights
    past_kt, past_vt, past_logf, past_u_s = past_s
    Ts = xs.shape[1]
    tt_p, bb_s = PROMPT_ROW_TILE, SAMPLE_ROW_TILE // Ts
    ffn_tiles = dict(tt_p=tt_p, bb_s=bb_s)
    xp1, xs1 = _ffn(xp, mod_p, xs, mod_s, g1, w1a, w1b, g_final, mod_base=0, final_norm=False, **ffn_tiles)

    q, kbt, vb, kp, vp, fp = _qkv(xp1, mod_p, gm, w_t, bf, None, bb=1, tt=tt_p)
    op = _attn_prompt(q, kbt, vb, fp, blk=ATTN_BLOCK)
    kp, vp = jnp.swapaxes(kp, 2, 3), jnp.swapaxes(vp, 2, 3)
    q, kb, vb, ks, vs, fs, f_past, f_new = _qkv(xs1, mod_s, gm, w_t, bf, past_logf, bb=bb_s, tt=Ts)
    os_ = _attn_sample(q, kb, vb, past_kt, past_vt, f_past, f_new)

    xp2, up = _mix(xp1, op, mod_p, gm, w_t, conv_w, past_u_p, w_out, bb=1, tt=tt_p)
    xs2, us = _mix(xs1, os_, mod_s, gm, w_t, conv_w, past_u_s, w_out, bb=bb_s, tt=Ts)
    yp, ys = _ffn(xp2, mod_p, xs2, mod_s, g2, w2a, w2b, g_final, mod_base=6, final_norm=True, **ffn_tiles)
    return (yp, kp, vp, fp, up), (ys, ks, vs, fs, us)


def kernel(x_prompt, x_sample, cache_k, cache_v, cache_logf, state_conv, c_prompt, c_sample, w_ada, b_ada, g_ffn1, w_ffn1_in, w_ffn1_out, g_mix, w_in, b_f, conv_w, w_out, g_ffn2, w_ffn2_in, w_ffn2_out, g_final):
    assert w_ada.shape[0] == 1, "single-layer encoder"
    Bp, Tp, D = x_prompt.shape
    Bs, Ts, _ = x_sample.shape
    A = N_HEADS * HEAD_DIM
    C = conv_w.shape[2]

    mod = _ada(jnp.concatenate([c_prompt, c_sample], axis=0), w_ada[0], b_ada[0])
    mod = mod.reshape(Bp + Bs, N_MOD, D)
    mod_p, mod_s = mod[:Bp], mod[Bp:]

    assert w_in.shape[2] == 3 * A + N_HEADS + 5 * C
    w_t = jnp.swapaxes(w_in[0], 0, 1).astype(BF16)
    bf = jnp.pad(b_f[0], (0, LANES - N_HEADS)).reshape(1, LANES)
    weights = (g_ffn1, w_ffn1_in[0].astype(BF16), w_ffn1_out[0].astype(BF16), g_mix, w_t, bf,
               conv_w[0], w_out[0].astype(BF16), g_ffn2, w_ffn2_in[0].astype(BF16), w_ffn2_out[0].astype(BF16))
    gfin = g_final.reshape(1, D)

    zero_u = jnp.zeros((Bp, CONV_WIDTH - 1, C), F32)
    past = (jnp.swapaxes(cache_k[0], 2, 3), jnp.swapaxes(cache_v[0], 2, 3), cache_logf[0], state_conv[0])
    (yp, kp, vp, fp, up), (ys, ks, vs, fs, us) = _layer(x_prompt, x_sample, mod_p, mod_s, past, zero_u,
                                                          weights, gfin)
    return (yp, ys, kp[None], vp[None], fp[None], up[None], ks[None], vs[None], fs[None], us[None])
```

```python
import functools

import jax
import jax.numpy as jnp
from jax import lax
from jax.experimental import pallas as pl
from jax.experimental.pallas import tpu as pltpu

F32 = jnp.float32
BF16 = jnp.bfloat16

EPS = 1e-6
N_HEADS = 16
HEAD_DIM = 64
N_MOD = 9
CONV_WIDTH = 3
LANES = 128
HEADS_PER_BLOCK = LANES // HEAD_DIM
V7X_VMEM_LIMIT_BYTES = 56 * 1024 * 1024
ADA_STEPS = 3
FFN_CHUNK = 256
PROMPT_ROW_TILE = 1024
SUB_TILES = 2
SAMPLE_ROW_TILE = 512
ATTN_BLOCK = 256
ATTN_ROWS = 64
ATTN_PAIRS = 4
ATTN_SLOTS = 4
LOG2E = 1.4426950408889634
Q_SCALE = HEAD_DIM ** -0.5 * LOG2E
SAMPLE_HEAD_GROUP = 16
SAMPLE_STACK = 4


def _params(semantics):
    return pltpu.CompilerParams(dimension_semantics=semantics, vmem_limit_bytes=V7X_VMEM_LIMIT_BYTES)


def _resident(shape):
    zeros = (0,) * len(shape)
    return pl.BlockSpec(shape, lambda *_: zeros, pipeline_mode=pl.Buffered(1))


def _resident_rows(w, row0, nrows):
    assert row0 + nrows <= w.shape[0]
    return pl.BlockSpec((pl.Element(nrows), pl.Element(w.shape[1])), lambda *_: (row0, 0),
                        pipeline_mode=pl.Buffered(1))


def _rms_mod(x, g, scale, shift):
    ms = jnp.mean(x * x, axis=-1, keepdims=True)
    y = x * lax.rsqrt(ms + EPS) * g
    return y * (1.0 + scale) + shift


def _norm_tile(x_ref, mod_ref, g_ref, shift_idx, h_scr):
    bb, tt, d = x_ref.shape
    shift = mod_ref[:, shift_idx:shift_idx + 1, :]
    scale = mod_ref[:, shift_idx + 1:shift_idx + 2, :]
    h_scr[0:bb * tt, :] = _rms_mod(x_ref[...], g_ref[...], scale, shift).reshape(bb * tt, d).astype(BF16)


def _dot_t(x, w_t):
    return lax.dot_general(x, w_t, (((1,), (1,)), ((), ())), preferred_element_type=F32)


def _sigmoid(x):
    return 1.0 / (1.0 + jnp.exp(-x))


def _log_sigmoid(x):
    return jnp.minimum(x, 0.0) - jnp.log1p(jnp.exp(-jnp.abs(x)))


def _cumsum_lanes(x):
    n = x.shape[-1]
    lane = lax.broadcasted_iota(jnp.int32, x.shape, x.ndim - 1)
    step = 1
    while step < n:
        x = x + jnp.where(lane >= step, pltpu.roll(x, step, axis=x.ndim - 1), 0.0)
        step *= 2
    return x


def _ada_kernel(c_ref, w_ref, b_ref, o_ref):
    c = c_ref[...]
    a = (c * _sigmoid(c)).astype(BF16)
    o_ref[...] = jnp.dot(a, w_ref[...].astype(BF16), preferred_element_type=F32) + b_ref[...]


def _ada(c, w_ada, b_ada):
    n, d = c.shape
    cols = w_ada.shape[1]
    tn = cols // ADA_STEPS
    assert cols % ADA_STEPS == 0 and tn % LANES == 0
    return pl.pallas_call(
        _ada_kernel,
        grid=(cols // tn,),
        in_specs=[
            pl.BlockSpec((n, d), lambda j: (0, 0)),
            pl.BlockSpec((d, tn), lambda j: (0, j)),
            pl.BlockSpec((1, tn), lambda j: (0, j)),
        ],
        out_specs=pl.BlockSpec((n, tn), lambda j: (0, j)),
        out_shape=jax.ShapeDtypeStruct((n, cols), F32),
        compiler_params=_params(("parallel",)),
        name="ada_mod",
    )(c, w_ada, b_ada.reshape(1, cols))


def _ffn_tile(x_ref, mod_ref, g_ref, w1_ref, w2_ref, gfin_ref, o_ref, h_scr, *, mod_base, final_norm):
    bb, tt, d = x_ref.shape
    tm = bb * tt
    f = w2_ref.shape[0]
    fc = FFN_CHUNK
    gate = mod_ref[:, mod_base + 2:mod_base + 3, :]
    _norm_tile(x_ref, mod_ref, g_ref, mod_base, h_scr)

    acc = None
    for lo in range(0, f, fc):
        h = h_scr[0:tm, :]
        a = jnp.dot(h, w1_ref[:, lo:lo + fc], preferred_element_type=F32)
        b = jnp.dot(h, w1_ref[:, f + lo:f + lo + fc], preferred_element_type=F32)
        act = (a * _sigmoid(a) * b).astype(BF16)
        part = jnp.dot(act, w2_ref[lo:lo + fc, :], preferred_element_type=F32)
        acc = part if acc is None else acc + part

    y = x_ref[...] + 0.5 * gate * acc.reshape(bb, tt, d)
    if final_norm:
        ms = jnp.mean(y * y, axis=-1, keepdims=True)
        y = y * lax.rsqrt(ms + EPS) * gfin_ref[...]
    o_ref[...] = y


def _ffn_kernel(xp_ref, modp_ref, xs_ref, mods_ref, g_ref, w1_ref, w2_ref, gfin_ref, op_ref, os_ref, h_scr, *,
                n_prompt, mod_base, final_norm):
    step = pl.program_id(0)
    tile = functools.partial(_ffn_tile, mod_base=mod_base, final_norm=final_norm)

    @pl.when(step < n_prompt)
    def _():
        tile(xp_ref, modp_ref, g_ref, w1_ref, w2_ref, gfin_ref, op_ref, h_scr)

    @pl.when(step >= n_prompt)
    def _():
        tile(xs_ref, mods_ref, g_ref, w1_ref, w2_ref, gfin_ref, os_ref, h_scr)


def _ffn(xp, modp, xs, mods, g, w1c, w2c, g_final, *, tt_p, bb_s, mod_base, final_norm):
    Bp, Tp, D = xp.shape
    Bs, Ts, _ = xs.shape
    assert w2c.shape[0] % FFN_CHUNK == 0 and Tp % tt_p == 0 and Bs % bb_s == 0
    per_b = Tp // tt_p
    n_prompt = Bp * per_b
    n_sample = Bs // bb_s
    p_tile = lambda i: jnp.minimum(i, n_prompt - 1)
    s_tile = lambda i: jnp.maximum(i - n_prompt, 0)
    kern = functools.partial(_ffn_kernel, n_prompt=n_prompt, mod_base=mod_base, final_norm=final_norm)
    return pl.pallas_call(
        kern,
        grid=(n_prompt + n_sample,),
        in_specs=[
            pl.BlockSpec((1, tt_p, D), lambda i: (p_tile(i) // per_b, p_tile(i) % per_b, 0)),
            pl.BlockSpec((1, N_MOD, D), lambda i: (p_tile(i) // per_b, 0, 0)),
            pl.BlockSpec((bb_s, Ts, D), lambda i: (s_tile(i), 0, 0)),
            pl.BlockSpec((bb_s, N_MOD, D), lambda i: (s_tile(i), 0, 0)),
            _resident((1, D)),
            _resident(w1c.shape),
            _resident(w2c.shape),
            _resident((1, D)),
        ],
        out_specs=[
            pl.BlockSpec((1, tt_p, D), lambda i: (p_tile(i) // per_b, p_tile(i) % per_b, 0)),
            pl.BlockSpec((bb_s, Ts, D), lambda i: (s_tile(i), 0, 0)),
        ],
        out_shape=[jax.ShapeDtypeStruct(xp.shape, F32), jax.ShapeDtypeStruct(xs.shape, F32)],
        scratch_shapes=[pltpu.VMEM((max(tt_p, bb_s * Ts), D), BF16)],
        compiler_params=_params(("arbitrary",)),
        name="ffn_final" if final_norm else "ffn",
    )(xp, modp, xs, mods, g, w1c, w2c, g_final)


def _qkv_kernel(*refs, prompt):
    if not prompt:
        _qkv_tile(*refs, prompt=False)
        return
    (x_ref, mod_ref, g_ref, wqkv_ref, bf_ref, q_ref, kb_ref, vb_ref, k32_ref, v32_ref, lf_ref, h_scr) = refs
    rows = x_ref.shape[1] // SUB_TILES
    for s in range(SUB_TILES):
        ts = slice(s * rows, (s + 1) * rows)
        _qkv_tile(x_ref.at[:, ts, :], mod_ref, g_ref, wqkv_ref, bf_ref,
                  q_ref.at[:, ts, :], kb_ref.at[:, :, ts], vb_ref.at[:, ts, :], k32_ref.at[:, :, :, ts],
                  v32_ref.at[:, :, :, ts], lf_ref.at[:, :, ts], h_scr.at[ts, :], prompt=True)


def _qkv_tile(*refs, prompt):
    if prompt:
        (x_ref, mod_ref, g_ref, wqkv_ref, bf_ref,
         q_ref, kb_ref, vb_ref, k32_ref, v32_ref, lf_ref, h_scr) = refs
    else:
        (x_ref, mod_ref, g_ref, wqkv_ref, bf_ref, clf_ref,
         q_ref, kb_ref, vb_ref, k32_ref, v32_ref, lf_ref, fp_ref, fn_ref, h_scr) = refs
    bb, tt, d = x_ref.shape
    tm = bb * tt
    a_dim = (wqkv_ref.shape[0] - LANES) // 3
    _norm_tile(x_ref, mod_ref, g_ref, 3, h_scr)
    h = h_scr[...]

    zq = _dot_t(h, wqkv_ref[0:a_dim, :])
    q_ref[...] = (zq * Q_SCALE).astype(BF16).reshape(bb, tt, a_dim)

    for w_idx, (lo_ref, hi_ref) in ((1, (kb_ref, k32_ref)), (2, (vb_ref, v32_ref))):
        z = _dot_t(h, wqkv_ref[w_idx * a_dim:(w_idx + 1) * a_dim, :])
        if prompt:
            zt = z.T
            hi_ref[0] = zt.reshape(N_HEADS, HEAD_DIM, tt)
            if w_idx == 1:
                lo_ref[0] = zt.astype(BF16)
            else:
                lo_ref[...] = z.astype(BF16).reshape(bb, tt, a_dim)
        else:
            lo_ref[...] = z.astype(BF16).reshape(bb, tt, a_dim)
            for hd in range(N_HEADS):
                hi_ref[:, hd, :, :] = z[:, hd * HEAD_DIM:(hd + 1) * HEAD_DIM].reshape(bb, tt, HEAD_DIM)

    zf = _dot_t(h, wqkv_ref[3 * a_dim:, :])
    lf = _log_sigmoid(zf + bf_ref[...]).T
    if not prompt:
        p_len = clf_ref.shape[2]
        f_past = LOG2E * _cumsum_lanes(clf_ref[...].reshape(bb * N_HEADS, p_len))
        fp_ref[...] = f_past.reshape(bb, N_HEADS, p_len)
    for b in range(bb):
        lf_b = lf[:N_HEADS, b * tt:(b + 1) * tt]
        lf_ref[b] = lf_b
        if not prompt:
            fn_ref[b] = f_past[b * N_HEADS:(b + 1) * N_HEADS, p_len - 1:p_len] + LOG2E * _cumsum_lanes(lf_b)


def _qkv(x, mod, g, w_t, bf, cache_logf, *, bb, tt):
    B, T, D = x.shape
    A = N_HEADS * HEAD_DIM
    prompt = cache_logf is None
    tok = lambda b, t: (b, t, 0)
    per_b = lambda b, t: (b, 0, 0)
    in_specs = [
        pl.BlockSpec((bb, tt, D), tok),
        pl.BlockSpec((bb, N_MOD, D), per_b),
        _resident((1, D)),
        _resident_rows(w_t, 0, 3 * A + LANES),
        _resident(bf.shape),
    ]
    tok_spec = pl.BlockSpec((bb, tt, A), tok)
    tok_shape = jax.ShapeDtypeStruct((B, T, A), BF16)
    lf_spec = pl.BlockSpec((bb, N_HEADS, tt), lambda b, t: (b, 0, t))
    lf_shape = jax.ShapeDtypeStruct((B, N_HEADS, T), F32)
    if prompt:
        assert bb == 1
        kv_spec = pl.BlockSpec((1, N_HEADS, HEAD_DIM, tt), lambda b, t: (b, 0, 0, t))
        kv_shape = jax.ShapeDtypeStruct((B, N_HEADS, HEAD_DIM, T), F32)
        out_specs = [tok_spec, pl.BlockSpec((1, A, tt), lambda b, t: (b, 0, t)), tok_spec, kv_spec, kv_spec, lf_spec]
        out_shape = [tok_shape, jax.ShapeDtypeStruct((B, A, T), BF16), tok_shape, kv_shape, kv_shape, lf_shape]
        args = (x, mod, g, w_t, bf)
    else:
        assert T == tt
        P = cache_logf.shape[2]
        kv_spec = pl.BlockSpec((bb, N_HEADS, tt, HEAD_DIM), lambda b, t: (b, 0, t, 0))
        kv_shape = jax.ShapeDtypeStruct((B, N_HEADS, T, HEAD_DIM), F32)
        in_specs.append(pl.BlockSpec((bb, N_HEADS, P), per_b))
        out_specs = [tok_spec, tok_spec, tok_spec, kv_spec, kv_spec, lf_spec,
                     pl.BlockSpec((bb, N_HEADS, P), per_b), lf_spec]
        out_shape = [tok_shape, tok_shape, tok_shape, kv_shape, kv_shape, lf_shape,
                     jax.ShapeDtypeStruct((B, N_HEADS, P), F32), lf_shape]
        args = (x, mod, g, w_t, bf, cache_logf)
    return pl.pallas_call(
        functools.partial(_qkv_kernel, prompt=prompt),
        grid=(B // bb, T // tt),
        in_specs=in_specs,
        out_specs=out_specs,
        out_shape=out_shape,
        scratch_shapes=[pltpu.VMEM((bb * tt, D), BF16)],
        compiler_params=_params(("parallel", "parallel")),
        name="qkv_proj",
    )(*args)


def _attn_rows(q2, kt_ref, v1_ref, negf_ref, s_scr, p_scr, nk, blk):
    rows = HEADS_PER_BLOCK * blk
    half = blk // 2
    nkeys = nk * blk
    keep = lax.broadcasted_iota(jnp.int32, (blk, blk), 1) <= lax.broadcasted_iota(jnp.int32, (blk, blk), 0)
    s = jnp.dot(q2, kt_ref[0, :, 0:nkeys], preferred_element_type=F32)
    for kb in range(nk):
        ks = slice(kb * blk, (kb + 1) * blk)
        for h in range(HEADS_PER_BLOCK):
            t = s[h * blk:(h + 1) * blk, ks] + negf_ref[h:h + 1, ks]
            if kb == nk - 1:
                t = jnp.where(keep, t, -jnp.inf)
            s_scr[h * blk:(h + 1) * blk, ks] = t

    for r0 in range(0, rows, ATTN_ROWS):
        rs = slice(r0, r0 + ATTN_ROWS)
        mx = None
        for kb in range(nk):
            t = s_scr[rs, kb * blk:(kb + 1) * blk]
            t = jnp.maximum(t[:, :half], t[:, half:])
            mx = t if mx is None else jnp.maximum(mx, t)
        mb = jnp.broadcast_to(jnp.max(mx, axis=-1, keepdims=True), (ATTN_ROWS, blk))
        for kb in range(nk):
            ks = slice(kb * blk, (kb + 1) * blk)
            p_scr[rs, ks] = jnp.exp2(s_scr[rs, ks] - mb).astype(BF16)

    acc = jnp.dot(p_scr[:, 0:nkeys], v1_ref[0:nkeys, :], preferred_element_type=F32)
    return acc[:, :LANES] * (1.0 / acc[:, LANES:])


def _attn_prompt_kernel(q_ref, kt_ref, v_ref, lf_ref, o_ref, negf_scr, v1_scr, s_scr, p_scr, *, blk):
    T = v_ref.shape[1]
    negf_scr[...] = -LOG2E * _cumsum_lanes(lf_ref[0, 0])
    for pr in range(ATTN_PAIRS):
        v1_scr[pr, :, :LANES] = v_ref[0, :, pr * LANES:(pr + 1) * LANES]
        v1_scr[pr, :, LANES:] = jnp.ones((T, LANES), v1_scr.dtype)
    lane = lax.broadcasted_iota(jnp.int32, (blk, LANES), 1)
    nq = T // blk
    order = list(range(0, nq, 2)) + list(range(nq - 1 - nq % 2, 0, -2))
    slot = 0
    for c in order:
        for pr in range(ATTN_PAIRS):
            lanes = slice(pr * LANES, (pr + 1) * LANES)
            q = q_ref[0, c * blk:(c + 1) * blk, lanes]
            zero = jnp.zeros_like(q)
            q2 = jnp.concatenate([jnp.where(lane < HEAD_DIM, q, zero), jnp.where(lane >= HEAD_DIM, q, zero)],
                                 axis=0)
            o = _attn_rows(q2, kt_ref.at[:, lanes, :], v1_scr.at[pr],
                           negf_scr.at[pr * HEADS_PER_BLOCK:(pr + 1) * HEADS_PER_BLOCK, :],
                           s_scr.at[slot % ATTN_SLOTS], p_scr.at[slot % ATTN_SLOTS], c + 1, blk)
            o_ref[0, c * blk:(c + 1) * blk, lanes] = jnp.where(lane < HEAD_DIM, o[:blk], o[blk:]).astype(o_ref.dtype)
            slot += 1


def _attn_prompt(q, kt, v, logf, *, blk):
    B, T, A = q.shape
    width = ATTN_PAIRS * LANES
    ngrp = A // width
    heads = ATTN_PAIRS * HEADS_PER_BLOCK
    lf = logf.reshape(B, ngrp, heads, T)
    kern = functools.partial(_attn_prompt_kernel, blk=blk)
    grp = pl.BlockSpec((1, T, width), lambda b, p: (b, 0, p))
    grp_t = pl.BlockSpec((1, width, T), lambda b, p: (b, p, 0))
    return pl.pallas_call(
        kern,
        grid=(B, ngrp),
        in_specs=[grp, grp_t, grp, pl.BlockSpec((1, 1, heads, T), lambda b, p: (b, p, 0, 0))],
        out_specs=grp,
        out_shape=jax.ShapeDtypeStruct((B, T, A), BF16),
        scratch_shapes=[
            pltpu.VMEM((heads, T), F32),
            pltpu.VMEM((ATTN_PAIRS, T, 2 * LANES), BF16),
            pltpu.VMEM((ATTN_SLOTS, HEADS_PER_BLOCK * blk, T), F32),
            pltpu.VMEM((ATTN_SLOTS, HEADS_PER_BLOCK * blk, T), BF16),
        ],
        compiler_params=_params(("parallel", "parallel")),
        name="attn_prompt",
    )(q, kt, v, lf)


def _attn_sample_kernel(q_ref, kn_ref, vn_ref, ckt_ref, cvt_ref, fp_ref, fn_ref, o_ref):
    tt = q_ref.shape[1]
    p_len = ckt_ref.shape[3]
    width = SAMPLE_STACK * HEAD_DIM
    nt = (((1,), (1,)), ((), ()))
    lane_head = lax.broadcasted_iota(jnp.int32, (tt, width), 1) // HEAD_DIM
    keep = lax.broadcasted_iota(jnp.int32, (tt, tt), 1) <= lax.broadcasted_iota(jnp.int32, (tt, tt), 0)
    for g in range(ckt_ref.shape[1] // SAMPLE_STACK):
        heads = range(g * SAMPLE_STACK, (g + 1) * SAMPLE_STACK)
        lanes = slice(g * width, (g + 1) * width)
        q = q_ref[0][:, lanes]
        zero = jnp.zeros_like(q)
        qs = jnp.concatenate([jnp.where(lane_head == i, q, zero) for i in range(SAMPLE_STACK)], axis=0)
        kt = ckt_ref[0, heads.start:heads.stop].reshape(width, p_len).astype(BF16)
        vt = cvt_ref[0, heads.start:heads.stop].reshape(width, p_len).astype(BF16)
        s_p = jnp.dot(qs, kt, preferred_element_type=F32)
        s_n = lax.dot_general(qs, kn_ref[0][:, lanes], nt, preferred_element_type=F32)
        s_p = jnp.concatenate([s_p[i * tt:(i + 1) * tt] - fp_ref[0, 0, hd:hd + 1, :]
                               for i, hd in enumerate(heads)], axis=0)
        s_n = jnp.concatenate([jnp.where(keep, s_n[i * tt:(i + 1) * tt] - fn_ref[0, 0, hd:hd + 1, :], -jnp.inf)
                               for i, hd in enumerate(heads)], axis=0)
        m = jnp.maximum(jnp.max(s_p, axis=-1, keepdims=True), jnp.max(s_n, axis=-1, keepdims=True))
        p_p = jnp.exp2(s_p - m)
        p_n = jnp.exp2(s_n - m)
        l = jnp.sum(p_p, axis=-1, keepdims=True) + jnp.sum(p_n, axis=-1, keepdims=True)
        o = (lax.dot_general(p_p.astype(BF16), vt, nt, preferred_element_type=F32)
             + jnp.dot(p_n.astype(BF16), vn_ref[0][:, lanes], preferred_element_type=F32)) * (1.0 / l)
        out = jnp.zeros((tt, width), F32)
        for i in range(SAMPLE_STACK):
            out = jnp.where(lane_head == i, o[i * tt:(i + 1) * tt], out)
        o_ref[0, :, lanes] = out.astype(o_ref.dtype)


def _attn_sample(q, k, v, cache_kt, cache_vt, f_past, f_new):
    B, T, A = q.shape
    P = cache_kt.shape[3]
    hg = SAMPLE_HEAD_GROUP
    ngrp = N_HEADS // hg
    clf = f_past.reshape(B, ngrp, hg, P)
    lf = f_new.reshape(B, ngrp, hg, T)
    tok = lambda b, p: (b, 0, p)
    grp = lambda b, p: (b, p, 0, 0)
    return pl.pallas_call(
        _attn_sample_kernel,
        grid=(B, ngrp),
        in_specs=[
            pl.BlockSpec((1, T, hg * HEAD_DIM), tok),
            pl.BlockSpec((1, T, hg * HEAD_DIM), tok),
            pl.BlockSpec((1, T, hg * HEAD_DIM), tok),
            pl.BlockSpec((1, hg, HEAD_DIM, P), grp),
            pl.BlockSpec((1, hg, HEAD_DIM, P), grp),
            pl.BlockSpec((1, 1, hg, P), grp),
            pl.BlockSpec((1, 1, hg, T), grp),
        ],
        out_specs=pl.BlockSpec((1, T, hg * HEAD_DIM), tok),
        out_shape=jax.ShapeDtypeStruct((B, T, A), BF16),
        compiler_params=_params(("parallel", "parallel")),
        name="attn_sample",
    )(q, k, v, cache_kt, cache_vt, clf, lf)


def _mix_kernel(x_ref, o_ref, mod_ref, g_ref, w5_ref, cw_ref, pu_ref, wo_ref, y_ref, nu_ref, carry_scr, h_scr):
    bb, tt, d = x_ref.shape
    tm = bb * tt

    @pl.when(pl.program_id(1) == 0)
    def _():
        carry_scr[...] = pu_ref[...]

    gate = mod_ref[:, 5:6, :]
    _norm_tile(x_ref, mod_ref, g_ref, 3, h_scr)
    h = h_scr[...]
    cdim = w5_ref.shape[0] // 5
    proj = lambda i: _dot_t(h, w5_ref[i * cdim:(i + 1) * cdim, :])

    u2 = proj(1) * proj(2)
    u = u2.reshape(bb, tt, cdim)
    tpos = lax.broadcasted_iota(jnp.int32, (1, tt, 1), 1)
    c0 = carry_scr[:, 0:1, :]
    c1 = carry_scr[:, 1:2, :]
    um1 = jnp.where(tpos == 0, c1, pltpu.roll(u2, 1, axis=0).reshape(bb, tt, cdim))
    um2 = jnp.where(tpos == 0, c0, jnp.where(tpos == 1, c1, pltpu.roll(u2, 2, axis=0).reshape(bb, tt, cdim)))
    conv = cw_ref[0:1, :] * um2 + cw_ref[1:2, :] * um1 + cw_ref[2:3, :] * u
    new_u = u[:, tt - (CONV_WIDTH - 1):, :]
    carry_scr[...] = new_u
    nu_ref[...] = new_u

    o_conv = proj(0) * conv.reshape(tm, cdim)
    m = _sigmoid(proj(3)) * o_ref[...].reshape(tm, cdim).astype(F32) + _sigmoid(proj(4)) * o_conv
    mo = jnp.dot(m.astype(BF16), wo_ref[...], preferred_element_type=F32)
    y_ref[...] = x_ref[...] + gate * mo.reshape(bb, tt, d)


def _mix(x, o_attn, mod, g, w_t, conv_w, past_u, w_out, *, bb, tt):
    B, T, D = x.shape
    C = conv_w.shape[1]
    tok = lambda b, t: (b, t, 0)
    per_b = lambda b, t: (b, 0, 0)
    return pl.pallas_call(
        _mix_kernel,
        grid=(B // bb, T // tt),
        in_specs=[
            pl.BlockSpec((bb, tt, D), tok),
            pl.BlockSpec((bb, tt, C), tok),
            pl.BlockSpec((bb, N_MOD, D), per_b),
            _resident((1, D)),
            _resident_rows(w_t, w_t.shape[0] - 5 * C, 5 * C),
            _resident(conv_w.shape),
            pl.BlockSpec((bb, CONV_WIDTH - 1, C), per_b),
            _resident(w_out.shape),
        ],
        out_specs=[
            pl.BlockSpec((bb, tt, D), tok),
            pl.BlockSpec((bb, CONV_WIDTH - 1, C), per_b),
        ],
        out_shape=[
            jax.ShapeDtypeStruct((B, T, D), F32),
            jax.ShapeDtypeStruct((B, CONV_WIDTH - 1, C), F32),
        ],
        scratch_shapes=[pltpu.VMEM((bb, CONV_WIDTH - 1, C), F32), pltpu.VMEM((bb * tt, D), BF16)],
        compiler_params=_params(("parallel", "arbitrary")),
        name="mix",
    )(x, o_attn, mod, g, w_t, conv_w, past_u, w_out)


def _layer(xp, xs, mod_p, mod_s, past_s, past_u_p, weights, g_final):
    (g1, w1a, w1b, gm, w_t, bf, conv_w, w_out, g2, w2a, w2b) = weights
    past_kt, past_vt, past_logf, past_u_s = past_s
    Ts = xs.shape[1]
    tt_p, bb_s = PROMPT_ROW_TILE, SAMPLE_ROW_TILE // Ts
    ffn_tiles = dict(tt_p=tt_p, bb_s=bb_s)
    xp1, xs1 = _ffn(xp, mod_p, xs, mod_s, g1, w1a, w1b, g_final, mod_base=0, final_norm=False, **ffn_tiles)

    q, kbt, vb, kp, vp, fp = _qkv(xp1, mod_p, gm, w_t, bf, None, bb=1, tt=tt_p)
    op = _attn_prompt(q, kbt, vb, fp, blk=ATTN_BLOCK)
    kp, vp = jnp.swapaxes(kp, 2, 3), jnp.swapaxes(vp, 2, 3)
    q, kb, vb, ks, vs, fs, f_past, f_new = _qkv(xs1, mod_s, gm, w_t, bf, past_logf, bb=bb_s, tt=Ts)
    os_ = _attn_sample(q, kb, vb, past_kt, past_vt, f_past, f_new)

    xp2, up = _mix(xp1, op, mod_p, gm, w_t, conv_w, past_u_p, w_out, bb=1, tt=tt_p)
    xs2, us = _mix(xs1, os_, mod_s, gm, w_t, conv_w, past_u_s, w_out, bb=bb_s, tt=Ts)
    yp, ys = _ffn(xp2, mod_p, xs2, mod_s, g2, w2a, w2b, g_final, mod_base=6, final_norm=True, **ffn_tiles)
    return (yp, kp, vp, fp, up), (ys, ks, vs, fs, us)


def kernel(x_prompt, x_sample, cache_k, cache_v, cache_logf, state_conv, c_prompt, c_sample, w_ada, b_ada, g_ffn1, w_ffn1_in, w_ffn1_out, g_mix, w_in, b_f, conv_w, w_out, g_ffn2, w_ffn2_in, w_ffn2_out, g_final):
    assert w_ada.shape[0] == 1, "single-layer encoder"
    Bp, Tp, D = x_prompt.shape
    Bs, Ts, _ = x_sample.shape
    A = N_HEADS * HEAD_DIM
    C = conv_w.shape[2]

    mod = _ada(jnp.concatenate([c_prompt, c_sample], axis=0), w_ada[0], b_ada[0])
    mod = mod.reshape(Bp + Bs, N_MOD, D)
    mod_p, mod_s = mod[:Bp], mod[Bp:]

    assert w_in.shape[2] == 3 * A + N_HEADS + 5 * C
    w_t = jnp.swapaxes(w_in[0], 0, 1).astype(BF16)
    bf = jnp.pad(b_f[0], (0, LANES - N_HEADS)).reshape(1, LANES)
    weights = (g_ffn1, w_ffn1_in[0].astype(BF16), w_ffn1_out[0].astype(BF16), g_mix, w_t, bf,
               conv_w[0], w_out[0].astype(BF16), g_ffn2, w_ffn2_in[0].astype(BF16), w_ffn2_out[0].astype(BF16))
    gfin = g_final.reshape(1, D)

    zero_u = jnp.zeros((Bp, CONV_WIDTH - 1, C), F32)
    past = (jnp.swapaxes(cache_k[0], 2, 3), jnp.swapaxes(cache_v[0], 2, 3), cache_logf[0], state_conv[0])
    (yp, kp, vp, fp, up), (ys, ks, vs, fs, us) = _layer(x_prompt, x_sample, mod_p, mod_s, past, zero_u,
                                                          weights, gfin)
    return (yp, ys, kp[None], vp[None], fp[None], up[None], ks[None], vs[None], fs[None], us[None])
```

```python
import functools

import jax
import jax.numpy as jnp
from jax import lax
from jax.experimental import pallas as pl
from jax.experimental.pallas import tpu as pltpu

F32 = jnp.float32
BF16 = jnp.bfloat16

EPS = 1e-6
N_HEADS = 16
HEAD_DIM = 64
N_MOD = 9
MOD_FFN1, MOD_MIX, MOD_FFN2 = 0, 3, 6
CONV_WIDTH = 3
LANES = 128
HEADS_PER_BLOCK = LANES // HEAD_DIM
V7X_VMEM_LIMIT_BYTES = 56 * 1024 * 1024
ADA_STEPS = 3
FFN_CHUNK = 256
PROMPT_ROW_TILE = 1024
SUB_TILES = 2
SAMPLE_ROW_TILE = 512
ATTN_BLOCK = 256
ATTN_ROWS = 64
ATTN_PAIRS = 4
ATTN_SLOTS = 4
LOG2E = 1.4426950408889634
Q_SCALE = HEAD_DIM ** -0.5 * LOG2E
SAMPLE_HEAD_GROUP = 16
SAMPLE_STACK = 4


def _params(semantics):
    return pltpu.CompilerParams(dimension_semantics=semantics, vmem_limit_bytes=V7X_VMEM_LIMIT_BYTES)


def _resident(shape):
    zeros = (0,) * len(shape)
    return pl.BlockSpec(shape, lambda *_: zeros, pipeline_mode=pl.Buffered(1))


def _resident_rows(w, row0, nrows):
    assert row0 + nrows <= w.shape[0]
    return pl.BlockSpec((pl.Element(nrows), pl.Element(w.shape[1])), lambda *_: (row0, 0),
                        pipeline_mode=pl.Buffered(1))


def _rms_mod(x, g, scale, shift):
    ms = jnp.mean(x * x, axis=-1, keepdims=True)
    y = x * lax.rsqrt(ms + EPS) * g
    return y * (1.0 + scale) + shift


def _norm_tile(x_ref, mod_ref, g_ref, shift_idx, h_scr):
    bb, tt, d = x_ref.shape
    shift = mod_ref[:, shift_idx:shift_idx + 1, :]
    scale = mod_ref[:, shift_idx + 1:shift_idx + 2, :]
    h_scr[0:bb * tt, :] = _rms_mod(x_ref[...], g_ref[...], scale, shift).reshape(bb * tt, d).astype(BF16)


def _dot_t(x, w_t):
    return lax.dot_general(x, w_t, (((1,), (1,)), ((), ())), preferred_element_type=F32)


def _sigmoid(x):
    return 1.0 / (1.0 + jnp.exp(-x))


def _log_sigmoid(x):
    return jnp.minimum(x, 0.0) - jnp.log1p(jnp.exp(-jnp.abs(x)))


def _cumsum_lanes(x):
    n = x.shape[-1]
    lane = lax.broadcasted_iota(jnp.int32, x.shape, x.ndim - 1)
    step = 1
    while step < n:
        x = x + jnp.where(lane >= step, pltpu.roll(x, step, axis=x.ndim - 1), 0.0)
        step *= 2
    return x


def _ada_kernel(c_ref, w_ref, b_ref, o_ref):
    c = c_ref[...]
    a = (c * _sigmoid(c)).astype(BF16)
    o_ref[...] = jnp.dot(a, w_ref[...].astype(BF16), preferred_element_type=F32) + b_ref[...]


def _ada(c, w_ada, b_ada):
    n, d = c.shape
    cols = w_ada.shape[1]
    tn = cols // ADA_STEPS
    assert cols % ADA_STEPS == 0 and tn % LANES == 0
    return pl.pallas_call(
        _ada_kernel,
        grid=(cols // tn,),
        in_specs=[
            pl.BlockSpec((n, d), lambda j: (0, 0)),
            pl.BlockSpec((d, tn), lambda j: (0, j)),
            pl.BlockSpec((1, tn), lambda j: (0, j)),
        ],
        out_specs=pl.BlockSpec((n, tn), lambda j: (0, j)),
        out_shape=jax.ShapeDtypeStruct((n, cols), F32),
        compiler_params=_params(("parallel",)),
        name="ada_mod",
    )(c, w_ada, b_ada.reshape(1, cols))


def _ffn_tile(x_ref, mod_ref, g_ref, w1_ref, w2_ref, gfin_ref, o_ref, h_scr, *, mod_base, final_norm):
    bb, tt, d = x_ref.shape
    tm = bb * tt
    f = w2_ref.shape[0]
    fc = FFN_CHUNK
    gate = mod_ref[:, mod_base + 2:mod_base + 3, :]
    _norm_tile(x_ref, mod_ref, g_ref, mod_base, h_scr)

    acc = None
    for lo in range(0, f, fc):
        h = h_scr[0:tm, :]
        a = jnp.dot(h, w1_ref[:, lo:lo + fc], preferred_element_type=F32)
        b = jnp.dot(h, w1_ref[:, f + lo:f + lo + fc], preferred_element_type=F32)
        act = (a * _sigmoid(a) * b).astype(BF16)
        part = jnp.dot(act, w2_ref[lo:lo + fc, :], preferred_element_type=F32)
        acc = part if acc is None else acc + part

    y = x_ref[...] + 0.5 * gate * acc.reshape(bb, tt, d)
    if final_norm:
        ms = jnp.mean(y * y, axis=-1, keepdims=True)
        y = y * lax.rsqrt(ms + EPS) * gfin_ref[...]
    o_ref[...] = y


def _ffn_kernel(xp_ref, modp_ref, xs_ref, mods_ref, g_ref, w1_ref, w2_ref, gfin_ref, op_ref, os_ref, h_scr, *,
                n_prompt, mod_base, final_norm):
    step = pl.program_id(0)
    tile = functools.partial(_ffn_tile, mod_base=mod_base, final_norm=final_norm)

    @pl.when(step < n_prompt)
    def _():
        tile(xp_ref, modp_ref, g_ref, w1_ref, w2_ref, gfin_ref, op_ref, h_scr)

    @pl.when(step >= n_prompt)
    def _():
        tile(xs_ref, mods_ref, g_ref, w1_ref, w2_ref, gfin_ref, os_ref, h_scr)


def _ffn(xp, modp, xs, mods, g, w1c, w2c, g_final, *, tt_p, bb_s, mod_base, final_norm):
    Bp, Tp, D = xp.shape
    Bs, Ts, _ = xs.shape
    assert w2c.shape[0] % FFN_CHUNK == 0 and Tp % tt_p == 0 and Bs % bb_s == 0
    per_b = Tp // tt_p
    n_prompt = Bp * per_b
    n_sample = Bs // bb_s
    p_tile = lambda i: jnp.minimum(i, n_prompt - 1)
    s_tile = lambda i: jnp.maximum(i - n_prompt, 0)
    kern = functools.partial(_ffn_kernel, n_prompt=n_prompt, mod_base=mod_base, final_norm=final_norm)
    return pl.pallas_call(
        kern,
        grid=(n_prompt + n_sample,),
        in_specs=[
            pl.BlockSpec((1, tt_p, D), lambda i: (p_tile(i) // per_b, p_tile(i) % per_b, 0)),
            pl.BlockSpec((1, N_MOD, D), lambda i: (p_tile(i) // per_b, 0, 0)),
            pl.BlockSpec((bb_s, Ts, D), lambda i: (s_tile(i), 0, 0)),
            pl.BlockSpec((bb_s, N_MOD, D), lambda i: (s_tile(i), 0, 0)),
            _resident((1, D)),
            _resident(w1c.shape),
            _resident(w2c.shape),
            _resident((1, D)),
        ],
        out_specs=[
            pl.BlockSpec((1, tt_p, D), lambda i: (p_tile(i) // per_b, p_tile(i) % per_b, 0)),
            pl.BlockSpec((bb_s, Ts, D), lambda i: (s_tile(i), 0, 0)),
        ],
        out_shape=[jax.ShapeDtypeStruct(xp.shape, F32), jax.ShapeDtypeStruct(xs.shape, F32)],
        scratch_shapes=[pltpu.VMEM((max(tt_p, bb_s * Ts), D), BF16)],
        compiler_params=_params(("arbitrary",)),
        name="ffn_final" if final_norm else "ffn",
    )(xp, modp, xs, mods, g, w1c, w2c, g_final)


def _qkv_kernel(*refs, prompt):
    if not prompt:
        _qkv_tile(*refs, prompt=False)
        return
    (x_ref, mod_ref, g_ref, wqkv_ref, bf_ref, q_ref, kb_ref, vb_ref, k32_ref, v32_ref, lf_ref, h_scr) = refs
    rows = x_ref.shape[1] // SUB_TILES
    for s in range(SUB_TILES):
        ts = slice(s * rows, (s + 1) * rows)
        _qkv_tile(x_ref.at[:, ts, :], mod_ref, g_ref, wqkv_ref, bf_ref,
                  q_ref.at[:, ts, :], kb_ref.at[:, :, ts], vb_ref.at[:, ts, :], k32_ref.at[:, :, :, ts],
                  v32_ref.at[:, :, :, ts], lf_ref.at[:, :, ts], h_scr.at[ts, :], prompt=True)


def _qkv_tile(*refs, prompt):
    if prompt:
        (x_ref, mod_ref, g_ref, wqkv_ref, bf_ref,
         q_ref, kb_ref, vb_ref, k32_ref, v32_ref, lf_ref, h_scr) = refs
    else:
        (x_ref, mod_ref, g_ref, wqkv_ref, bf_ref, clf_ref,
         q_ref, kb_ref, vb_ref, k32_ref, v32_ref, lf_ref, fp_ref, fn_ref, h_scr) = refs
    bb, tt, d = x_ref.shape
    tm = bb * tt
    a_dim = (wqkv_ref.shape[0] - LANES) // 3
    _norm_tile(x_ref, mod_ref, g_ref, MOD_MIX, h_scr)
    h = h_scr[...]

    zq =_dot_t(h, wqkv_ref[0:a_dim, :])
    q_ref[...] = (zq * Q_SCALE).astype(BF16).reshape(bb, tt, a_dim)

    for w_idx, (lo_ref, hi_ref) in ((1, (kb_ref, k32_ref)), (2, (vb_ref, v32_ref))):
        z = _dot_t(h, wqkv_ref[w_idx * a_dim:(w_idx + 1) * a_dim, :])
        if prompt:
            zt = z.T
            hi_ref[0] = zt.reshape(N_HEADS, HEAD_DIM, tt)
            if w_idx == 1:
                lo_ref[0] = zt.astype(BF16)
            else:
                lo_ref[...] = z.astype(BF16).reshape(bb, tt, a_dim)
        else:
            lo_ref[...] = z.astype(BF16).reshape(bb, tt, a_dim)
            for hd in range(N_HEADS):
                hi_ref[:, hd, :, :] = z[:, hd * HEAD_DIM:(hd + 1) * HEAD_DIM].reshape(bb, tt, HEAD_DIM)

    zf = _dot_t(h, wqkv_ref[3 * a_dim:, :])
    lf = _log_sigmoid(zf + bf_ref[...]).T
    if not prompt:
        p_len = clf_ref.shape[2]
        f_past = LOG2E * _cumsum_lanes(clf_ref[...].reshape(bb * N_HEADS, p_len))
        fp_ref[...] = f_past.reshape(bb, N_HEADS, p_len)
    for b in range(bb):
        lf_b = lf[:N_HEADS, b * tt:(b + 1) * tt]
        lf_ref[b] = lf_b
        if not prompt:
            fn_ref[b] = f_past[b * N_HEADS:(b + 1) * N_HEADS, p_len - 1:p_len] + LOG2E * _cumsum_lanes(lf_b)


def _qkv(x, mod, g, w_t, bf, cache_logf, *, bb, tt):
    B, T, D = x.shape
    A = N_HEADS * HEAD_DIM
    prompt = cache_logf is None
    tok = lambda b, t: (b, t, 0)
    per_b = lambda b, t: (b, 0, 0)
    in_specs = [
        pl.BlockSpec((bb, tt, D), tok),
        pl.BlockSpec((bb, N_MOD, D), per_b),
        _resident((1, D)),
        _resident_rows(w_t, 0, 3 * A + LANES),
        _resident(bf.shape),
    ]
    tok_spec = pl.BlockSpec((bb, tt, A), tok)
    tok_shape = jax.ShapeDtypeStruct((B, T, A), BF16)
    lf_spec = pl.BlockSpec((bb, N_HEADS, tt), lambda b, t: (b, 0, t))
    lf_shape = jax.ShapeDtypeStruct((B, N_HEADS, T), F32)
    if prompt:
        assert bb == 1
        kv_spec = pl.BlockSpec((1, N_HEADS, HEAD_DIM, tt), lambda b, t: (b, 0, 0, t))
        kv_shape = jax.ShapeDtypeStruct((B, N_HEADS, HEAD_DIM, T), F32)
        out_specs = [tok_spec, pl.BlockSpec((1, A, tt), lambda b, t: (b, 0, t)), tok_spec, kv_spec, kv_spec, lf_spec]
        out_shape = [tok_shape, jax.ShapeDtypeStruct((B, A, T), BF16), tok_shape, kv_shape, kv_shape, lf_shape]
        args = (x, mod, g, w_t, bf)
    else:
        assert T == tt
        P = cache_logf.shape[2]
        kv_spec = pl.BlockSpec((bb, N_HEADS, tt, HEAD_DIM), lambda b, t: (b, 0, t, 0))
        kv_shape = jax.ShapeDtypeStruct((B, N_HEADS, T, HEAD_DIM), F32)
        in_specs.append(pl.BlockSpec((bb, N_HEADS, P), per_b))
        out_specs = [tok_spec, tok_spec, tok_spec, kv_spec, kv_spec, lf_spec,
                     pl.BlockSpec((bb, N_HEADS, P), per_b), lf_spec]
        out_shape = [tok_shape, tok_shape, tok_shape, kv_shape, kv_shape, lf_shape,
                     jax.ShapeDtypeStruct((B, N_HEADS, P), F32), lf_shape]
        args = (x, mod, g, w_t, bf, cache_logf)
    return pl.pallas_call(
        functools.partial(_qkv_kernel, prompt=prompt),
        grid=(B // bb, T // tt),
        in_specs=in_specs,
        out_specs=out_specs,
        out_shape=out_shape,
        scratch_shapes=[pltpu.VMEM((bb * tt, D), BF16)],
        compiler_params=_params(("parallel", "parallel")),
        name="qkv_proj",
    )(*args)


def _attn_rows(q2, kt_ref, v1_ref, negf_ref, s_scr, p_scr, nk, blk):
    rows = HEADS_PER_BLOCK * blk
    half = blk // 2
    nkeys = nk * blk
    keep = lax.broadcasted_iota(jnp.int32, (blk, blk), 1) <= lax.broadcasted_iota(jnp.int32, (blk, blk), 0)
    s = jnp.dot(q2, kt_ref[0, :, 0:nkeys], preferred_element_type=F32)
    for kb in range(nk):
        ks = slice(kb * blk, (kb + 1) * blk)
        for h in range(HEADS_PER_BLOCK):
            t = s[h * blk:(h + 1) * blk, ks] + negf_ref[h:h + 1, ks]
            if kb == nk - 1:
                t = jnp.where(keep, t, -jnp.inf)
            s_scr[h * blk:(h + 1) * blk, ks] = t

    for r0 in range(0, rows, ATTN_ROWS):
        rs = slice(r0, r0 + ATTN_ROWS)
        mx = None
        for kb in range(nk):
            t = s_scr[rs, kb * blk:(kb + 1) * blk]
            t = jnp.maximum(t[:, :half], t[:, half:])
            mx = t if mx is None else jnp.maximum(mx, t)
        mb = jnp.broadcast_to(jnp.max(mx, axis=-1, keepdims=True), (ATTN_ROWS, blk))
        for kb in range(nk):
            ks = slice(kb * blk, (kb + 1) * blk)
            p_scr[rs, ks] = jnp.exp2(s_scr[rs, ks] - mb).astype(BF16)

    acc = jnp.dot(p_scr[:, 0:nkeys], v1_ref[0:nkeys, :], preferred_element_type=F32)
    return acc[:, :LANES] * (1.0 / acc[:, LANES:])


def _attn_prompt_kernel(q_ref, kt_ref, v_ref, lf_ref, o_ref, negf_scr, v1_scr, s_scr, p_scr, *, blk):
    T = v_ref.shape[1]
    negf_scr[...] = -LOG2E * _cumsum_lanes(lf_ref[0, 0])
    for pr in range(ATTN_PAIRS):
        v1_scr[pr, :, :LANES] = v_ref[0, :, pr * LANES:(pr + 1) * LANES]
        v1_scr[pr, :, LANES:] = jnp.ones((T, LANES), v1_scr.dtype)
    lane = lax.broadcasted_iota(jnp.int32, (blk, LANES), 1)
    nq = T // blk
    order = list(range(0, nq, 2)) + list(range(nq - 1 - nq % 2, 0, -2))
    slot = 0
    for c in order:
        for pr in range(ATTN_PAIRS):
            lanes = slice(pr * LANES, (pr + 1) * LANES)
            q = q_ref[0, c * blk:(c + 1) * blk, lanes]
            zero = jnp.zeros_like(q)
            q2 = jnp.concatenate([jnp.where(lane < HEAD_DIM, q, zero), jnp.where(lane >= HEAD_DIM, q, zero)],
                                 axis=0)
            o = _attn_rows(q2, kt_ref.at[:, lanes, :], v1_scr.at[pr],
                           negf_scr.at[pr * HEADS_PER_BLOCK:(pr + 1) * HEADS_PER_BLOCK, :],
                           s_scr.at[slot % ATTN_SLOTS], p_scr.at[slot % ATTN_SLOTS], c + 1, blk)
            o_ref[0, c * blk:(c + 1) * blk, lanes] = jnp.where(lane < HEAD_DIM, o[:blk], o[blk:]).astype(o_ref.dtype)
            slot += 1


def _attn_prompt(q, kt, v, logf, *, blk):
    B, T, A = q.shape
    width = ATTN_PAIRS * LANES
    ngrp = A // width
    heads = ATTN_PAIRS * HEADS_PER_BLOCK
    lf = logf.reshape(B, ngrp, heads, T)
    kern = functools.partial(_attn_prompt_kernel, blk=blk)
    grp = pl.BlockSpec((1, T, width), lambda b, p: (b, 0, p))
    grp_t = pl.BlockSpec((1, width, T), lambda b, p: (b, p, 0))
    return pl.pallas_call(
        kern,
        grid=(B, ngrp),
        in_specs=[grp, grp_t, grp, pl.BlockSpec((1, 1, heads, T), lambda b, p: (b, p, 0, 0))],
        out_specs=grp,
        out_shape=jax.ShapeDtypeStruct((B, T, A), BF16),
        scratch_shapes=[
            pltpu.VMEM((heads, T), F32),
            pltpu.VMEM((ATTN_PAIRS, T, 2 * LANES), BF16),
            pltpu.VMEM((ATTN_SLOTS, HEADS_PER_BLOCK * blk, T), F32),
            pltpu.VMEM((ATTN_SLOTS, HEADS_PER_BLOCK * blk, T), BF16),
        ],
        compiler_params=_params(("parallel", "parallel")),
        name="attn_prompt",
    )(q, kt, v, lf)


def _attn_sample_kernel(q_ref, kn_ref, vn_ref, ckt_ref, cvt_ref, fp_ref, fn_ref, o_ref):
    tt = q_ref.shape[1]
    p_len = ckt_ref.shape[3]
    width = SAMPLE_STACK * HEAD_DIM
    nt = (((1,), (1,)), ((), ()))
    lane_head = lax.broadcasted_iota(jnp.int32, (tt, width), 1) // HEAD_DIM
    keep = lax.broadcasted_iota(jnp.int32, (tt, tt), 1) <= lax.broadcasted_iota(jnp.int32, (tt, tt), 0)
    for g in range(ckt_ref.shape[1] // SAMPLE_STACK):
        heads = range(g * SAMPLE_STACK, (g + 1) * SAMPLE_STACK)
        lanes = slice(g * width, (g + 1) * width)
        q = q_ref[0][:, lanes]
        zero = jnp.zeros_like(q)
        qs = jnp.concatenate([jnp.where(lane_head == i, q, zero) for i in range(SAMPLE_STACK)], axis=0)
        kt = ckt_ref[0, heads.start:heads.stop].reshape(width, p_len).astype(BF16)
        vt = cvt_ref[0, heads.start:heads.stop].reshape(width, p_len).astype(BF16)
        s_p = jnp.dot(qs, kt, preferred_element_type=F32)
        s_n = lax.dot_general(qs, kn_ref[0][:, lanes], nt, preferred_element_type=F32)
        s_p = jnp.concatenate([s_p[i * tt:(i + 1) * tt] - fp_ref[0, 0, hd:hd + 1, :]
                               for i, hd in enumerate(heads)], axis=0)
        s_n = jnp.concatenate([jnp.where(keep, s_n[i * tt:(i + 1) * tt] - fn_ref[0, 0, hd:hd + 1, :], -jnp.inf)
                               for i, hd in enumerate(heads)], axis=0)
        m = jnp.maximum(jnp.max(s_p, axis=-1, keepdims=True), jnp.max(s_n, axis=-1, keepdims=True))
        p_p = jnp.exp2(s_p - m)
        p_n = jnp.exp2(s_n - m)
        l = jnp.sum(p_p, axis=-1, keepdims=True) + jnp.sum(p_n, axis=-1, keepdims=True)
        o = (lax.dot_general(p_p.astype(BF16), vt, nt, preferred_element_type=F32)
             + jnp.dot(p_n.astype(BF16), vn_ref[0][:, lanes], preferred_element_type=F32)) * (1.0 / l)
        out = jnp.zeros((tt, width), F32)
        for i in range(SAMPLE_STACK):
            out = jnp.where(lane_head == i, o[i * tt:(i + 1) * tt], out)
        o_ref[0, :, lanes] = out.astype(o_ref.dtype)


def _attn_sample(q, k, v, cache_kt, cache_vt, f_past, f_new):
    B, T, A = q.shape
    P = cache_kt.shape[3]
    hg = SAMPLE_HEAD_GROUP
    ngrp = N_HEADS // hg
    clf = f_past.reshape(B, ngrp, hg, P)
    lf = f_new.reshape(B, ngrp, hg, T)
    tok = lambda b, p: (b, 0, p)
    grp = lambda b, p: (b, p, 0, 0)
    return pl.pallas_call(
        _attn_sample_kernel,
        grid=(B, ngrp),
        in_specs=[
            pl.BlockSpec((1, T, hg * HEAD_DIM), tok),
            pl.BlockSpec((1, T, hg * HEAD_DIM), tok),
            pl.BlockSpec((1, T, hg * HEAD_DIM), tok),
            pl.BlockSpec((1, hg, HEAD_DIM, P), grp),
            pl.BlockSpec((1, hg, HEAD_DIM, P), grp),
            pl.BlockSpec((1, 1, hg, P), grp),
            pl.BlockSpec((1, 1, hg, T), grp),
        ],
        out_specs=pl.BlockSpec((1, T, hg * HEAD_DIM), tok),
        out_shape=jax.ShapeDtypeStruct((B, T, A), BF16),
        compiler_params=_params(("parallel", "parallel")),
        name="attn_sample",
    )(q, k, v, cache_kt, cache_vt, clf, lf)


def _mix_kernel(x_ref, o_ref, mod_ref, g_ref, w5_ref, cw_ref, pu_ref, wo_ref, y_ref, nu_ref, carry_scr, h_scr):
    bb, tt, d = x_ref.shape
    tm = bb * tt

    @pl.when(pl.program_id(1) == 0)
    def _():
        carry_scr[...] = pu_ref[...]

    gate = mod_ref[:, MOD_MIX + 2:MOD_MIX + 3, :]
    _norm_tile(x_ref, mod_ref, g_ref, MOD_MIX, h_scr)
    h = h_scr[...]
    cdim = w5_ref.shape[0] // 5
    proj = lambda i: _dot_t(h, w5_ref[i * cdim:(i + 1) * cdim, :])

    u2 = proj(1) * proj(2)
    u = u2.reshape(bb, tt, cdim)
    tpos = lax.broadcasted_iota(jnp.int32, (1, tt, 1), 1)
    c0 = carry_scr[:, 0:1, :]
    c1 = carry_scr[:, 1:2, :]
    um1 = jnp.where(tpos == 0, c1, pltpu.roll(u2, 1, axis=0).reshape(bb, tt, cdim))
    um2 = jnp.where(tpos == 0, c0, jnp.where(tpos == 1, c1, pltpu.roll(u2, 2, axis=0).reshape(bb, tt, cdim)))
    conv = cw_ref[0:1, :] * um2 + cw_ref[1:2, :] * um1 + cw_ref[2:3, :] * u
    new_u = u[:, tt - (CONV_WIDTH - 1):, :]
    carry_scr[...] = new_u
    nu_ref[...] = new_u

    o_conv = proj(0) * conv.reshape(tm, cdim)
    m = _sigmoid(proj(3)) * o_ref[...].reshape(tm, cdim).astype(F32) + _sigmoid(proj(4)) * o_conv
    mo = jnp.dot(m.astype(BF16), wo_ref[...], preferred_element_type=F32)
    y_ref[...] = x_ref[...] + gate * mo.reshape(bb, tt, d)


def _mix(x, o_attn, mod, g, w_t, conv_w, past_u, w_out, *, bb, tt):
    B, T, D = x.shape
    C = conv_w.shape[1]
    tok = lambda b, t: (b, t, 0)
    per_b = lambda b, t: (b, 0, 0)
    return pl.pallas_call(
        _mix_kernel,
        grid=(B // bb, T // tt),
        in_specs=[
            pl.BlockSpec((bb, tt, D), tok),
            pl.BlockSpec((bb, tt, C), tok),
            pl.BlockSpec((bb, N_MOD, D), per_b),
            _resident((1, D)),
            _resident_rows(w_t, w_t.shape[0] - 5 * C, 5 * C),
            _resident(conv_w.shape),
            pl.BlockSpec((bb, CONV_WIDTH - 1, C), per_b),
            _resident(w_out.shape),
        ],
        out_specs=[
            pl.BlockSpec((bb, tt, D), tok),
            pl.BlockSpec((bb, CONV_WIDTH - 1, C), per_b),
        ],
        out_shape=[
            jax.ShapeDtypeStruct((B, T, D), F32),
            jax.ShapeDtypeStruct((B, CONV_WIDTH - 1, C), F32),
        ],
        scratch_shapes=[pltpu.VMEM((bb, CONV_WIDTH - 1, C), F32), pltpu.VMEM((bb * tt, D), BF16)],
        compiler_params=_params(("parallel", "arbitrary")),
        name="mix",
    )(x, o_attn, mod, g, w_t, conv_w, past_u, w_out)


def _layer(xp, xs, mod_p, mod_s, past_s, past_u_p, weights, g_final):
    (g1, w1a, w1b, gm, w_t, bf, conv_w, w_out, g2, w2a, w2b) = weights
    past_kt, past_vt, past_logf, past_u_s = past_s
    Ts = xs.shape[1]
    tt_p, bb_s = PROMPT_ROW_TILE, SAMPLE_ROW_TILE // Ts
    ffn_tiles = dict(tt_p=tt_p, bb_s=bb_s)
    xp1, xs1 = _ffn(xp, mod_p, xs, mod_s, g1, w1a, w1b, g_final, mod_base=MOD_FFN1, final_norm=False, **ffn_tiles)

    q, kbt, vb, kp, vp, fp = _qkv(xp1, mod_p, gm, w_t, bf, None, bb=1, tt=tt_p)
    op = _attn_prompt(q, kbt, vb, fp, blk=ATTN_BLOCK)
    kp, vp = jnp.swapaxes(kp, 2, 3), jnp.swapaxes(vp, 2, 3)
    q, kb, vb, ks, vs, fs, f_past, f_new = _qkv(xs1, mod_s, gm, w_t, bf, past_logf, bb=bb_s, tt=Ts)
    os_ = _attn_sample(q, kb, vb, past_kt, past_vt, f_past, f_new)

    xp2, up = _mix(xp1, op, mod_p, gm, w_t, conv_w, past_u_p, w_out, bb=1, tt=tt_p)
    xs2, us = _mix(xs1, os_, mod_s, gm, w_t, conv_w, past_u_s, w_out, bb=bb_s, tt=Ts)
    yp, ys = _ffn(xp2, mod_p, xs2, mod_s, g2, w2a, w2b, g_final, mod_base=MOD_FFN2, final_norm=True, **ffn_tiles)
    return (yp, kp, vp, fp, up), (ys, ks, vs, fs, us)


def kernel(x_prompt, x_sample, cache_k, cache_v, cache_logf, state_conv, c_prompt, c_sample, w_ada, b_ada, g_ffn1, w_ffn1_in, w_ffn1_out, g_mix, w_in, b_f, conv_w, w_out, g_ffn2, w_ffn2_in, w_ffn2_out, g_final):
    assert w_ada.shape[0] == 1, "single-layer encoder"
    Bp, Tp, D = x_prompt.shape
    Bs, Ts, _ = x_sample.shape
    A = N_HEADS * HEAD_DIM
    C = conv_w.shape[2]

    mod = _ada(jnp.concatenate([c_prompt, c_sample], axis=0), w_ada[0], b_ada[0])
    mod = mod.reshape(Bp + Bs, N_MOD, D)
    mod_p, mod_s = mod[:Bp], mod[Bp:]

    assert w_in.shape[2] == 3 * A + N_HEADS + 5 * C
    w_t = jnp.swapaxes(w_in[0], 0, 1).astype(BF16)
    bf = jnp.pad(b_f[0], (0, LANES - N_HEADS)).reshape(1, LANES)
    weights = (g_ffn1, w_ffn1_in[0].astype(BF16), w_ffn1_out[0].astype(BF16), g_mix, w_t, bf,
               conv_w[0], w_out[0].astype(BF16), g_ffn2, w_ffn2_in[0].astype(BF16), w_ffn2_out[0].astype(BF16))
    gfin = g_final.reshape(1, D)

    zero_u = jnp.zeros((Bp, CONV_WIDTH - 1, C), F32)
    past = (jnp.swapaxes(cache_k[0], 2, 3), jnp.swapaxes(cache_v[0], 2, 3), cache_logf[0], state_conv[0])
    (yp, kp, vp, fp, up), (ys, ks, vs, fs, us) = _layer(x_prompt, x_sample, mod_p, mod_s, past, zero_u,
                                                          weights, gfin)
    return (yp, ys, kp[None], vp[None], fp[None], up[None], ks[None], vs[None], fs[None], us[None])
```

```python
import functools

import jax
import jax.numpy as jnp
from jax import lax
from jax.experimental import pallas as pl
from jax.experimental.pallas import tpu as pltpu

F32 = jnp.float32
BF16 = jnp.bfloat16

EPS = 1e-6
N_HEADS = 16
HEAD_DIM = 64
N_MOD = 9
MOD_FFN1, MOD_MIX, MOD_FFN2 = 0, 3, 6
CONV_WIDTH = 3
LANES = 128
HEADS_PER_BLOCK = LANES // HEAD_DIM
V7X_VMEM_LIMIT_BYTES = 56 * 1024 * 1024
ADA_STEPS = 3
FFN_CHUNK = 256
PROMPT_ROW_TILE = 1024
SUB_TILES = 2
SAMPLE_ROW_TILE = 512
ATTN_BLOCK = 256
ATTN_ROWS = 64
ATTN_PAIRS = 4
ATTN_SLOTS = 4
BIAS_ROWS = 16
LOG2E = 1.4426950408889634
Q_SCALE = HEAD_DIM ** -0.5 * LOG2E
SAMPLE_HEAD_GROUP = 16
SAMPLE_STACK = 4


def _params(semantics):
    return pltpu.CompilerParams(dimension_semantics=semantics, vmem_limit_bytes=V7X_VMEM_LIMIT_BYTES)


def _resident(shape):
    zeros = (0,) * len(shape)
    return pl.BlockSpec(shape, lambda *_: zeros, pipeline_mode=pl.Buffered(1))


def _resident_rows(w, row0, nrows):
    assert row0 + nrows <= w.shape[0]
    return pl.BlockSpec((pl.Element(nrows), pl.Element(w.shape[1])), lambda *_: (row0, 0),
                        pipeline_mode=pl.Buffered(1))


def _rms_mod(x, g, scale, shift):
    ms = jnp.mean(x * x, axis=-1, keepdims=True)
    y = x * lax.rsqrt(ms + EPS) * g
    return y * (1.0 + scale) + shift


def _norm_tile(x_ref, mod_ref, g_ref, shift_idx, h_scr):
    bb, tt, d = x_ref.shape
    shift = mod_ref[:, shift_idx:shift_idx + 1, :]
    scale = mod_ref[:, shift_idx + 1:shift_idx + 2, :]
    h_scr[0:bb * tt, :] = _rms_mod(x_ref[...], g_ref[...], scale, shift).reshape(bb * tt, d).astype(BF16)


def _dot_t(x, w_t):
    return lax.dot_general(x, w_t, (((1,), (1,)), ((), ())), preferred_element_type=F32)


def _sigmoid(x):
    return 1.0 / (1.0 + jnp.exp(-x))


def _log_sigmoid(x):
    return jnp.minimum(x, 0.0) - jnp.log1p(jnp.exp(-jnp.abs(x)))


def _cumsum_lanes(x):
    n = x.shape[-1]
    lane = lax.broadcasted_iota(jnp.int32, x.shape, x.ndim - 1)
    step = 1
    while step < n:
        x = x + jnp.where(lane >= step, pltpu.roll(x, step, axis=x.ndim - 1), 0.0)
        step *= 2
    return x


def _ada_kernel(c_ref, w_ref, b_ref, o_ref):
    c = c_ref[...]
    a = (c * _sigmoid(c)).astype(BF16)
    o_ref[...] = jnp.dot(a, w_ref[...].astype(BF16), preferred_element_type=F32) + b_ref[...]


def _ada(c, w_ada, b_ada):
    n, d = c.shape
    cols = w_ada.shape[1]
    tn = cols // ADA_STEPS
    assert cols % ADA_STEPS == 0 and tn % LANES == 0
    return pl.pallas_call(
        _ada_kernel,
        grid=(cols // tn,),
        in_specs=[
            pl.BlockSpec((n, d), lambda j: (0, 0)),
            pl.BlockSpec((d, tn), lambda j: (0, j)),
            pl.BlockSpec((1, tn), lambda j: (0, j)),
        ],
        out_specs=pl.BlockSpec((n, tn), lambda j: (0, j)),
        out_shape=jax.ShapeDtypeStruct((n, cols), F32),
        compiler_params=_params(("parallel",)),
        name="ada_mod",
    )(c, w_ada, b_ada.reshape(1, cols))


def _ffn_tile(x_ref, mod_ref, g_ref, w1_ref, w2_ref, gfin_ref, o_ref, h_scr, *, mod_base, final_norm):
    bb, tt, d = x_ref.shape
    tm = bb * tt
    f = w2_ref.shape[0]
    fc = FFN_CHUNK
    gate = mod_ref[:, mod_base + 2:mod_base + 3, :]
    _norm_tile(x_ref, mod_ref, g_ref, mod_base, h_scr)

    acc = None
    for lo in range(0, f, fc):
        h = h_scr[0:tm, :]
        a = jnp.dot(h, w1_ref[:, lo:lo + fc], preferred_element_type=F32)
        b = jnp.dot(h, w1_ref[:, f + lo:f + lo + fc], preferred_element_type=F32)
        act = (a * _sigmoid(a) * b).astype(BF16)
        part = jnp.dot(act, w2_ref[lo:lo + fc, :], preferred_element_type=F32)
        acc = part if acc is None else acc + part

    y = x_ref[...] + 0.5 * gate * acc.reshape(bb, tt, d)
    if final_norm:
        ms = jnp.mean(y * y, axis=-1, keepdims=True)
        y = y * lax.rsqrt(ms + EPS) * gfin_ref[...]
    o_ref[...] = y


def _ffn_kernel(xp_ref, modp_ref, xs_ref, mods_ref, g_ref, w1_ref, w2_ref, gfin_ref, op_ref, os_ref, h_scr, *,
                n_prompt, mod_base, final_norm):
    step = pl.program_id(0)
    tile = functools.partial(_ffn_tile, mod_base=mod_base, final_norm=final_norm)

    @pl.when(step < n_prompt)
    def _():
        tile(xp_ref, modp_ref, g_ref, w1_ref, w2_ref, gfin_ref, op_ref, h_scr)

    @pl.when(step >= n_prompt)
    def _():
        tile(xs_ref, mods_ref, g_ref, w1_ref, w2_ref, gfin_ref, os_ref, h_scr)


def _ffn(xp, modp, xs, mods, g, w1c, w2c, g_final, *, tt_p, bb_s, mod_base, final_norm):
    Bp, Tp, D = xp.shape
    Bs, Ts, _ = xs.shape
    assert w2c.shape[0] % FFN_CHUNK == 0 and Tp % tt_p == 0 and Bs % bb_s == 0
    per_b = Tp // tt_p
    n_prompt = Bp * per_b
    n_sample = Bs // bb_s
    p_tile = lambda i: jnp.minimum(i, n_prompt - 1)
    s_tile = lambda i: jnp.maximum(i - n_prompt, 0)
    kern = functools.partial(_ffn_kernel, n_prompt=n_prompt, mod_base=mod_base, final_norm=final_norm)
    return pl.pallas_call(
        kern,
        grid=(n_prompt + n_sample,),
        in_specs=[
            pl.BlockSpec((1, tt_p, D), lambda i: (p_tile(i) // per_b, p_tile(i) % per_b, 0)),
            pl.BlockSpec((1, N_MOD, D), lambda i: (p_tile(i) // per_b, 0, 0)),
            pl.BlockSpec((bb_s, Ts, D), lambda i: (s_tile(i), 0, 0)),
            pl.BlockSpec((bb_s, N_MOD, D), lambda i: (s_tile(i), 0, 0)),
            _resident((1, D)),
            _resident(w1c.shape),
            _resident(w2c.shape),
            _resident((1, D)),
        ],
        out_specs=[
            pl.BlockSpec((1, tt_p, D), lambda i: (p_tile(i) // per_b, p_tile(i) % per_b, 0)),
            pl.BlockSpec((bb_s, Ts, D), lambda i: (s_tile(i), 0, 0)),
        ],
        out_shape=[jax.ShapeDtypeStruct(xp.shape, F32), jax.ShapeDtypeStruct(xs.shape, F32)],
        scratch_shapes=[pltpu.VMEM((max(tt_p, bb_s * Ts), D), BF16)],
        compiler_params=_params(("arbitrary",)),
        name="ffn_final" if final_norm else "ffn",
    )(xp, modp, xs, mods, g, w1c, w2c, g_final)


def _qkv_kernel(*refs, prompt):
    if not prompt:
        _qkv_tile(*refs, prompt=False)
        return
    (x_ref, mod_ref, g_ref, wqkv_ref, bf_ref, q_ref, kb_ref, vb_ref, k32_ref, v32_ref, lf_ref, h_scr) = refs
    rows = x_ref.shape[1] // SUB_TILES
    for s in range(SUB_TILES):
        ts = slice(s * rows, (s + 1) * rows)
        _qkv_tile(x_ref.at[:, ts, :], mod_ref, g_ref, wqkv_ref, bf_ref,
                  q_ref.at[:, ts, :], kb_ref.at[:, :, ts], vb_ref.at[:, ts, :], k32_ref.at[:, :, :, ts],
                  v32_ref.at[:, :, :, ts], lf_ref.at[:, :, ts], h_scr.at[ts, :], prompt=True)


def _qkv_tile(*refs, prompt):
    if prompt:
        (x_ref, mod_ref, g_ref, wqkv_ref, bf_ref,
         q_ref, kb_ref, vb_ref, k32_ref, v32_ref, lf_ref, h_scr) = refs
    else:
        (x_ref, mod_ref, g_ref, wqkv_ref, bf_ref, clf_ref,
         q_ref, kb_ref, vb_ref, k32_ref, v32_ref, lf_ref, fp_ref, fn_ref, h_scr) = refs
    bb, tt, d = x_ref.shape
    tm = bb * tt
    a_dim = (wqkv_ref.shape[0] - LANES) // 3
    _norm_tile(x_ref, mod_ref, g_ref, MOD_MIX, h_scr)
    h = h_scr[...]

    zq =_dot_t(h, wqkv_ref[0:a_dim, :])
    q_ref[...] = (zq * Q_SCALE).astype(BF16).reshape(bb, tt, a_dim)

    for w_idx, (lo_ref, hi_ref) in ((1, (kb_ref, k32_ref)), (2, (vb_ref, v32_ref))):
        z = _dot_t(h, wqkv_ref[w_idx * a_dim:(w_idx + 1) * a_dim, :])
        if prompt:
            zt = z.T
            hi_ref[0] = zt.reshape(N_HEADS, HEAD_DIM, tt)
            if w_idx == 1:
                lo_ref[0] = zt.astype(BF16)
            else:
                lo_ref[...] = z.astype(BF16).reshape(bb, tt, a_dim)
        else:
            lo_ref[...] = z.astype(BF16).reshape(bb, tt, a_dim)
            for hd in range(N_HEADS):
                hi_ref[:, hd, :, :] = z[:, hd * HEAD_DIM:(hd + 1) * HEAD_DIM].reshape(bb, tt, HEAD_DIM)

    zf = _dot_t(h, wqkv_ref[3 * a_dim:, :])
    lf = _log_sigmoid(zf + bf_ref[...]).T
    if not prompt:
        p_len = clf_ref.shape[2]
        f_past = LOG2E * _cumsum_lanes(clf_ref[...].reshape(bb * N_HEADS, p_len))
        fp_ref[...] = f_past.reshape(bb, N_HEADS, p_len)
    for b in range(bb):
        lf_b = lf[:N_HEADS, b * tt:(b + 1) * tt]
        lf_ref[b] = lf_b
        if not prompt:
            fn_ref[b] = f_past[b * N_HEADS:(b + 1) * N_HEADS, p_len - 1:p_len] + LOG2E * _cumsum_lanes(lf_b)


def _qkv(x, mod, g, w_t, bf, cache_logf, *, bb, tt):
    B, T, D = x.shape
    A = N_HEADS * HEAD_DIM
    prompt = cache_logf is None
    tok = lambda b, t: (b, t, 0)
    per_b = lambda b, t: (b, 0, 0)
    in_specs = [
        pl.BlockSpec((bb, tt, D), tok),
        pl.BlockSpec((bb, N_MOD, D), per_b),
        _resident((1, D)),
        _resident_rows(w_t, 0, 3 * A + LANES),
        _resident(bf.shape),
    ]
    tok_spec = pl.BlockSpec((bb, tt, A), tok)
    tok_shape = jax.ShapeDtypeStruct((B, T, A), BF16)
    lf_spec = pl.BlockSpec((bb, N_HEADS, tt), lambda b, t: (b, 0, t))
    lf_shape = jax.ShapeDtypeStruct((B, N_HEADS, T), F32)
    if prompt:
        assert bb == 1
        kv_spec = pl.BlockSpec((1, N_HEADS, HEAD_DIM, tt), lambda b, t: (b, 0, 0, t))
        kv_shape = jax.ShapeDtypeStruct((B, N_HEADS, HEAD_DIM, T), F32)
        out_specs = [tok_spec, pl.BlockSpec((1, A, tt), lambda b, t: (b, 0, t)), tok_spec, kv_spec, kv_spec, lf_spec]
        out_shape = [tok_shape, jax.ShapeDtypeStruct((B, A, T), BF16), tok_shape, kv_shape, kv_shape, lf_shape]
        args = (x, mod, g, w_t, bf)
    else:
        assert T == tt
        P = cache_logf.shape[2]
        kv_spec = pl.BlockSpec((bb, N_HEADS, tt, HEAD_DIM), lambda b, t: (b, 0, t, 0))
        kv_shape = jax.ShapeDtypeStruct((B, N_HEADS, T, HEAD_DIM), F32)
        in_specs.append(pl.BlockSpec((bb, N_HEADS, P), per_b))
        out_specs = [tok_spec, tok_spec, tok_spec, kv_spec, kv_spec, lf_spec,
                     pl.BlockSpec((bb, N_HEADS, P), per_b), lf_spec]
        out_shape = [tok_shape, tok_shape, tok_shape, kv_shape, kv_shape, lf_shape,
                     jax.ShapeDtypeStruct((B, N_HEADS, P), F32), lf_shape]
        args = (x, mod, g, w_t, bf, cache_logf)
    return pl.pallas_call(
        functools.partial(_qkv_kernel, prompt=prompt),
        grid=(B // bb, T // tt),
        in_specs=in_specs,
        out_specs=out_specs,
        out_shape=out_shape,
        scratch_shapes=[pltpu.VMEM((bb * tt, D), BF16)],
        compiler_params=_params(("parallel", "parallel")),
        name="qkv_proj",
    )(*args)


def _attn_rows(q2, k1_ref, v1_ref, s_scr, p_scr, nk, blk):
    rows = HEADS_PER_BLOCK * blk
    half = blk // 2
    nkeys = nk * blk
    keep = lax.broadcasted_iota(jnp.int32, (blk, blk), 1) <= lax.broadcasted_iota(jnp.int32, (blk, blk), 0)
    s = jnp.dot(q2, k1_ref[:, 0:nkeys], preferred_element_type=F32)
    for h in range(HEADS_PER_BLOCK):
        hs = slice(h * blk, (h + 1) * blk)
        if nk > 1:
            s_scr[hs, 0:nkeys - blk] = s[hs, 0:nkeys - blk]
        s_scr[hs, nkeys - blk:nkeys] = jnp.where(keep, s[hs, nkeys - blk:nkeys], -jnp.inf)

    for r0 in range(0, rows, ATTN_ROWS):
        rs = slice(r0, r0 + ATTN_ROWS)
        mx = None
        for kb in range(nk):
            t = s_scr[rs, kb * blk:(kb + 1) * blk]
            t = jnp.maximum(t[:, :half], t[:, half:])
            mx = t if mx is None else jnp.maximum(mx, t)
        mb = jnp.broadcast_to(jnp.max(mx, axis=-1, keepdims=True), (ATTN_ROWS, blk))
        for kb in range(nk):
            ks = slice(kb * blk, (kb + 1) * blk)
            p_scr[rs, ks] = jnp.exp2(s_scr[rs, ks] - mb).astype(BF16)

    acc = jnp.dot(p_scr[:, 0:nkeys], v1_ref[0:nkeys, :], preferred_element_type=F32)
    return acc[:, :LANES] * (1.0 / acc[:, LANES:])


def _attn_prompt_kernel(q_ref, kt_ref, v_ref, lf_ref, o_ref, k1_scr, v1_scr, s_scr, p_scr, *, blk):
    T = v_ref.shape[1]
    negf = -LOG2E * _cumsum_lanes(lf_ref[0, 0])
    hi = negf.astype(BF16).astype(F32)
    mid = (negf - hi).astype(BF16).astype(F32)
    lo = negf - hi - mid
    pieces = (hi, mid, lo)
    brow = lax.broadcasted_iota(jnp.int32, (BIAS_ROWS, T), 0)
    for pr in range(ATTN_PAIRS):
        v1_scr[pr, :, :LANES] = v_ref[0, :, pr * LANES:(pr + 1) * LANES]
        v1_scr[pr, :, LANES:] = jnp.ones((T, LANES), v1_scr.dtype)
        k1_scr[pr, 0:LANES, :] = kt_ref[0, pr * LANES:(pr + 1) * LANES, :]
        bias = jnp.zeros((BIAS_ROWS, T), F32)
        for h in range(HEADS_PER_BLOCK):
            hd = pr * HEADS_PER_BLOCK + h
            for j, piece in enumerate(pieces):
                bias = jnp.where(brow == len(pieces) * h + j, piece[hd:hd + 1, :], bias)
        k1_scr[pr, LANES:LANES + BIAS_ROWS, :] = bias.astype(BF16)
        k1_scr[pr, LANES + BIAS_ROWS:, :] = jnp.zeros((LANES - BIAS_ROWS, T), BF16)
    lane = lax.broadcasted_iota(jnp.int32, (blk, LANES), 1)
    sel = jnp.concatenate([jnp.where((lane >= len(pieces) * h) & (lane < len(pieces) * (h + 1)), 1.0, 0.0)
                           for h in range(HEADS_PER_BLOCK)], axis=0).astype(BF16)
    nq = T // blk
    order = list(range(0, nq, 2)) + list(range(nq - 1 - nq % 2, 0, -2))
    slot = 0
    for c in order:
        for pr in range(ATTN_PAIRS):
            lanes = slice(pr * LANES, (pr + 1) * LANES)
            q = q_ref[0, c * blk:(c + 1) * blk, lanes]
            zero = jnp.zeros_like(q)
            q2 = jnp.concatenate([jnp.where(lane < HEAD_DIM, q, zero), jnp.where(lane >= HEAD_DIM, q, zero)],
                                 axis=0)
            q2 = jnp.concatenate([q2, sel], axis=1)
            o = _attn_rows(q2, k1_scr.at[pr], v1_scr.at[pr],
                           s_scr.at[slot % ATTN_SLOTS], p_scr.at[slot % ATTN_SLOTS], c + 1, blk)
            o_ref[0, c * blk:(c + 1) * blk, lanes] = jnp.where(lane < HEAD_DIM, o[:blk], o[blk:]).astype(o_ref.dtype)
            slot += 1


def _attn_prompt(q, kt, v, logf, *, blk):
    B, T, A = q.shape
    width = ATTN_PAIRS * LANES
    ngrp = A // width
    heads = ATTN_PAIRS * HEADS_PER_BLOCK
    lf = logf.reshape(B, ngrp, heads, T)
    kern = functools.partial(_attn_prompt_kernel, blk=blk)
    grp = pl.BlockSpec((1, T, width), lambda b, p: (b, 0, p))
    grp_t = pl.BlockSpec((1, width, T), lambda b, p: (b, p, 0))
    return pl.pallas_call(
        kern,
        grid=(B, ngrp),
        in_specs=[grp, grp_t, grp, pl.BlockSpec((1, 1, heads, T), lambda b, p: (b, p, 0, 0))],
        out_specs=grp,
        out_shape=jax.ShapeDtypeStruct((B, T, A), BF16),
        scratch_shapes=[
            pltpu.VMEM((ATTN_PAIRS, 2 * LANES, T), BF16),
            pltpu.VMEM((ATTN_PAIRS, T, 2 * LANES), BF16),
            pltpu.VMEM((ATTN_SLOTS, HEADS_PER_BLOCK * blk, T), F32),
            pltpu.VMEM((ATTN_SLOTS, HEADS_PER_BLOCK * blk, T), BF16),
        ],
        compiler_params=_params(("parallel", "parallel")),
        name="attn_prompt",
    )(q, kt, v, lf)


def _attn_sample_kernel(q_ref, kn_ref, vn_ref, ckt_ref, cvt_ref, fp_ref, fn_ref, o_ref):
    tt = q_ref.shape[1]
    p_len = ckt_ref.shape[3]
    width = SAMPLE_STACK * HEAD_DIM
    nt = (((1,), (1,)), ((), ()))
    lane_head = lax.broadcasted_iota(jnp.int32, (tt, width), 1) // HEAD_DIM
    keep = lax.broadcasted_iota(jnp.int32, (tt, tt), 1) <= lax.broadcasted_iota(jnp.int32, (tt, tt), 0)
    for g in range(ckt_ref.shape[1] // SAMPLE_STACK):
        heads = range(g * SAMPLE_STACK, (g + 1) * SAMPLE_STACK)
        lanes = slice(g * width, (g + 1) * width)
        q = q_ref[0][:, lanes]
        zero = jnp.zeros_like(q)
        qs = jnp.concatenate([jnp.where(lane_head == i, q, zero) for i in range(SAMPLE_STACK)], axis=0)
        kt = ckt_ref[0, heads.start:heads.stop].reshape(width, p_len).astype(BF16)
        vt = cvt_ref[0, heads.start:heads.stop].reshape(width, p_len).astype(BF16)
        s_p = jnp.dot(qs, kt, preferred_element_type=F32)
        s_n = lax.dot_general(qs, kn_ref[0][:, lanes], nt, preferred_element_type=F32)
        s_p = jnp.concatenate([s_p[i * tt:(i + 1) * tt] - fp_ref[0, 0, hd:hd + 1, :]
                               for i, hd in enumerate(heads)], axis=0)
        s_n = jnp.concatenate([jnp.where(keep, s_n[i * tt:(i + 1) * tt] - fn_ref[0, 0, hd:hd + 1, :], -jnp.inf)
                               for i, hd in enumerate(heads)], axis=0)
        m = jnp.maximum(jnp.max(s_p, axis=-1, keepdims=True), jnp.max(s_n, axis=-1, keepdims=True))
        p_p = jnp.exp2(s_p - m)
        p_n = jnp.exp2(s_n - m)
        l = jnp.sum(p_p, axis=-1, keepdims=True) + jnp.sum(p_n, axis=-1, keepdims=True)
        o = (lax.dot_general(p_p.astype(BF16), vt, nt, preferred_element_type=F32)
             + jnp.dot(p_n.astype(BF16), vn_ref[0][:, lanes], preferred_element_type=F32)) * (1.0 / l)
        out = jnp.zeros((tt, width), F32)
        for i in range(SAMPLE_STACK):
            out = jnp.where(lane_head == i, o[i * tt:(i + 1) * tt], out)
        o_ref[0, :, lanes] = out.astype(o_ref.dtype)


def _attn_sample(q, k, v, cache_kt, cache_vt, f_past, f_new):
    B, T, A = q.shape
    P = cache_kt.shape[3]
    hg = SAMPLE_HEAD_GROUP
    ngrp = N_HEADS // hg
    clf = f_past.reshape(B, ngrp, hg, P)
    lf = f_new.reshape(B, ngrp, hg, T)
    tok = lambda b, p: (b, 0, p)
    grp = lambda b, p: (b, p, 0, 0)
    return pl.pallas_call(
        _attn_sample_kernel,
        grid=(B, ngrp),
        in_specs=[
            pl.BlockSpec((1, T, hg * HEAD_DIM), tok),
            pl.BlockSpec((1, T, hg * HEAD_DIM), tok),
            pl.BlockSpec((1, T, hg * HEAD_DIM), tok),
            pl.BlockSpec((1, hg, HEAD_DIM, P), grp),
            pl.BlockSpec((1, hg, HEAD_DIM, P), grp),
            pl.BlockSpec((1, 1, hg, P), grp),
            pl.BlockSpec((1, 1, hg, T), grp),
        ],
        out_specs=pl.BlockSpec((1, T, hg * HEAD_DIM), tok),
        out_shape=jax.ShapeDtypeStruct((B, T, A), BF16),
        compiler_params=_params(("parallel", "parallel")),
        name="attn_sample",
    )(q, k, v, cache_kt, cache_vt, clf, lf)


def _mix_kernel(x_ref, o_ref, mod_ref, g_ref, w5_ref, cw_ref, pu_ref, wo_ref, y_ref, nu_ref, carry_scr, h_scr):
    bb, tt, d = x_ref.shape
    tm = bb * tt

    @pl.when(pl.program_id(1) == 0)
    def _():
        carry_scr[...] = pu_ref[...]

    gate = mod_ref[:, MOD_MIX + 2:MOD_MIX + 3, :]
    _norm_tile(x_ref, mod_ref, g_ref, MOD_MIX, h_scr)
    h = h_scr[...]
    cdim = w5_ref.shape[0] // 5
    proj = lambda i: _dot_t(h, w5_ref[i * cdim:(i + 1) * cdim, :])

    u2 = proj(1) * proj(2)
    u = u2.reshape(bb, tt, cdim)
    tpos = lax.broadcasted_iota(jnp.int32, (1, tt, 1), 1)
    c0 = carry_scr[:, 0:1, :]
    c1 = carry_scr[:, 1:2, :]
    um1 = jnp.where(tpos == 0, c1, pltpu.roll(u2, 1, axis=0).reshape(bb, tt, cdim))
    um2 = jnp.where(tpos == 0, c0, jnp.where(tpos == 1, c1, pltpu.roll(u2, 2, axis=0).reshape(bb, tt, cdim)))
    conv = cw_ref[0:1, :] * um2 + cw_ref[1:2, :] * um1 + cw_ref[2:3, :] * u
    new_u = u[:, tt - (CONV_WIDTH - 1):, :]
    carry_scr[...] = new_u
    nu_ref[...] = new_u

    o_conv = proj(0) * conv.reshape(tm, cdim)
    m = _sigmoid(proj(3)) * o_ref[...].reshape(tm, cdim).astype(F32) + _sigmoid(proj(4)) * o_conv
    mo = jnp.dot(m.astype(BF16), wo_ref[...], preferred_element_type=F32)
    y_ref[...] = x_ref[...] + gate * mo.reshape(bb, tt, d)


def _mix(x, o_attn, mod, g, w_t, conv_w, past_u, w_out, *, bb, tt):
    B, T, D = x.shape
    C = conv_w.shape[1]
    tok = lambda b, t: (b, t, 0)
    per_b = lambda b, t: (b, 0, 0)
    return pl.pallas_call(
        _mix_kernel,
        grid=(B // bb, T // tt),
        in_specs=[
            pl.BlockSpec((bb, tt, D), tok),
            pl.BlockSpec((bb, tt, C), tok),
            pl.BlockSpec((bb, N_MOD, D), per_b),
            _resident((1, D)),
            _resident_rows(w_t, w_t.shape[0] - 5 * C, 5 * C),
            _resident(conv_w.shape),
            pl.BlockSpec((bb, CONV_WIDTH - 1, C), per_b),
            _resident(w_out.shape),
        ],
        out_specs=[
            pl.BlockSpec((bb, tt, D), tok),
            pl.BlockSpec((bb, CONV_WIDTH - 1, C), per_b),
        ],
        out_shape=[
            jax.ShapeDtypeStruct((B, T, D), F32),
            jax.ShapeDtypeStruct((B, CONV_WIDTH - 1, C), F32),
        ],
        scratch_shapes=[pltpu.VMEM((bb, CONV_WIDTH - 1, C), F32), pltpu.VMEM((bb * tt, D), BF16)],
        compiler_params=_params(("parallel", "arbitrary")),
        name="mix",
    )(x, o_attn, mod, g, w_t, conv_w, past_u, w_out)


def _layer(xp, xs, mod_p, mod_s, past_s, past_u_p, weights, g_final):
    (g1, w1a, w1b, gm, w_t, bf, conv_w, w_out, g2, w2a, w2b) = weights
    past_kt, past_vt, past_logf, past_u_s = past_s
    Ts = xs.shape[1]
    tt_p, bb_s = PROMPT_ROW_TILE, SAMPLE_ROW_TILE // Ts
    ffn_tiles = dict(tt_p=tt_p, bb_s=bb_s)
    xp1, xs1 = _ffn(xp, mod_p, xs, mod_s, g1, w1a, w1b, g_final, mod_base=MOD_FFN1, final_norm=False, **ffn_tiles)

    q, kbt, vb, kp, vp, fp = _qkv(xp1, mod_p, gm, w_t, bf, None, bb=1, tt=tt_p)
    op = _attn_prompt(q, kbt, vb, fp, blk=ATTN_BLOCK)
    kp, vp = jnp.swapaxes(kp, 2, 3), jnp.swapaxes(vp, 2, 3)
    q, kb, vb, ks, vs, fs, f_past, f_new = _qkv(xs1, mod_s, gm, w_t, bf, past_logf, bb=bb_s, tt=Ts)
    os_ = _attn_sample(q, kb, vb, past_kt, past_vt, f_past, f_new)

    xp2, up = _mix(xp1, op, mod_p, gm, w_t, conv_w, past_u_p, w_out, bb=1, tt=tt_p)
    xs2, us = _mix(xs1, os_, mod_s, gm, w_t, conv_w, past_u_s, w_out, bb=bb_s, tt=Ts)
    yp, ys = _ffn(xp2, mod_p, xs2, mod_s, g2, w2a, w2b, g_final, mod_base=MOD_FFN2, final_norm=True, **ffn_tiles)
    return (yp, kp, vp, fp, up), (ys, ks, vs, fs, us)


def kernel(x_prompt, x_sample, cache_k, cache_v, cache_logf, state_conv, c_prompt, c_sample, w_ada, b_ada, g_ffn1, w_ffn1_in, w_ffn1_out, g_mix, w_in, b_f, conv_w, w_out, g_ffn2, w_ffn2_in, w_ffn2_out, g_final):
    assert w_ada.shape[0] == 1, "single-layer encoder"
    Bp, Tp, D = x_prompt.shape
    Bs, Ts, _ = x_sample.shape
    A = N_HEADS * HEAD_DIM
    C = conv_w.shape[2]

    mod = _ada(jnp.concatenate([c_prompt, c_sample], axis=0), w_ada[0], b_ada[0])
    mod = mod.reshape(Bp + Bs, N_MOD, D)
    mod_p, mod_s = mod[:Bp], mod[Bp:]

    assert w_in.shape[2] == 3 * A + N_HEADS + 5 * C
    w_t = jnp.swapaxes(w_in[0], 0, 1).astype(BF16)
    bf = jnp.pad(b_f[0], (0, LANES - N_HEADS)).reshape(1, LANES)
    weights = (g_ffn1, w_ffn1_in[0].astype(BF16), w_ffn1_out[0].astype(BF16), g_mix, w_t, bf,
               conv_w[0], w_out[0].astype(BF16), g_ffn2, w_ffn2_in[0].astype(BF16), w_ffn2_out[0].astype(BF16))
    gfin = g_final.reshape(1, D)

    zero_u = jnp.zeros((Bp, CONV_WIDTH - 1, C), F32)
    past = (jnp.swapaxes(cache_k[0], 2, 3), jnp.swapaxes(cache_v[0], 2, 3), cache_logf[0], state_conv[0])
    (yp, kp, vp, fp, up), (ys, ks, vs, fs, us) = _layer(x_prompt, x_sample, mod_p, mod_s, past, zero_u,
                                                          weights, gfin)
    return (yp, ys, kp[None], vp[None], fp[None], up[None], ks[None], vs[None], fs[None], us[None])
```

```python
import functools

import jax
import jax.numpy as jnp
from jax import lax
from jax.experimental import pallas as pl
from jax.experimental.pallas import tpu as pltpu

F32 = jnp.float32
BF16 = jnp.bfloat16

EPS = 1e-6
N_HEADS = 16
HEAD_DIM = 64
N_MOD = 9
MOD_FFN1, MOD_MIX, MOD_FFN2 = 0, 3, 6
CONV_WIDTH = 3
LANES = 128
HEADS_PER_BLOCK = LANES // HEAD_DIM
V7X_VMEM_LIMIT_BYTES = 60 * 1024 * 1024
ADA_STEPS = 3
FFN_CHUNK = 256
PROMPT_ROW_TILE = 1024
SUB_TILES = 2
SAMPLE_ROW_TILE = 512
ATTN_BLOCK = 256
ATTN_ROWS = 64
ATTN_PAIRS = 4
ATTN_SLOTS = 4
BIAS_ROWS = 16
LOG2E = 1.4426950408889634
Q_SCALE = HEAD_DIM ** -0.5 * LOG2E
SAMPLE_HEAD_GROUP = 16
SAMPLE_STACK = 4


def _params(semantics):
    return pltpu.CompilerParams(dimension_semantics=semantics, vmem_limit_bytes=V7X_VMEM_LIMIT_BYTES)


def _resident(shape):
    zeros = (0,) * len(shape)
    return pl.BlockSpec(shape, lambda *_: zeros, pipeline_mode=pl.Buffered(1))


def _resident_rows(w, row0, nrows):
    assert row0 + nrows <= w.shape[0]
    return pl.BlockSpec((pl.Element(nrows), pl.Element(w.shape[1])), lambda *_: (row0, 0),
                        pipeline_mode=pl.Buffered(1))


def _rms_mod(x, g, scale, shift):
    ms = jnp.mean(x * x, axis=-1, keepdims=True)
    y = x * lax.rsqrt(ms + EPS) * g
    return y * (1.0 + scale) + shift


def _norm_tile(x_ref, mod_ref, g_ref, shift_idx, h_scr):
    bb, tt, d = x_ref.shape
    shift = mod_ref[:, shift_idx:shift_idx + 1, :]
    scale = mod_ref[:, shift_idx + 1:shift_idx + 2, :]
    h_scr[0:bb * tt, :] = _rms_mod(x_ref[...], g_ref[...], scale, shift).reshape(bb * tt, d).astype(BF16)


def _dot_t(x, w_t):
    return lax.dot_general(x, w_t, (((1,), (1,)), ((), ())), preferred_element_type=F32)


def _sigmoid(x):
    return 1.0 / (1.0 + jnp.exp(-x))


def _log_sigmoid(x):
    return jnp.minimum(x, 0.0) - jnp.log1p(jnp.exp(-jnp.abs(x)))


def _cumsum_lanes(x):
    n = x.shape[-1]
    lane = lax.broadcasted_iota(jnp.int32, x.shape, x.ndim - 1)
    step = 1
    while step < n:
        x = x + jnp.where(lane >= step, pltpu.roll(x, step, axis=x.ndim - 1), 0.0)
        step *= 2
    return x


def _ada_kernel(c_ref, w_ref, b_ref, o_ref):
    c = c_ref[...]
    a = (c * _sigmoid(c)).astype(BF16)
    o_ref[...] = jnp.dot(a, w_ref[...].astype(BF16), preferred_element_type=F32) + b_ref[...]


def _ada(c, w_ada, b_ada):
    n, d = c.shape
    cols = w_ada.shape[1]
    tn = cols // ADA_STEPS
    assert cols % ADA_STEPS == 0 and tn % LANES == 0
    return pl.pallas_call(
        _ada_kernel,
        grid=(cols // tn,),
        in_specs=[
            pl.BlockSpec((n, d), lambda j: (0, 0)),
            pl.BlockSpec((d, tn), lambda j: (0, j)),
            pl.BlockSpec((1, tn), lambda j: (0, j)),
        ],
        out_specs=pl.BlockSpec((n, tn), lambda j: (0, j)),
        out_shape=jax.ShapeDtypeStruct((n, cols), F32),
        compiler_params=_params(("parallel",)),
        name="ada_mod",
    )(c, w_ada, b_ada.reshape(1, cols))


def _ffn_tile(x_ref, mod_ref, g_ref, w1_ref, w2_ref, gfin_ref, o_ref, h_scr, *, mod_base, final_norm):
    bb, tt, d = x_ref.shape
    tm = bb * tt
    f = w2_ref.shape[0]
    fc = FFN_CHUNK
    gate = mod_ref[:, mod_base + 2:mod_base + 3, :]
    _norm_tile(x_ref, mod_ref, g_ref, mod_base, h_scr)

    acc = None
    for lo in range(0, f, fc):
        h = h_scr[0:tm, :]
        a = jnp.dot(h, w1_ref[:, lo:lo + fc], preferred_element_type=F32)
        b = jnp.dot(h, w1_ref[:, f + lo:f + lo + fc], preferred_element_type=F32)
        act = (a * _sigmoid(a) * b).astype(BF16)
        part = jnp.dot(act, w2_ref[lo:lo + fc, :], preferred_element_type=F32)
        acc = part if acc is None else acc + part

    y = x_ref[...] + 0.5 * gate * acc.reshape(bb, tt, d)
    if final_norm:
        ms = jnp.mean(y * y, axis=-1, keepdims=True)
        y = y * lax.rsqrt(ms + EPS) * gfin_ref[...]
    o_ref[...] = y


def _ffn_kernel(xp_ref, modp_ref, xs_ref, mods_ref, g_ref, w1_ref, w2_ref, gfin_ref, op_ref, os_ref, h_scr, *,
                n_prompt, mod_base, final_norm):
    step = pl.program_id(0)
    tile = functools.partial(_ffn_tile, mod_base=mod_base, final_norm=final_norm)

    @pl.when(step < n_prompt)
    def _():
        tile(xp_ref, modp_ref, g_ref, w1_ref, w2_ref, gfin_ref, op_ref, h_scr)

    @pl.when(step >= n_prompt)
    def _():
        tile(xs_ref, mods_ref, g_ref, w1_ref, w2_ref, gfin_ref, os_ref, h_scr)


def _ffn(xp, modp, xs, mods, g, w1c, w2c, g_final, *, tt_p, bb_s, mod_base, final_norm):
    Bp, Tp, D = xp.shape
    Bs, Ts, _ = xs.shape
    assert w2c.shape[0] % FFN_CHUNK == 0 and Tp % tt_p == 0 and Bs % bb_s == 0
    per_b = Tp // tt_p
    n_prompt = Bp * per_b
    n_sample = Bs // bb_s
    p_tile = lambda i: jnp.minimum(i, n_prompt - 1)
    s_tile = lambda i: jnp.maximum(i - n_prompt, 0)
    kern = functools.partial(_ffn_kernel, n_prompt=n_prompt, mod_base=mod_base, final_norm=final_norm)
    return pl.pallas_call(
        kern,
        grid=(n_prompt + n_sample,),
        in_specs=[
            pl.BlockSpec((1, tt_p, D), lambda i: (p_tile(i) // per_b, p_tile(i) % per_b, 0)),
            pl.BlockSpec((1, N_MOD, D), lambda i: (p_tile(i) // per_b, 0, 0)),
            pl.BlockSpec((bb_s, Ts, D), lambda i: (s_tile(i), 0, 0)),
            pl.BlockSpec((bb_s, N_MOD, D), lambda i: (s_tile(i), 0, 0)),
            _resident((1, D)),
            _resident(w1c.shape),
            _resident(w2c.shape),
            _resident((1, D)),
        ],
        out_specs=[
            pl.BlockSpec((1, tt_p, D), lambda i: (p_tile(i) // per_b, p_tile(i) % per_b, 0)),
            pl.BlockSpec((bb_s, Ts, D), lambda i: (s_tile(i), 0, 0)),
        ],
        out_shape=[jax.ShapeDtypeStruct(xp.shape, F32), jax.ShapeDtypeStruct(xs.shape, F32)],
        scratch_shapes=[pltpu.VMEM((max(tt_p, bb_s * Ts), D), BF16)],
        compiler_params=_params(("arbitrary",)),
        name="ffn_final" if final_norm else "ffn",
    )(xp, modp, xs, mods, g, w1c, w2c, g_final)


def _qkv_kernel(*refs, prompt):
    if not prompt:
        _qkv_tile(*refs, prompt=False)
        return
    (x_ref, mod_ref, g_ref, wqkv_ref, bf_ref, q_ref, kb_ref, vb_ref, k32_ref, v32_ref, lf_ref, h_scr) = refs
    rows = x_ref.shape[1] // SUB_TILES
    for s in range(SUB_TILES):
        ts = slice(s * rows, (s + 1) * rows)
        _qkv_tile(x_ref.at[:, ts, :], mod_ref, g_ref, wqkv_ref, bf_ref,
                  q_ref.at[:, ts, :], kb_ref.at[:, :, ts], vb_ref.at[:, ts, :], k32_ref.at[:, :, :, ts],
                  v32_ref.at[:, :, :, ts], lf_ref.at[:, :, ts], h_scr.at[ts, :], prompt=True)


def _qkv_tile(*refs, prompt):
    if prompt:
        (x_ref, mod_ref, g_ref, wqkv_ref, bf_ref,
         q_ref, kb_ref, vb_ref, k32_ref, v32_ref, lf_ref, h_scr) = refs
    else:
        (x_ref, mod_ref, g_ref, wqkv_ref, bf_ref, clf_ref,
         q_ref, kb_ref, vb_ref, k32_ref, v32_ref, lf_ref, fp_ref, fn_ref, h_scr) = refs
    bb, tt, d = x_ref.shape
    tm = bb * tt
    a_dim = (wqkv_ref.shape[0] - LANES) // 3
    _norm_tile(x_ref, mod_ref, g_ref, MOD_MIX, h_scr)
    h = h_scr[...]

    zq =_dot_t(h, wqkv_ref[0:a_dim, :])
    q_ref[...] = (zq * Q_SCALE).astype(BF16).reshape(bb, tt, a_dim)

    for w_idx, (lo_ref, hi_ref) in ((1, (kb_ref, k32_ref)), (2, (vb_ref, v32_ref))):
        z = _dot_t(h, wqkv_ref[w_idx * a_dim:(w_idx + 1) * a_dim, :])
        if prompt:
            zt = z.T
            hi_ref[0] = zt.reshape(N_HEADS, HEAD_DIM, tt)
            if w_idx == 1:
                lo_ref[0] = zt.astype(BF16)
            else:
                lo_ref[...] = z.astype(BF16).reshape(bb, tt, a_dim)
        else:
            lo_ref[...] = z.astype(BF16).reshape(bb, tt, a_dim)
            for hd in range(N_HEADS):
                hi_ref[:, hd, :, :] = z[:, hd * HEAD_DIM:(hd + 1) * HEAD_DIM].reshape(bb, tt, HEAD_DIM)

    zf = _dot_t(h, wqkv_ref[3 * a_dim:, :])
    lf = _log_sigmoid(zf + bf_ref[...]).T
    if not prompt:
        p_len = clf_ref.shape[2]
        f_past = LOG2E * _cumsum_lanes(clf_ref[...].reshape(bb * N_HEADS, p_len))
        fp_ref[...] = f_past.reshape(bb, N_HEADS, p_len)
    for b in range(bb):
        lf_b = lf[:N_HEADS, b * tt:(b + 1) * tt]
        lf_ref[b] = lf_b
        if not prompt:
            fn_ref[b] = f_past[b * N_HEADS:(b + 1) * N_HEADS, p_len - 1:p_len] + LOG2E * _cumsum_lanes(lf_b)


def _qkv(x, mod, g, w_t, bf, cache_logf, *, bb, tt):
    B, T, D = x.shape
    A = N_HEADS * HEAD_DIM
    prompt = cache_logf is None
    tok = lambda b, t: (b, t, 0)
    per_b = lambda b, t: (b, 0, 0)
    in_specs = [
        pl.BlockSpec((bb, tt, D), tok),
        pl.BlockSpec((bb, N_MOD, D), per_b),
        _resident((1, D)),
        _resident_rows(w_t, 0, 3 * A + LANES),
        _resident(bf.shape),
    ]
    tok_spec = pl.BlockSpec((bb, tt, A), tok)
    tok_shape = jax.ShapeDtypeStruct((B, T, A), BF16)
    lf_spec = pl.BlockSpec((bb, N_HEADS, tt), lambda b, t: (b, 0, t))
    lf_shape = jax.ShapeDtypeStruct((B, N_HEADS, T), F32)
    if prompt:
        assert bb == 1
        kv_spec = pl.BlockSpec((1, N_HEADS, HEAD_DIM, tt), lambda b, t: (b, 0, 0, t))
        kv_shape = jax.ShapeDtypeStruct((B, N_HEADS, HEAD_DIM, T), F32)
        out_specs = [tok_spec, pl.BlockSpec((1, A, tt), lambda b, t: (b, 0, t)), tok_spec, kv_spec, kv_spec, lf_spec]
        out_shape = [tok_shape, jax.ShapeDtypeStruct((B, A, T), BF16), tok_shape, kv_shape, kv_shape, lf_shape]
        args = (x, mod, g, w_t, bf)
    else:
        assert T == tt
        P = cache_logf.shape[2]
        kv_spec = pl.BlockSpec((bb, N_HEADS, tt, HEAD_DIM), lambda b, t: (b, 0, t, 0))
        kv_shape = jax.ShapeDtypeStruct((B, N_HEADS, T, HEAD_DIM), F32)
        in_specs.append(pl.BlockSpec((bb, N_HEADS, P), per_b))
        out_specs = [tok_spec, tok_spec, tok_spec, kv_spec, kv_spec, lf_spec,
                     pl.BlockSpec((bb, N_HEADS, P), per_b), lf_spec]
        out_shape = [tok_shape, tok_shape, tok_shape, kv_shape, kv_shape, lf_shape,
                     jax.ShapeDtypeStruct((B, N_HEADS, P), F32), lf_shape]
        args = (x, mod, g, w_t, bf, cache_logf)
    return pl.pallas_call(
        functools.partial(_qkv_kernel, prompt=prompt),
        grid=(B // bb, T // tt),
        in_specs=in_specs,
        out_specs=out_specs,
        out_shape=out_shape,
        scratch_shapes=[pltpu.VMEM((bb * tt, D), BF16)],
        compiler_params=_params(("parallel", "parallel")),
        name="qkv_proj",
    )(*args)


def _attn_rows(q2, k1_ref, v1_ref, s_scr, p_scr, nk, blk):
    rows = HEADS_PER_BLOCK * blk
    half = blk // 2
    nkeys = nk * blk
    keep = lax.broadcasted_iota(jnp.int32, (blk, blk), 1) <= lax.broadcasted_iota(jnp.int32, (blk, blk), 0)
    s = jnp.dot(q2, k1_ref[:, 0:nkeys], preferred_element_type=F32)
    for h in range(HEADS_PER_BLOCK):
        hs = slice(h * blk, (h + 1) * blk)
        if nk > 1:
            s_scr[hs, 0:nkeys - blk] = s[hs, 0:nkeys - blk]
        s_scr[hs, nkeys - blk:nkeys] = jnp.where(keep, s[hs, nkeys - blk:nkeys], -jnp.inf)

    for r0 in range(0, rows, ATTN_ROWS):
        rs = slice(r0, r0 + ATTN_ROWS)
        mx = None
        for kb in range(nk):
            t = s_scr[rs, kb * blk:(kb + 1) * blk]
            t = jnp.maximum(t[:, :half], t[:, half:])
            mx = t if mx is None else jnp.maximum(mx, t)
        mb = jnp.broadcast_to(jnp.max(mx, axis=-1, keepdims=True), (ATTN_ROWS, blk))
        for kb in range(nk):
            ks = slice(kb * blk, (kb + 1) * blk)
            p_scr[rs, ks] = jnp.exp2(s_scr[rs, ks] - mb).astype(BF16)

    acc = jnp.dot(p_scr[:, 0:nkeys], v1_ref[0:nkeys, :], preferred_element_type=F32)
    return acc[:, :LANES] * (1.0 / acc[:, LANES:])


def _attn_prompt_kernel(q_ref, kt_ref, v_ref, lf_ref, o_ref, k1_scr, v1_scr, s_scr, p_scr, *, blk):
    T = v_ref.shape[1]
    negf = -LOG2E * _cumsum_lanes(lf_ref[0, 0])
    hi = negf.astype(BF16).astype(F32)
    mid = (negf - hi).astype(BF16).astype(F32)
    lo = negf - hi - mid
    pieces = (hi, mid, lo)
    brow = lax.broadcasted_iota(jnp.int32, (BIAS_ROWS, T), 0)
    for pr in range(ATTN_PAIRS):
        v1_scr[pr, :, :LANES] = v_ref[0, :, pr * LANES:(pr + 1) * LANES]
        v1_scr[pr, :, LANES:] = jnp.ones((T, LANES), v1_scr.dtype)
        k1_scr[pr, 0:LANES, :] = kt_ref[0, pr * LANES:(pr + 1) * LANES, :]
        bias = jnp.zeros((BIAS_ROWS, T), F32)
        for h in range(HEADS_PER_BLOCK):
            hd = pr * HEADS_PER_BLOCK + h
            for j, piece in enumerate(pieces):
                bias = jnp.where(brow == len(pieces) * h + j, piece[hd:hd + 1, :], bias)
        k1_scr[pr, LANES:LANES + BIAS_ROWS, :] = bias.astype(BF16)
        k1_scr[pr, LANES + BIAS_ROWS:, :] = jnp.zeros((LANES - BIAS_ROWS, T), BF16)
    lane = lax.broadcasted_iota(jnp.int32, (blk, LANES), 1)
    sel = jnp.concatenate([jnp.where((lane >= len(pieces) * h) & (lane < len(pieces) * (h + 1)), 1.0, 0.0)
                           for h in range(HEADS_PER_BLOCK)], axis=0).astype(BF16)
    nq = T // blk
    order = list(range(0, nq, 2)) + list(range(nq - 1 - nq % 2, 0, -2))
    slot = 0
    for c in order:
        for pr in range(ATTN_PAIRS):
            lanes = slice(pr * LANES, (pr + 1) * LANES)
            q = q_ref[0, c * blk:(c + 1) * blk, lanes]
            zero = jnp.zeros_like(q)
            q2 = jnp.concatenate([jnp.where(lane < HEAD_DIM, q, zero), jnp.where(lane >= HEAD_DIM, q, zero)],
                                 axis=0)
            q2 = jnp.concatenate([q2, sel], axis=1)
            o = _attn_rows(q2, k1_scr.at[pr], v1_scr.at[pr],
                           s_scr.at[slot % ATTN_SLOTS], p_scr.at[slot % ATTN_SLOTS], c + 1, blk)
            o_ref[0, c * blk:(c + 1) * blk, lanes] = jnp.where(lane < HEAD_DIM, o[:blk], o[blk:]).astype(o_ref.dtype)
            slot += 1


def _attn_prompt(q, kt, v, logf, *, blk):
    B, T, A = q.shape
    width = ATTN_PAIRS * LANES
    ngrp = A // width
    heads = ATTN_PAIRS * HEADS_PER_BLOCK
    lf = logf.reshape(B, ngrp, heads, T)
    kern = functools.partial(_attn_prompt_kernel, blk=blk)
    grp = pl.BlockSpec((1, T, width), lambda b, p: (b, 0, p))
    grp_t = pl.BlockSpec((1, width, T), lambda b, p: (b, p, 0))
    return pl.pallas_call(
        kern,
        grid=(B, ngrp),
        in_specs=[grp, grp_t, grp, pl.BlockSpec((1, 1, heads, T), lambda b, p: (b, p, 0, 0))],
        out_specs=grp,
        out_shape=jax.ShapeDtypeStruct((B, T, A), BF16),
        scratch_shapes=[
            pltpu.VMEM((ATTN_PAIRS, 2 * LANES, T), BF16),
            pltpu.VMEM((ATTN_PAIRS, T, 2 * LANES), BF16),
            pltpu.VMEM((ATTN_SLOTS, HEADS_PER_BLOCK * blk, T), F32),
            pltpu.VMEM((ATTN_SLOTS, HEADS_PER_BLOCK * blk, T), BF16),
        ],
        compiler_params=_params(("parallel", "parallel")),
        name="attn_prompt",
    )(q, kt, v, lf)


def _attn_sample_kernel(q_ref, kn_ref, vn_ref, ckt_ref, cvt_ref, fp_ref, fn_ref, o_ref):
    tt = q_ref.shape[1]
    p_len = ckt_ref.shape[3]
    width = SAMPLE_STACK * HEAD_DIM
    nt = (((1,), (1,)), ((), ()))
    lane_head = lax.broadcasted_iota(jnp.int32, (tt, width), 1) // HEAD_DIM
    keep = lax.broadcasted_iota(jnp.int32, (tt, tt), 1) <= lax.broadcasted_iota(jnp.int32, (tt, tt), 0)
    for g in range(ckt_ref.shape[1] // SAMPLE_STACK):
        heads = range(g * SAMPLE_STACK, (g + 1) * SAMPLE_STACK)
        lanes = slice(g * width, (g + 1) * width)
        q = q_ref[0][:, lanes]
        zero = jnp.zeros_like(q)
        qs = jnp.concatenate([jnp.where(lane_head == i, q, zero) for i in range(SAMPLE_STACK)], axis=0)
        kt = ckt_ref[0, heads.start:heads.stop].reshape(width, p_len).astype(BF16)
        vt = cvt_ref[0, heads.start:heads.stop].reshape(width, p_len).astype(BF16)
        s_p = jnp.dot(qs, kt, preferred_element_type=F32)
        s_n = lax.dot_general(qs, kn_ref[0][:, lanes], nt, preferred_element_type=F32)
        s_p = jnp.concatenate([s_p[i * tt:(i + 1) * tt] - fp_ref[0, 0, hd:hd + 1, :]
                               for i, hd in enumerate(heads)], axis=0)
        s_n = jnp.concatenate([jnp.where(keep, s_n[i * tt:(i + 1) * tt] - fn_ref[0, 0, hd:hd + 1, :], -jnp.inf)
                               for i, hd in enumerate(heads)], axis=0)
        m = jnp.maximum(jnp.max(s_p, axis=-1, keepdims=True), jnp.max(s_n, axis=-1, keepdims=True))
        p_p = jnp.exp2(s_p - m)
        p_n = jnp.exp2(s_n - m)
        l = jnp.sum(p_p, axis=-1, keepdims=True) + jnp.sum(p_n, axis=-1, keepdims=True)
        o = (lax.dot_general(p_p.astype(BF16), vt, nt, preferred_element_type=F32)
             + jnp.dot(p_n.astype(BF16), vn_ref[0][:, lanes], preferred_element_type=F32)) * (1.0 / l)
        out = jnp.zeros((tt, width), F32)
        for i in range(SAMPLE_STACK):
            out = jnp.where(lane_head == i, o[i * tt:(i + 1) * tt], out)
        o_ref[0, :, lanes] = out.astype(o_ref.dtype)


def _attn_sample(q, k, v, cache_kt, cache_vt, f_past, f_new):
    B, T, A = q.shape
    P = cache_kt.shape[3]
    hg = SAMPLE_HEAD_GROUP
    ngrp = N_HEADS // hg
    clf = f_past.reshape(B, ngrp, hg, P)
    lf = f_new.reshape(B, ngrp, hg, T)
    tok = lambda b, p: (b, 0, p)
    grp = lambda b, p: (b, p, 0, 0)
    return pl.pallas_call(
        _attn_sample_kernel,
        grid=(B, ngrp),
        in_specs=[
            pl.BlockSpec((1, T, hg * HEAD_DIM), tok),
            pl.BlockSpec((1, T, hg * HEAD_DIM), tok),
            pl.BlockSpec((1, T, hg * HEAD_DIM), tok),
            pl.BlockSpec((1, hg, HEAD_DIM, P), grp),
            pl.BlockSpec((1, hg, HEAD_DIM, P), grp),
            pl.BlockSpec((1, 1, hg, P), grp),
            pl.BlockSpec((1, 1, hg, T), grp),
        ],
        out_specs=pl.BlockSpec((1, T, hg * HEAD_DIM), tok),
        out_shape=jax.ShapeDtypeStruct((B, T, A), BF16),
        compiler_params=_params(("parallel", "parallel")),
        name="attn_sample",
    )(q, k, v, cache_kt, cache_vt, clf, lf)


def _mix_kernel(xp_ref, op_ref, modp_ref, pup_ref, xs_ref, os_ref, mods_ref, pus_ref, g_ref, w5_ref, cw_ref, wo_ref,
                yp_ref, nup_ref, ys_ref, nus_ref, carry_scr, h_scr, *, n_prompt, per_b):
    step = pl.program_id(0)

    @pl.when(step < n_prompt)
    def _():
        @pl.when(step % per_b == 0)
        def _():
            carry_scr[...] = pup_ref[...]

        _mix_tile(xp_ref, op_ref, modp_ref, g_ref, w5_ref, cw_ref, wo_ref, carry_scr, yp_ref, nup_ref, carry_scr,
                  h_scr)

    @pl.when(step >= n_prompt)
    def _():
        _mix_tile(xs_ref, os_ref, mods_ref, g_ref, w5_ref, cw_ref, wo_ref, pus_ref, ys_ref, nus_ref, None, h_scr)


def _mix_tile(x_ref, o_ref, mod_ref, g_ref, w5_ref, cw_ref, wo_ref, past_ref, y_ref, nu_ref, carry_out, h_scr):
    bb, tt, d = x_ref.shape
    tm = bb * tt
    gate = mod_ref[:, MOD_MIX + 2:MOD_MIX + 3, :]
    _norm_tile(x_ref, mod_ref, g_ref, MOD_MIX, h_scr)
    h = h_scr[0:tm, :]
    cdim = w5_ref.shape[0] // 5
    proj = lambda i: _dot_t(h, w5_ref[i * cdim:(i + 1) * cdim, :])

    u2 = proj(1) * proj(2)
    u = u2.reshape(bb, tt, cdim)
    tpos = lax.broadcasted_iota(jnp.int32, (1, tt, 1), 1)
    c0 = past_ref[:, 0:1, :]
    c1 = past_ref[:, 1:2, :]
    um1 = jnp.where(tpos == 0, c1, pltpu.roll(u2, 1, axis=0).reshape(bb, tt, cdim))
    um2 = jnp.where(tpos == 0, c0, jnp.where(tpos == 1, c1, pltpu.roll(u2, 2, axis=0).reshape(bb, tt, cdim)))
    conv = cw_ref[0:1, :] * um2 + cw_ref[1:2, :] * um1 + cw_ref[2:3, :] * u
    new_u = u[:, tt - (CONV_WIDTH - 1):, :]
    if carry_out is not None:
        carry_out[...] = new_u
    nu_ref[...] = new_u

    o_conv = proj(0) * conv.reshape(tm, cdim)
    m = _sigmoid(proj(3)) * o_ref[...].reshape(tm, cdim).astype(F32) + _sigmoid(proj(4)) * o_conv
    mo = jnp.dot(m.astype(BF16), wo_ref[...], preferred_element_type=F32)
    y_ref[...] = x_ref[...] + gate * mo.reshape(bb, tt, d)


def _mix(xp, op, modp, pup, xs, os_, mods, pus, g, w_t, conv_w, w_out, *, tt_p, bb_s):
    Bp, Tp, D = xp.shape
    Bs, Ts, _ = xs.shape
    C = conv_w.shape[1]
    per_b = Tp // tt_p
    n_prompt = Bp * per_b
    n_sample = Bs // bb_s
    nrow = CONV_WIDTH - 1
    p_tile = lambda i: jnp.minimum(i, n_prompt - 1)
    s_tile = lambda i: jnp.maximum(i - n_prompt, 0)
    p_tok = lambda i: (p_tile(i) // per_b, p_tile(i) % per_b, 0)
    p_str = lambda i: (p_tile(i) // per_b, 0, 0)
    s_str = lambda i: (s_tile(i), 0, 0)
    return pl.pallas_call(
        functools.partial(_mix_kernel, n_prompt=n_prompt, per_b=per_b),
        grid=(n_prompt + n_sample,),
        in_specs=[
            pl.BlockSpec((1, tt_p, D), p_tok),
            pl.BlockSpec((1, tt_p, C), p_tok),
            pl.BlockSpec((1, N_MOD, D), p_str),
            pl.BlockSpec((1, nrow, C), p_str),
            pl.BlockSpec((bb_s, Ts, D), s_str),
            pl.BlockSpec((bb_s, Ts, C), s_str),
            pl.BlockSpec((bb_s, N_MOD, D), s_str),
            pl.BlockSpec((bb_s, nrow, C), s_str),
            _resident((1, D)),
            _resident_rows(w_t, w_t.shape[0] - 5 * C, 5 * C),
            _resident(conv_w.shape),
            _resident(w_out.shape),
        ],
        out_specs=[
            pl.BlockSpec((1, tt_p, D), p_tok),
            pl.BlockSpec((1, nrow, C), p_str),
            pl.BlockSpec((bb_s, Ts, D), s_str),
            pl.BlockSpec((bb_s, nrow, C), s_str),
        ],
        out_shape=[
            jax.ShapeDtypeStruct(xp.shape, F32),
            jax.ShapeDtypeStruct((Bp, nrow, C), F32),
            jax.ShapeDtypeStruct(xs.shape, F32),
            jax.ShapeDtypeStruct((Bs, nrow, C), F32),
        ],
        scratch_shapes=[pltpu.VMEM((1, nrow, C), F32), pltpu.VMEM((max(tt_p, bb_s * Ts), D), BF16)],
        compiler_params=_params(("arbitrary",)),
        name="mix",
    )(xp, op, modp, pup, xs, os_, mods, pus, g, w_t, conv_w, w_out)


def _layer(xp, xs, mod_p, mod_s, past_s, past_u_p, weights, g_final):
    (g1, w1a, w1b, gm, w_t, bf, conv_w, w_out, g2, w2a, w2b) = weights
    past_kt, past_vt, past_logf, past_u_s = past_s
    Ts = xs.shape[1]
    tt_p, bb_s = PROMPT_ROW_TILE, SAMPLE_ROW_TILE // Ts
    ffn_tiles = dict(tt_p=tt_p, bb_s=bb_s)
    xp1, xs1 = _ffn(xp, mod_p, xs, mod_s, g1, w1a, w1b, g_final, mod_base=MOD_FFN1, final_norm=False, **ffn_tiles)

    q, kbt, vb, kp, vp, fp = _qkv(xp1, mod_p, gm, w_t, bf, None, bb=1, tt=tt_p)
    op = _attn_prompt(q, kbt, vb, fp, blk=ATTN_BLOCK)
    kp, vp = jnp.swapaxes(kp, 2, 3), jnp.swapaxes(vp, 2, 3)
    q, kb, vb, ks, vs, fs, f_past, f_new = _qkv(xs1, mod_s, gm, w_t, bf, past_logf, bb=bb_s, tt=Ts)
    os_ = _attn_sample(q, kb, vb, past_kt, past_vt, f_past, f_new)

    xp2, up, xs2, us = _mix(xp1, op, mod_p, past_u_p, xs1, os_, mod_s, past_u_s, gm, w_t, conv_w, w_out,
                            tt_p=tt_p, bb_s=bb_s)
    yp, ys = _ffn(xp2, mod_p, xs2, mod_s, g2, w2a, w2b, g_final, mod_base=MOD_FFN2, final_norm=True, **ffn_tiles)
    return (yp, kp, vp, fp, up), (ys, ks, vs, fs, us)


def kernel(x_prompt, x_sample, cache_k, cache_v, cache_logf, state_conv, c_prompt, c_sample, w_ada, b_ada, g_ffn1, w_ffn1_in, w_ffn1_out, g_mix, w_in, b_f, conv_w, w_out, g_ffn2, w_ffn2_in, w_ffn2_out, g_final):
    assert w_ada.shape[0] == 1, "single-layer encoder"
    Bp, Tp, D = x_prompt.shape
    Bs, Ts, _ = x_sample.shape
    A = N_HEADS * HEAD_DIM
    C = conv_w.shape[2]

    mod = _ada(jnp.concatenate([c_prompt, c_sample], axis=0), w_ada[0], b_ada[0])
    mod = mod.reshape(Bp + Bs, N_MOD, D)
    mod_p, mod_s = mod[:Bp], mod[Bp:]

    assert w_in.shape[2] == 3 * A + N_HEADS + 5 * C
    w_t = jnp.swapaxes(w_in[0], 0, 1).astype(BF16)
    bf = jnp.pad(b_f[0], (0, LANES - N_HEADS)).reshape(1, LANES)
    weights = (g_ffn1, w_ffn1_in[0].astype(BF16), w_ffn1_out[0].astype(BF16), g_mix, w_t, bf,
               conv_w[0], w_out[0].astype(BF16), g_ffn2, w_ffn2_in[0].astype(BF16), w_ffn2_out[0].astype(BF16))
    gfin = g_final.reshape(1, D)

    zero_u = jnp.zeros((Bp, CONV_WIDTH - 1, C), F32)
    past = (jnp.swapaxes(cache_k[0], 2, 3), jnp.swapaxes(cache_v[0], 2, 3), cache_logf[0], state_conv[0])
    (yp, kp, vp, fp, up), (ys, ks, vs, fs, us) = _layer(x_prompt, x_sample, mod_p, mod_s, past, zero_u,
                                                          weights, gfin)
    return (yp, ys, kp[None], vp[None], fp[None], up[None], ks[None], vs[None], fs[None], us[None])
```

```python
import functools

import jax
import jax.numpy as jnp
from jax import lax
from jax.experimental import pallas as pl
from jax.experimental.pallas import tpu as pltpu

F32 = jnp.float32
BF16 = jnp.bfloat16

EPS = 1e-6
N_HEADS = 16
HEAD_DIM = 64
N_MOD = 9
MOD_FFN1, MOD_MIX, MOD_FFN2 = 0, 3, 6
CONV_WIDTH = 3
LANES = 128
HEADS_PER_BLOCK = LANES // HEAD_DIM
V7X_VMEM_LIMIT_BYTES = 56 * 1024 * 1024
V7X_VMEM_LIMIT_MIX_BYTES = 60 * 1024 * 1024
ADA_STEPS = 3
FFN_CHUNK = 256
PROMPT_ROW_TILE = 1024
SUB_TILES = 2
SAMPLE_ROW_TILE = 512
ATTN_BLOCK = 256
ATTN_ROWS = 64
ATTN_PAIRS = 4
ATTN_SLOTS = 4
BIAS_ROWS = 16
LOG2E = 1.4426950408889634
Q_SCALE = HEAD_DIM ** -0.5 * LOG2E
SAMPLE_HEAD_GROUP = 16
SAMPLE_STACK = 4


def _params(semantics, vmem_limit_bytes=V7X_VMEM_LIMIT_BYTES):
    return pltpu.CompilerParams(dimension_semantics=semantics, vmem_limit_bytes=vmem_limit_bytes)


def _resident(shape):
    zeros = (0,) * len(shape)
    return pl.BlockSpec(shape, lambda *_: zeros, pipeline_mode=pl.Buffered(1))


def _resident_rows(w, row0, nrows):
    assert row0 + nrows <= w.shape[0]
    return pl.BlockSpec((pl.Element(nrows), pl.Element(w.shape[1])), lambda *_: (row0, 0),
                        pipeline_mode=pl.Buffered(1))


def _rms_mod(x, g, scale, shift):
    ms = jnp.mean(x * x, axis=-1, keepdims=True)
    y = x * lax.rsqrt(ms + EPS) * g
    return y * (1.0 + scale) + shift


def _norm_tile(x_ref, mod_ref, g_ref, shift_idx, h_scr):
    bb, tt, d = x_ref.shape
    shift = mod_ref[:, shift_idx:shift_idx + 1, :]
    scale = mod_ref[:, shift_idx + 1:shift_idx + 2, :]
    h_scr[0:bb * tt, :] = _rms_mod(x_ref[...], g_ref[...], scale, shift).reshape(bb * tt, d).astype(BF16)


def _dot_t(x, w_t):
    return lax.dot_general(x, w_t, (((1,), (1,)), ((), ())), preferred_element_type=F32)


def _sigmoid(x):
    return 1.0 / (1.0 + jnp.exp(-x))


def _log_sigmoid(x):
    return jnp.minimum(x, 0.0) - jnp.log1p(jnp.exp(-jnp.abs(x)))


def _cumsum_lanes(x):
    n = x.shape[-1]
    lane = lax.broadcasted_iota(jnp.int32, x.shape, x.ndim - 1)
    step = 1
    while step < n:
        x = x + jnp.where(lane >= step, pltpu.roll(x, step, axis=x.ndim - 1), 0.0)
        step *= 2
    return x


def _ada_kernel(c_ref, w_ref, b_ref, o_ref):
    c = c_ref[...]
    a = (c * _sigmoid(c)).astype(BF16)
    o_ref[...] = jnp.dot(a, w_ref[...].astype(BF16), preferred_element_type=F32) + b_ref[...]


def _ada(c, w_ada, b_ada):
    n, d = c.shape
    cols = w_ada.shape[1]
    tn = cols // ADA_STEPS
    assert cols % ADA_STEPS == 0 and tn % LANES == 0
    return pl.pallas_call(
        _ada_kernel,
        grid=(cols // tn,),
        in_specs=[
            pl.BlockSpec((n, d), lambda j: (0, 0)),
            pl.BlockSpec((d, tn), lambda j: (0, j)),
            pl.BlockSpec((1, tn), lambda j: (0, j)),
        ],
        out_specs=pl.BlockSpec((n, tn), lambda j: (0, j)),
        out_shape=jax.ShapeDtypeStruct((n, cols), F32),
        compiler_params=_params(("parallel",)),
        name="ada_mod",
    )(c, w_ada, b_ada.reshape(1, cols))


def _ffn_tile(x_ref, mod_ref, g_ref, w1_ref, w2_ref, gfin_ref, o_ref, h_scr, *, mod_base, final_norm):
    bb, tt, d = x_ref.shape
    tm = bb * tt
    f = w2_ref.shape[0]
    fc = FFN_CHUNK
    gate = mod_ref[:, mod_base + 2:mod_base + 3, :]
    _norm_tile(x_ref, mod_ref, g_ref, mod_base, h_scr)

    acc = None
    for lo in range(0, f, fc):
        h = h_scr[0:tm, :]
        a = jnp.dot(h, w1_ref[:, lo:lo + fc], preferred_element_type=F32)
        b = jnp.dot(h, w1_ref[:, f + lo:f + lo + fc], preferred_element_type=F32)
        act = (a * _sigmoid(a) * b).astype(BF16)
        part = jnp.dot(act, w2_ref[lo:lo + fc, :], preferred_element_type=F32)
        acc = part if acc is None else acc + part

    y = x_ref[...] + 0.5 * gate * acc.reshape(bb, tt, d)
    if final_norm:
        ms = jnp.mean(y * y, axis=-1, keepdims=True)
        y = y * lax.rsqrt(ms + EPS) * gfin_ref[...]
    o_ref[...] = y


def _ffn_kernel(xp_ref, modp_ref, xs_ref, mods_ref, g_ref, w1_ref, w2_ref, gfin_ref, op_ref, os_ref, h_scr, *,
                n_prompt, mod_base, final_norm):
    step = pl.program_id(0)
    tile = functools.partial(_ffn_tile, mod_base=mod_base, final_norm=final_norm)

    @pl.when(step < n_prompt)
    def _():
        tile(xp_ref, modp_ref, g_ref, w1_ref, w2_ref, gfin_ref, op_ref, h_scr)

    @pl.when(step >= n_prompt)
    def _():
        tile(xs_ref, mods_ref, g_ref, w1_ref, w2_ref, gfin_ref, os_ref, h_scr)


def _ffn(xp, modp, xs, mods, g, w1c, w2c, g_final, *, tt_p, bb_s, mod_base, final_norm):
    Bp, Tp, D = xp.shape
    Bs, Ts, _ = xs.shape
    assert w2c.shape[0] % FFN_CHUNK == 0 and Tp % tt_p == 0 and Bs % bb_s == 0
    per_b = Tp // tt_p
    n_prompt = Bp * per_b
    n_sample = Bs // bb_s
    p_tile = lambda i: jnp.minimum(i, n_prompt - 1)
    s_tile = lambda i: jnp.maximum(i - n_prompt, 0)
    kern = functools.partial(_ffn_kernel, n_prompt=n_prompt, mod_base=mod_base, final_norm=final_norm)
    return pl.pallas_call(
        kern,
        grid=(n_prompt + n_sample,),
        in_specs=[
            pl.BlockSpec((1, tt_p, D), lambda i: (p_tile(i) // per_b, p_tile(i) % per_b, 0)),
            pl.BlockSpec((1, N_MOD, D), lambda i: (p_tile(i) // per_b, 0, 0)),
            pl.BlockSpec((bb_s, Ts, D), lambda i: (s_tile(i), 0, 0)),
            pl.BlockSpec((bb_s, N_MOD, D), lambda i: (s_tile(i), 0, 0)),
            _resident((1, D)),
            _resident(w1c.shape),
            _resident(w2c.shape),
            _resident((1, D)),
        ],
        out_specs=[
            pl.BlockSpec((1, tt_p, D), lambda i: (p_tile(i) // per_b, p_tile(i) % per_b, 0)),
            pl.BlockSpec((bb_s, Ts, D), lambda i: (s_tile(i), 0, 0)),
        ],
        out_shape=[jax.ShapeDtypeStruct(xp.shape, F32), jax.ShapeDtypeStruct(xs.shape, F32)],
        scratch_shapes=[pltpu.VMEM((max(tt_p, bb_s * Ts), D), BF16)],
        compiler_params=_params(("arbitrary",)),
        name="ffn_final" if final_norm else "ffn",
    )(xp, modp, xs, mods, g, w1c, w2c, g_final)


def _qkv_kernel(*refs, prompt):
    if not prompt:
        _qkv_tile(*refs, prompt=False)
        return
    (x_ref, mod_ref, g_ref, wqkv_ref, bf_ref, q_ref, kb_ref, vb_ref, k32_ref, v32_ref, lf_ref, h_scr) = refs
    rows = x_ref.shape[1] // SUB_TILES
    for s in range(SUB_TILES):
        ts = slice(s * rows, (s + 1) * rows)
        _qkv_tile(x_ref.at[:, ts, :], mod_ref, g_ref, wqkv_ref, bf_ref,
                  q_ref.at[:, ts, :], kb_ref.at[:, :, ts], vb_ref.at[:, ts, :], k32_ref.at[:, :, :, ts],
                  v32_ref.at[:, :, :, ts], lf_ref.at[:, :, ts], h_scr.at[ts, :], prompt=True)


def _qkv_tile(*refs, prompt):
    if prompt:
        (x_ref, mod_ref, g_ref, wqkv_ref, bf_ref,
         q_ref, kb_ref, vb_ref, k32_ref, v32_ref, lf_ref, h_scr) = refs
    else:
        (x_ref, mod_ref, g_ref, wqkv_ref, bf_ref, clf_ref,
         q_ref, kb_ref, vb_ref, k32_ref, v32_ref, lf_ref, fp_ref, fn_ref, h_scr) = refs
    bb, tt, d = x_ref.shape
    tm = bb * tt
    a_dim = (wqkv_ref.shape[0] - LANES) // 3
    _norm_tile(x_ref, mod_ref, g_ref, MOD_MIX, h_scr)
    h = h_scr[...]

    zq =_dot_t(h, wqkv_ref[0:a_dim, :])
    q_ref[...] = (zq * Q_SCALE).astype(BF16).reshape(bb, tt, a_dim)

    for w_idx, (lo_ref, hi_ref) in ((1, (kb_ref, k32_ref)), (2, (vb_ref, v32_ref))):
        z = _dot_t(h, wqkv_ref[w_idx * a_dim:(w_idx + 1) * a_dim, :])
        if prompt:
            zt = z.T
            hi_ref[0] = zt.reshape(N_HEADS, HEAD_DIM, tt)
            if w_idx == 1:
                lo_ref[0] = zt.astype(BF16)
            else:
                lo_ref[...] = z.astype(BF16).reshape(bb, tt, a_dim)
        else:
            lo_ref[...] = z.astype(BF16).reshape(bb, tt, a_dim)
            for hd in range(N_HEADS):
                hi_ref[:, hd, :, :] = z[:, hd * HEAD_DIM:(hd + 1) * HEAD_DIM].reshape(bb, tt, HEAD_DIM)

    zf = _dot_t(h, wqkv_ref[3 * a_dim:, :])
    lf = _log_sigmoid(zf + bf_ref[...]).T
    if not prompt:
        p_len = clf_ref.shape[2]
        f_past = LOG2E * _cumsum_lanes(clf_ref[...].reshape(bb * N_HEADS, p_len))
        fp_ref[...] = f_past.reshape(bb, N_HEADS, p_len)
    for b in range(bb):
        lf_b = lf[:N_HEADS, b * tt:(b + 1) * tt]
        lf_ref[b] = lf_b
        if not prompt:
            fn_ref[b] = f_past[b * N_HEADS:(b + 1) * N_HEADS, p_len - 1:p_len] + LOG2E * _cumsum_lanes(lf_b)


def _qkv(x, mod, g, w_t, bf, cache_logf, *, bb, tt):
    B, T, D = x.shape
    A = N_HEADS * HEAD_DIM
    prompt = cache_logf is None
    tok = lambda b, t: (b, t, 0)
    per_b = lambda b, t: (b, 0, 0)
    in_specs = [
        pl.BlockSpec((bb, tt, D), tok),
        pl.BlockSpec((bb, N_MOD, D), per_b),
        _resident((1, D)),
        _resident_rows(w_t, 0, 3 * A + LANES),
        _resident(bf.shape),
    ]
    tok_spec = pl.BlockSpec((bb, tt, A), tok)
    tok_shape = jax.ShapeDtypeStruct((B, T, A), BF16)
    lf_spec = pl.BlockSpec((bb, N_HEADS, tt), lambda b, t: (b, 0, t))
    lf_shape = jax.ShapeDtypeStruct((B, N_HEADS, T), F32)
    if prompt:
        assert bb == 1
        kv_spec = pl.BlockSpec((1, N_HEADS, HEAD_DIM, tt), lambda b, t: (b, 0, 0, t))
        kv_shape = jax.ShapeDtypeStruct((B, N_HEADS, HEAD_DIM, T), F32)
        out_specs = [tok_spec, pl.BlockSpec((1, A, tt), lambda b, t: (b, 0, t)), tok_spec, kv_spec, kv_spec, lf_spec]
        out_shape = [tok_shape, jax.ShapeDtypeStruct((B, A, T), BF16), tok_shape, kv_shape, kv_shape, lf_shape]
        args = (x, mod, g, w_t, bf)
    else:
        assert T == tt
        P = cache_logf.shape[2]
        kv_spec = pl.BlockSpec((bb, N_HEADS, tt, HEAD_DIM), lambda b, t: (b, 0, t, 0))
        kv_shape = jax.ShapeDtypeStruct((B, N_HEADS, T, HEAD_DIM), F32)
        in_specs.append(pl.BlockSpec((bb, N_HEADS, P), per_b))
        out_specs = [tok_spec, tok_spec, tok_spec, kv_spec, kv_spec, lf_spec,
                     pl.BlockSpec((bb, N_HEADS, P), per_b), lf_spec]
        out_shape = [tok_shape, tok_shape, tok_shape, kv_shape, kv_shape, lf_shape,
                     jax.ShapeDtypeStruct((B, N_HEADS, P), F32), lf_shape]
        args = (x, mod, g, w_t, bf, cache_logf)
    return pl.pallas_call(
        functools.partial(_qkv_kernel, prompt=prompt),
        grid=(B // bb, T // tt),
        in_specs=in_specs,
        out_specs=out_specs,
        out_shape=out_shape,
        scratch_shapes=[pltpu.VMEM((bb * tt, D), BF16)],
        compiler_params=_params(("parallel", "parallel")),
        name="qkv_proj",
    )(*args)


def _attn_rows(q2, k1_ref, v1_ref, s_scr, p_scr, nk, blk):
    rows = HEADS_PER_BLOCK * blk
    half = blk // 2
    nkeys = nk * blk
    keep = lax.broadcasted_iota(jnp.int32, (blk, blk), 1) <= lax.broadcasted_iota(jnp.int32, (blk, blk), 0)
    s = jnp.dot(q2, k1_ref[:, 0:nkeys], preferred_element_type=F32)
    for h in range(HEADS_PER_BLOCK):
        hs = slice(h * blk, (h + 1) * blk)
        if nk > 1:
            s_scr[hs, 0:nkeys - blk] = s[hs, 0:nkeys - blk]
        s_scr[hs, nkeys - blk:nkeys] = jnp.where(keep, s[hs, nkeys - blk:nkeys], -jnp.inf)

    for r0 in range(0, rows, ATTN_ROWS):
        rs = slice(r0, r0 + ATTN_ROWS)
        mx = None
        for kb in range(nk):
            t = s_scr[rs, kb * blk:(kb + 1) * blk]
            t = jnp.maximum(t[:, :half], t[:, half:])
            mx = t if mx is None else jnp.maximum(mx, t)
        mb = jnp.broadcast_to(jnp.max(mx, axis=-1, keepdims=True), (ATTN_ROWS, blk))
        for kb in range(nk):
            ks = slice(kb * blk, (kb + 1) * blk)
            p_scr[rs, ks] = jnp.exp2(s_scr[rs, ks] - mb).astype(BF16)

    acc = jnp.dot(p_scr[:, 0:nkeys], v1_ref[0:nkeys, :], preferred_element_type=F32)
    return acc[:, :LANES] * (1.0 / acc[:, LANES:])


def _attn_prompt_kernel(q_ref, kt_ref, v_ref, lf_ref, o_ref, k1_scr, v1_scr, s_scr, p_scr, *, blk):
    T = v_ref.shape[1]
    negf = -LOG2E * _cumsum_lanes(lf_ref[0, 0])
    hi = negf.astype(BF16).astype(F32)
    mid = (negf - hi).astype(BF16).astype(F32)
    lo = negf - hi - mid
    pieces = (hi, mid, lo)
    brow = lax.broadcasted_iota(jnp.int32, (BIAS_ROWS, T), 0)
    for pr in range(ATTN_PAIRS):
        v1_scr[pr, :, :LANES] = v_ref[0, :, pr * LANES:(pr + 1) * LANES]
        v1_scr[pr, :, LANES:] = jnp.ones((T, LANES), v1_scr.dtype)
        k1_scr[pr, 0:LANES, :] = kt_ref[0, pr * LANES:(pr + 1) * LANES, :]
        bias = jnp.zeros((BIAS_ROWS, T), F32)
        for h in range(HEADS_PER_BLOCK):
            hd = pr * HEADS_PER_BLOCK + h
            for j, piece in enumerate(pieces):
                bias = jnp.where(brow == len(pieces) * h + j, piece[hd:hd + 1, :], bias)
        k1_scr[pr, LANES:LANES + BIAS_ROWS, :] = bias.astype(BF16)
        k1_scr[pr, LANES + BIAS_ROWS:, :] = jnp.zeros((LANES - BIAS_ROWS, T), BF16)
    lane = lax.broadcasted_iota(jnp.int32, (blk, LANES), 1)
    sel = jnp.concatenate([jnp.where((lane >= len(pieces) * h) & (lane < len(pieces) * (h + 1)), 1.0, 0.0)
                           for h in range(HEADS_PER_BLOCK)], axis=0).astype(BF16)
    nq = T // blk
    order = list(range(0, nq, 2)) + list(range(nq - 1 - nq % 2, 0, -2))
    slot = 0
    for c in order:
        for pr in range(ATTN_PAIRS):
            lanes = slice(pr * LANES, (pr + 1) * LANES)
            q = q_ref[0, c * blk:(c + 1) * blk, lanes]
            zero = jnp.zeros_like(q)
            q2 = jnp.concatenate([jnp.where(lane < HEAD_DIM, q, zero), jnp.where(lane >= HEAD_DIM, q, zero)],
                                 axis=0)
            q2 = jnp.concatenate([q2, sel], axis=1)
            o = _attn_rows(q2, k1_scr.at[pr], v1_scr.at[pr],
                           s_scr.at[slot % ATTN_SLOTS], p_scr.at[slot % ATTN_SLOTS], c + 1, blk)
            o_ref[0, c * blk:(c + 1) * blk, lanes] = jnp.where(lane < HEAD_DIM, o[:blk], o[blk:]).astype(o_ref.dtype)
            slot += 1


def _attn_prompt(q, kt, v, logf, *, blk):
    B, T, A = q.shape
    width = ATTN_PAIRS * LANES
    ngrp = A // width
    heads = ATTN_PAIRS * HEADS_PER_BLOCK
    lf = logf.reshape(B, ngrp, heads, T)
    kern = functools.partial(_attn_prompt_kernel, blk=blk)
    grp = pl.BlockSpec((1, T, width), lambda b, p: (b, 0, p))
    grp_t = pl.BlockSpec((1, width, T), lambda b, p: (b, p, 0))
    return pl.pallas_call(
        kern,
        grid=(B, ngrp),
        in_specs=[grp, grp_t, grp, pl.BlockSpec((1, 1, heads, T), lambda b, p: (b, p, 0, 0))],
        out_specs=grp,
        out_shape=jax.ShapeDtypeStruct((B, T, A), BF16),
        scratch_shapes=[
            pltpu.VMEM((ATTN_PAIRS, 2 * LANES, T), BF16),
            pltpu.VMEM((ATTN_PAIRS, T, 2 * LANES), BF16),
            pltpu.VMEM((ATTN_SLOTS, HEADS_PER_BLOCK * blk, T), F32),
            pltpu.VMEM((ATTN_SLOTS, HEADS_PER_BLOCK * blk, T), BF16),
        ],
        compiler_params=_params(("parallel", "parallel")),
        name="attn_prompt",
    )(q, kt, v, lf)


def _attn_sample_kernel(q_ref, kn_ref, vn_ref, ckt_ref, cvt_ref, fp_ref, fn_ref, o_ref):
    tt = q_ref.shape[1]
    p_len = ckt_ref.shape[3]
    width = SAMPLE_STACK * HEAD_DIM
    nt = (((1,), (1,)), ((), ()))
    lane_head = lax.broadcasted_iota(jnp.int32, (tt, width), 1) // HEAD_DIM
    keep = lax.broadcasted_iota(jnp.int32, (tt, tt), 1) <= lax.broadcasted_iota(jnp.int32, (tt, tt), 0)
    for g in range(ckt_ref.shape[1] // SAMPLE_STACK):
        heads = range(g * SAMPLE_STACK, (g + 1) * SAMPLE_STACK)
        lanes = slice(g * width, (g + 1) * width)
        q = q_ref[0][:, lanes]
        zero = jnp.zeros_like(q)
        qs = jnp.concatenate([jnp.where(lane_head == i, q, zero) for i in range(SAMPLE_STACK)], axis=0)
        kt = ckt_ref[0, heads.start:heads.stop].reshape(width, p_len).astype(BF16)
        vt = cvt_ref[0, heads.start:heads.stop].reshape(width, p_len).astype(BF16)
        s_p = jnp.dot(qs, kt, preferred_element_type=F32)
        s_n = lax.dot_general(qs, kn_ref[0][:, lanes], nt, preferred_element_type=F32)
        s_p = jnp.concatenate([s_p[i * tt:(i + 1) * tt] - fp_ref[0, 0, hd:hd + 1, :]
                               for i, hd in enumerate(heads)], axis=0)
        s_n = jnp.concatenate([jnp.where(keep, s_n[i * tt:(i + 1) * tt] - fn_ref[0, 0, hd:hd + 1, :], -jnp.inf)
                               for i, hd in enumerate(heads)], axis=0)
        m = jnp.maximum(jnp.max(s_p, axis=-1, keepdims=True), jnp.max(s_n, axis=-1, keepdims=True))
        p_p = jnp.exp2(s_p - m)
        p_n = jnp.exp2(s_n - m)
        l = jnp.sum(p_p, axis=-1, keepdims=True) + jnp.sum(p_n, axis=-1, keepdims=True)
        o = (lax.dot_general(p_p.astype(BF16), vt, nt, preferred_element_type=F32)
             + jnp.dot(p_n.astype(BF16), vn_ref[0][:, lanes], preferred_element_type=F32)) * (1.0 / l)
        out = jnp.zeros((tt, width), F32)
        for i in range(SAMPLE_STACK):
            out = jnp.where(lane_head == i, o[i * tt:(i + 1) * tt], out)
        o_ref[0, :, lanes] = out.astype(o_ref.dtype)


def _attn_sample(q, k, v, cache_kt, cache_vt, f_past, f_new):
    B, T, A = q.shape
    P = cache_kt.shape[3]
    hg = SAMPLE_HEAD_GROUP
    ngrp = N_HEADS // hg
    clf = f_past.reshape(B, ngrp, hg, P)
    lf = f_new.reshape(B, ngrp, hg, T)
    tok = lambda b, p: (b, 0, p)
    grp = lambda b, p: (b, p, 0, 0)
    return pl.pallas_call(
        _attn_sample_kernel,
        grid=(B, ngrp),
        in_specs=[
            pl.BlockSpec((1, T, hg * HEAD_DIM), tok),
            pl.BlockSpec((1, T, hg * HEAD_DIM), tok),
            pl.BlockSpec((1, T, hg * HEAD_DIM), tok),
            pl.BlockSpec((1, hg, HEAD_DIM, P), grp),
            pl.BlockSpec((1, hg, HEAD_DIM, P), grp),
            pl.BlockSpec((1, 1, hg, P), grp),
            pl.BlockSpec((1, 1, hg, T), grp),
        ],
        out_specs=pl.BlockSpec((1, T, hg * HEAD_DIM), tok),
        out_shape=jax.ShapeDtypeStruct((B, T, A), BF16),
        compiler_params=_params(("parallel", "parallel")),
        name="attn_sample",
    )(q, k, v, cache_kt, cache_vt, clf, lf)


def _mix_kernel(xp_ref, op_ref, modp_ref, pup_ref, xs_ref, os_ref, mods_ref, pus_ref, g_ref, w5_ref, cw_ref, wo_ref,
                yp_ref, nup_ref, ys_ref, nus_ref, carry_scr, h_scr, *, n_prompt, per_b):
    step = pl.program_id(0)

    @pl.when(step < n_prompt)
    def _():
        @pl.when(step % per_b == 0)
        def _():
            carry_scr[...] = pup_ref[...]

        _mix_tile(xp_ref, op_ref, modp_ref, g_ref, w5_ref, cw_ref, wo_ref, carry_scr, yp_ref, nup_ref, carry_scr,
                  h_scr)

    @pl.when(step >= n_prompt)
    def _():
        _mix_tile(xs_ref, os_ref, mods_ref, g_ref, w5_ref, cw_ref, wo_ref, pus_ref, ys_ref, nus_ref, None, h_scr)


def _mix_tile(x_ref, o_ref, mod_ref, g_ref, w5_ref, cw_ref, wo_ref, past_ref, y_ref, nu_ref, carry_out, h_scr):
    bb, tt, d = x_ref.shape
    tm = bb * tt
    gate = mod_ref[:, MOD_MIX + 2:MOD_MIX + 3, :]
    _norm_tile(x_ref, mod_ref, g_ref, MOD_MIX, h_scr)
    h = h_scr[0:tm, :]
    cdim = w5_ref.shape[0] // 5
    proj = lambda i: _dot_t(h, w5_ref[i * cdim:(i + 1) * cdim, :])

    u2 = proj(1) * proj(2)
    u = u2.reshape(bb, tt, cdim)
    tpos = lax.broadcasted_iota(jnp.int32, (1, tt, 1), 1)
    c0 = past_ref[:, 0:1, :]
    c1 = past_ref[:, 1:2, :]
    um1 = jnp.where(tpos == 0, c1, pltpu.roll(u2, 1, axis=0).reshape(bb, tt, cdim))
    um2 = jnp.where(tpos == 0, c0, jnp.where(tpos == 1, c1, pltpu.roll(u2, 2, axis=0).reshape(bb, tt, cdim)))
    conv = cw_ref[0:1, :] * um2 + cw_ref[1:2, :] * um1 + cw_ref[2:3, :] * u
    new_u = u[:, tt - (CONV_WIDTH - 1):, :]
    if carry_out is not None:
        carry_out[...] = new_u
    nu_ref[...] = new_u

    o_conv = proj(0) * conv.reshape(tm, cdim)
    m = _sigmoid(proj(3)) * o_ref[...].reshape(tm, cdim).astype(F32) + _sigmoid(proj(4)) * o_conv
    mo = jnp.dot(m.astype(BF16), wo_ref[...], preferred_element_type=F32)
    y_ref[...] = x_ref[...] + gate * mo.reshape(bb, tt, d)


def _mix(xp, op, modp, pup, xs, os_, mods, pus, g, w_t, conv_w, w_out, *, tt_p, bb_s):
    Bp, Tp, D = xp.shape
    Bs, Ts, _ = xs.shape
    C = conv_w.shape[1]
    per_b = Tp // tt_p
    n_prompt = Bp * per_b
    n_sample = Bs // bb_s
    nrow = CONV_WIDTH - 1
    p_tile = lambda i: jnp.minimum(i, n_prompt - 1)
    s_tile = lambda i: jnp.maximum(i - n_prompt, 0)
    p_tok = lambda i: (p_tile(i) // per_b, p_tile(i) % per_b, 0)
    p_str = lambda i: (p_tile(i) // per_b, 0, 0)
    s_str = lambda i: (s_tile(i), 0, 0)
    return pl.pallas_call(
        functools.partial(_mix_kernel, n_prompt=n_prompt, per_b=per_b),
        grid=(n_prompt + n_sample,),
        in_specs=[
            pl.BlockSpec((1, tt_p, D), p_tok),
            pl.BlockSpec((1, tt_p, C), p_tok),
            pl.BlockSpec((1, N_MOD, D), p_str),
            pl.BlockSpec((1, nrow, C), p_str),
            pl.BlockSpec((bb_s, Ts, D), s_str),
            pl.BlockSpec((bb_s, Ts, C), s_str),
            pl.BlockSpec((bb_s, N_MOD, D), s_str),
            pl.BlockSpec((bb_s, nrow, C), s_str),
            _resident((1, D)),
            _resident_rows(w_t, w_t.shape[0] - 5 * C, 5 * C),
            _resident(conv_w.shape),
            _resident(w_out.shape),
        ],
        out_specs=[
            pl.BlockSpec((1, tt_p, D), p_tok),
            pl.BlockSpec((1, nrow, C), p_str),
            pl.BlockSpec((bb_s, Ts, D), s_str),
            pl.BlockSpec((bb_s, nrow, C), s_str),
        ],
        out_shape=[
            jax.ShapeDtypeStruct(xp.shape, F32),
            jax.ShapeDtypeStruct((Bp, nrow, C), F32),
            jax.ShapeDtypeStruct(xs.shape, F32),
            jax.ShapeDtypeStruct((Bs, nrow, C), F32),
        ],
        scratch_shapes=[pltpu.VMEM((1, nrow, C), F32), pltpu.VMEM((max(tt_p, bb_s * Ts), D), BF16)],
        compiler_params=_params(("arbitrary",), V7X_VMEM_LIMIT_MIX_BYTES),
        name="mix",
    )(xp, op, modp, pup, xs, os_, mods, pus, g, w_t, conv_w, w_out)


def _layer(xp, xs, mod_p, mod_s, past_s, past_u_p, weights, g_final):
    (g1, w1a, w1b, gm, w_t, bf, conv_w, w_out, g2, w2a, w2b) = weights
    past_kt, past_vt, past_logf, past_u_s = past_s
    Ts = xs.shape[1]
    tt_p, bb_s = PROMPT_ROW_TILE, SAMPLE_ROW_TILE // Ts
    ffn_tiles = dict(tt_p=tt_p, bb_s=bb_s)
    xp1, xs1 = _ffn(xp, mod_p, xs, mod_s, g1, w1a, w1b, g_final, mod_base=MOD_FFN1, final_norm=False, **ffn_tiles)

    q, kbt, vb, kp, vp, fp = _qkv(xp1, mod_p, gm, w_t, bf, None, bb=1, tt=tt_p)
    op = _attn_prompt(q, kbt, vb, fp, blk=ATTN_BLOCK)
    kp, vp = jnp.swapaxes(kp, 2, 3), jnp.swapaxes(vp, 2, 3)
    q, kb, vb, ks, vs, fs, f_past, f_new = _qkv(xs1, mod_s, gm, w_t, bf, past_logf, bb=bb_s, tt=Ts)
    os_ = _attn_sample(q, kb, vb, past_kt, past_vt, f_past, f_new)

    xp2, up, xs2, us = _mix(xp1, op, mod_p, past_u_p, xs1, os_, mod_s, past_u_s, gm, w_t, conv_w, w_out,
                            tt_p=tt_p, bb_s=bb_s)
    yp, ys = _ffn(xp2, mod_p, xs2, mod_s, g2, w2a, w2b, g_final, mod_base=MOD_FFN2, final_norm=True, **ffn_tiles)
    return (yp, kp, vp, fp, up), (ys, ks, vs, fs, us)


def kernel(x_prompt, x_sample, cache_k, cache_v, cache_logf, state_conv, c_prompt, c_sample, w_ada, b_ada, g_ffn1, w_ffn1_in, w_ffn1_out, g_mix, w_in, b_f, conv_w, w_out, g_ffn2, w_ffn2_in, w_ffn2_out, g_final):
    assert w_ada.shape[0] == 1, "single-layer encoder"
    Bp, Tp, D = x_prompt.shape
    Bs, Ts, _ = x_sample.shape
    A = N_HEADS * HEAD_DIM
    C = conv_w.shape[2]

    mod = _ada(jnp.concatenate([c_prompt, c_sample], axis=0), w_ada[0], b_ada[0])
    mod = mod.reshape(Bp + Bs, N_MOD, D)
    mod_p, mod_s = mod[:Bp], mod[Bp:]

    assert w_in.shape[2] == 3 * A + N_HEADS + 5 * C
    w_t = jnp.swapaxes(w_in[0], 0, 1).astype(BF16)
    bf = jnp.pad(b_f[0], (0, LANES - N_HEADS)).reshape(1, LANES)
    weights = (g_ffn1, w_ffn1_in[0].astype(BF16), w_ffn1_out[0].astype(BF16), g_mix, w_t, bf,
               conv_w[0], w_out[0].astype(BF16), g_ffn2, w_ffn2_in[0].astype(BF16), w_ffn2_out[0].astype(BF16))
    gfin = g_final.reshape(1, D)

    zero_u = jnp.zeros((Bp, CONV_WIDTH - 1, C), F32)
    past = (jnp.swapaxes(cache_k[0], 2, 3), jnp.swapaxes(cache_v[0], 2, 3), cache_logf[0], state_conv[0])
    (yp, kp, vp, fp, up), (ys, ks, vs, fs, us) = _layer(x_prompt, x_sample, mod_p, mod_s, past, zero_u,
                                                          weights, gfin)
    return (yp, ys, kp[None], vp[None], fp[None], up[None], ks[None], vs[None], fs[None], us[None])
```

```python
import functools

import jax
import jax.numpy as jnp
from jax import lax
from jax.experimental import pallas as pl
from jax.experimental.pallas import tpu as pltpu

F32 = jnp.float32
BF16 = jnp.bfloat16

EPS = 1e-6
N_HEADS = 16
HEAD_DIM = 64
N_MOD = 9
MOD_FFN1, MOD_MIX, MOD_FFN2 = 0, 3, 6
CONV_WIDTH = 3
LANES = 128
HEADS_PER_BLOCK = LANES // HEAD_DIM
V7X_VMEM_LIMIT_BYTES = 56 * 1024 * 1024
ADA_STEPS = 3
FFN_CHUNK = 256
PROMPT_ROW_TILE = 1024
SUB_TILES = 2
SAMPLE_ROW_TILE = 512
ATTN_BLOCK = 256
ATTN_ROWS = 64
ATTN_PAIRS = 4
ATTN_SLOTS = 4
BIAS_ROWS = 16
LOG2E = 1.4426950408889634
Q_SCALE = HEAD_DIM ** -0.5 * LOG2E
SAMPLE_HEAD_GROUP = 16
SAMPLE_STACK = 4


def _params(semantics):
    return pltpu.CompilerParams(dimension_semantics=semantics, vmem_limit_bytes=V7X_VMEM_LIMIT_BYTES)


def _resident(shape):
    zeros = (0,) * len(shape)
    return pl.BlockSpec(shape, lambda *_: zeros, pipeline_mode=pl.Buffered(1))


def _resident_rows(w, row0, nrows):
    assert row0 + nrows <= w.shape[0]
    return pl.BlockSpec((pl.Element(nrows), pl.Element(w.shape[1])), lambda *_: (row0, 0),
                        pipeline_mode=pl.Buffered(1))


def _rms_mod(x, g, scale, shift):
    ms = jnp.mean(x * x, axis=-1, keepdims=True)
    y = x * lax.rsqrt(ms + EPS) * g
    return y * (1.0 + scale) + shift


def _norm_tile(x_ref, mod_ref, g_ref, shift_idx, h_scr):
    bb, tt, d = x_ref.shape
    shift = mod_ref[:, shift_idx:shift_idx + 1, :]
    scale = mod_ref[:, shift_idx + 1:shift_idx + 2, :]
    h_scr[0:bb * tt, :] = _rms_mod(x_ref[...], g_ref[...], scale, shift).reshape(bb * tt, d).astype(BF16)


def _dot_t(x, w_t):
    return lax.dot_general(x, w_t, (((1,), (1,)), ((), ())), preferred_element_type=F32)


def _sigmoid(x):
    return 1.0 / (1.0 + jnp.exp(-x))


def _log_sigmoid(x):
    return jnp.minimum(x, 0.0) - jnp.log1p(jnp.exp(-jnp.abs(x)))


def _cumsum_lanes(x):
    n = x.shape[-1]
    lane = lax.broadcasted_iota(jnp.int32, x.shape, x.ndim - 1)
    step = 1
    while step < n:
        x = x + jnp.where(lane >= step, pltpu.roll(x, step, axis=x.ndim - 1), 0.0)
        step *= 2
    return x


def _ada_kernel(c_ref, w_ref, b_ref, o_ref):
    c = c_ref[...]
    a = (c * _sigmoid(c)).astype(BF16)
    o_ref[...] = jnp.dot(a, w_ref[...].astype(BF16), preferred_element_type=F32) + b_ref[...]


def _ada(c, w_ada, b_ada):
    n, d = c.shape
    cols = w_ada.shape[1]
    tn = cols // ADA_STEPS
    assert cols % ADA_STEPS == 0 and tn % LANES == 0
    return pl.pallas_call(
        _ada_kernel,
        grid=(cols // tn,),
        in_specs=[
            pl.BlockSpec((n, d), lambda j: (0, 0)),
            pl.BlockSpec((d, tn), lambda j: (0, j)),
            pl.BlockSpec((1, tn), lambda j: (0, j)),
        ],
        out_specs=pl.BlockSpec((n, tn), lambda j: (0, j)),
        out_shape=jax.ShapeDtypeStruct((n, cols), F32),
        compiler_params=_params(("parallel",)),
        name="ada_mod",
    )(c, w_ada, b_ada.reshape(1, cols))


def _ffn_tile(x_ref, mod_ref, g_ref, w1_ref, w2_ref, gfin_ref, o_ref, h_scr, *, mod_base, final_norm):
    bb, tt, d = x_ref.shape
    tm = bb * tt
    f = w2_ref.shape[0]
    fc = FFN_CHUNK
    gate = mod_ref[:, mod_base + 2:mod_base + 3, :]
    _norm_tile(x_ref, mod_ref, g_ref, mod_base, h_scr)

    acc = None
    for lo in range(0, f, fc):
        h = h_scr[0:tm, :]
        a = jnp.dot(h, w1_ref[:, lo:lo + fc], preferred_element_type=F32)
        b = jnp.dot(h, w1_ref[:, f + lo:f + lo + fc], preferred_element_type=F32)
        act = (a * _sigmoid(a) * b).astype(BF16)
        part = jnp.dot(act, w2_ref[lo:lo + fc, :], preferred_element_type=F32)
        acc = part if acc is None else acc + part

    y = x_ref[...] + 0.5 * gate * acc.reshape(bb, tt, d)
    if final_norm:
        ms = jnp.mean(y * y, axis=-1, keepdims=True)
        y = y * lax.rsqrt(ms + EPS) * gfin_ref[...]
    o_ref[...] = y


def _ffn_kernel(xp_ref, modp_ref, xs_ref, mods_ref, g_ref, w1_ref, w2_ref, gfin_ref, op_ref, os_ref, h_scr, *,
                n_prompt, mod_base, final_norm):
    step = pl.program_id(0)
    tile = functools.partial(_ffn_tile, mod_base=mod_base, final_norm=final_norm)

    @pl.when(step < n_prompt)
    def _():
        tile(xp_ref, modp_ref, g_ref, w1_ref, w2_ref, gfin_ref, op_ref, h_scr)

    @pl.when(step >= n_prompt)
    def _():
        tile(xs_ref, mods_ref, g_ref, w1_ref, w2_ref, gfin_ref, os_ref, h_scr)


def _ffn(xp, modp, xs, mods, g, w1c, w2c, g_final, *, tt_p, bb_s, mod_base, final_norm):
    Bp, Tp, D = xp.shape
    Bs, Ts, _ = xs.shape
    assert w2c.shape[0] % FFN_CHUNK == 0 and Tp % tt_p == 0 and Bs % bb_s == 0
    per_b = Tp // tt_p
    n_prompt = Bp * per_b
    n_sample = Bs // bb_s
    p_tile = lambda i: jnp.minimum(i, n_prompt - 1)
    s_tile = lambda i: jnp.maximum(i - n_prompt, 0)
    kern = functools.partial(_ffn_kernel, n_prompt=n_prompt, mod_base=mod_base, final_norm=final_norm)
    return pl.pallas_call(
        kern,
        grid=(n_prompt + n_sample,),
        in_specs=[
            pl.BlockSpec((1, tt_p, D), lambda i: (p_tile(i) // per_b, p_tile(i) % per_b, 0)),
            pl.BlockSpec((1, N_MOD, D), lambda i: (p_tile(i) // per_b, 0, 0)),
            pl.BlockSpec((bb_s, Ts, D), lambda i: (s_tile(i), 0, 0)),
            pl.BlockSpec((bb_s, N_MOD, D), lambda i: (s_tile(i), 0, 0)),
            _resident((1, D)),
            _resident(w1c.shape),
            _resident(w2c.shape),
            _resident((1, D)),
        ],
        out_specs=[
            pl.BlockSpec((1, tt_p, D), lambda i: (p_tile(i) // per_b, p_tile(i) % per_b, 0)),
            pl.BlockSpec((bb_s, Ts, D), lambda i: (s_tile(i), 0, 0)),
        ],
        out_shape=[jax.ShapeDtypeStruct(xp.shape, F32), jax.ShapeDtypeStruct(xs.shape, F32)],
        scratch_shapes=[pltpu.VMEM((max(tt_p, bb_s * Ts), D), BF16)],
        compiler_params=_params(("arbitrary",)),
        name="ffn_final" if final_norm else "ffn",
    )(xp, modp, xs, mods, g, w1c, w2c, g_final)


def _qkv_kernel(*refs, prompt):
    if not prompt:
        (x_ref, mod_ref, g_ref, wqkv_ref, bf_ref, clf_ref,
         q_ref, kb_ref, vb_ref, k32_ref, v32_ref, lf_ref, fp_ref, fn_ref, h_scr) = refs
        nb, tt = x_ref.shape[0] // SUB_TILES, x_ref.shape[1]
        for s in range(SUB_TILES):
            bs = slice(s * nb, (s + 1) * nb)
            _qkv_tile(x_ref.at[bs], mod_ref.at[bs], g_ref, wqkv_ref, bf_ref, clf_ref.at[bs],
                      q_ref.at[bs], kb_ref.at[bs], vb_ref.at[bs], k32_ref.at[bs], v32_ref.at[bs], lf_ref.at[bs],
                      fp_ref.at[bs], fn_ref.at[bs], h_scr.at[s * nb * tt:(s + 1) * nb * tt, :], prompt=False)
        return
    (x_ref, mod_ref, g_ref, wqkv_ref, bf_ref, q_ref, kb_ref, vb_ref, k32_ref, v32_ref, lf_ref, h_scr) = refs
    rows = x_ref.shape[1] // SUB_TILES
    for s in range(SUB_TILES):
        ts = slice(s * rows, (s + 1) * rows)
        _qkv_tile(x_ref.at[:, ts, :], mod_ref, g_ref, wqkv_ref, bf_ref,
                  q_ref.at[:, ts, :], kb_ref.at[:, :, ts], vb_ref.at[:, ts, :], k32_ref.at[:, :, :, ts],
                  v32_ref.at[:, :, :, ts], lf_ref.at[:, :, ts], h_scr.at[ts, :], prompt=True)


def _qkv_tile(*refs, prompt):
    if prompt:
        (x_ref, mod_ref, g_ref, wqkv_ref, bf_ref,
         q_ref, kb_ref, vb_ref, k32_ref, v32_ref, lf_ref, h_scr) = refs
    else:
        (x_ref, mod_ref, g_ref, wqkv_ref, bf_ref, clf_ref,
         q_ref, kb_ref, vb_ref, k32_ref, v32_ref, lf_ref, fp_ref, fn_ref, h_scr) = refs
    bb, tt, d = x_ref.shape
    tm = bb * tt
    a_dim = (wqkv_ref.shape[0] - LANES) // 3
    _norm_tile(x_ref, mod_ref, g_ref, MOD_MIX, h_scr)
    h = h_scr[...]

    zq =_dot_t(h, wqkv_ref[0:a_dim, :])
    q_ref[...] = (zq * Q_SCALE).astype(BF16).reshape(bb, tt, a_dim)

    for w_idx, (lo_ref, hi_ref) in ((1, (kb_ref, k32_ref)), (2, (vb_ref, v32_ref))):
        z = _dot_t(h, wqkv_ref[w_idx * a_dim:(w_idx + 1) * a_dim, :])
        if prompt:
            zt = z.T
            hi_ref[0] = zt.reshape(N_HEADS, HEAD_DIM, tt)
            if w_idx == 1:
                lo_ref[0] = zt.astype(BF16)
            else:
                lo_ref[...] = z.astype(BF16).reshape(bb, tt, a_dim)
        else:
            lo_ref[...] = z.astype(BF16).reshape(bb, tt, a_dim)
            for hd in range(N_HEADS):
                hi_ref[:, hd, :, :] = z[:, hd * HEAD_DIM:(hd + 1) * HEAD_DIM].reshape(bb, tt, HEAD_DIM)

    zf = _dot_t(h, wqkv_ref[3 * a_dim:, :])
    lf = _log_sigmoid(zf + bf_ref[...]).T
    if not prompt:
        p_len = clf_ref.shape[2]
        f_past = LOG2E * _cumsum_lanes(clf_ref[...].reshape(bb * N_HEADS, p_len))
        fp_ref[...] = f_past.reshape(bb, N_HEADS, p_len)
    for b in range(bb):
        lf_b = lf[:N_HEADS, b * tt:(b + 1) * tt]
        lf_ref[b] = lf_b
        if not prompt:
            fn_ref[b] = f_past[b * N_HEADS:(b + 1) * N_HEADS, p_len - 1:p_len] + LOG2E * _cumsum_lanes(lf_b)


def _qkv(x, mod, g, w_t, bf, cache_logf, *, bb, tt):
    B, T, D = x.shape
    A = N_HEADS * HEAD_DIM
    prompt = cache_logf is None
    tok = lambda b, t: (b, t, 0)
    per_b = lambda b, t: (b, 0, 0)
    in_specs = [
        pl.BlockSpec((bb, tt, D), tok),
        pl.BlockSpec((bb, N_MOD, D), per_b),
        _resident((1, D)),
        _resident_rows(w_t, 0, 3 * A + LANES),
        _resident(bf.shape),
    ]
    tok_spec = pl.BlockSpec((bb, tt, A), tok)
    tok_shape = jax.ShapeDtypeStruct((B, T, A), BF16)
    lf_spec = pl.BlockSpec((bb, N_HEADS, tt), lambda b, t: (b, 0, t))
    lf_shape = jax.ShapeDtypeStruct((B, N_HEADS, T), F32)
    if prompt:
        assert bb == 1
        kv_spec = pl.BlockSpec((1, N_HEADS, HEAD_DIM, tt), lambda b, t: (b, 0, 0, t))
        kv_shape = jax.ShapeDtypeStruct((B, N_HEADS, HEAD_DIM, T), F32)
        out_specs = [tok_spec, pl.BlockSpec((1, A, tt), lambda b, t: (b, 0, t)), tok_spec, kv_spec, kv_spec, lf_spec]
        out_shape = [tok_shape, jax.ShapeDtypeStruct((B, A, T), BF16), tok_shape, kv_shape, kv_shape, lf_shape]
        args = (x, mod, g, w_t, bf)
    else:
        assert T == tt
        P = cache_logf.shape[2]
        kv_spec = pl.BlockSpec((bb, N_HEADS, tt, HEAD_DIM), lambda b, t: (b, 0, t, 0))
        kv_shape = jax.ShapeDtypeStruct((B, N_HEADS, T, HEAD_DIM), F32)
        in_specs.append(pl.BlockSpec((bb, N_HEADS, P), per_b))
        out_specs = [tok_spec, tok_spec, tok_spec, kv_spec, kv_spec, lf_spec,
                     pl.BlockSpec((bb, N_HEADS, P), per_b), lf_spec]
        out_shape = [tok_shape, tok_shape, tok_shape, kv_shape, kv_shape, lf_shape,
                     jax.ShapeDtypeStruct((B, N_HEADS, P), F32), lf_shape]
        args = (x, mod, g, w_t, bf, cache_logf)
    return pl.pallas_call(
        functools.partial(_qkv_kernel, prompt=prompt),
        grid=(B // bb, T // tt),
        in_specs=in_specs,
        out_specs=out_specs,
        out_shape=out_shape,
        scratch_shapes=[pltpu.VMEM((bb * tt, D), BF16)],
        compiler_params=_params(("parallel", "parallel")),
        name="qkv_proj",
    )(*args)


def _attn_rows(q2, k1_ref, v1_ref, s_scr, p_scr, nk, blk):
    rows = HEADS_PER_BLOCK * blk
    half = blk // 2
    nkeys = nk * blk
    keep = lax.broadcasted_iota(jnp.int32, (blk, blk), 1) <= lax.broadcasted_iota(jnp.int32, (blk, blk), 0)
    s = jnp.dot(q2, k1_ref[:, 0:nkeys], preferred_element_type=F32)
    for h in range(HEADS_PER_BLOCK):
        hs = slice(h * blk, (h + 1) * blk)
        if nk > 1:
            s_scr[hs, 0:nkeys - blk] = s[hs, 0:nkeys - blk]
        s_scr[hs, nkeys - blk:nkeys] = jnp.where(keep, s[hs, nkeys - blk:nkeys], -jnp.inf)

    for r0 in range(0, rows, ATTN_ROWS):
        rs = slice(r0, r0 + ATTN_ROWS)
        mx = None
        for kb in range(nk):
            t = s_scr[rs, kb * blk:(kb + 1) * blk]
            t = jnp.maximum(t[:, :half], t[:, half:])
            mx = t if mx is None else jnp.maximum(mx, t)
        mb = jnp.broadcast_to(jnp.max(mx, axis=-1, keepdims=True), (ATTN_ROWS, blk))
        for kb in range(nk):
            ks = slice(kb * blk, (kb + 1) * blk)
            p_scr[rs, ks] = jnp.exp2(s_scr[rs, ks] - mb).astype(BF16)

    acc = jnp.dot(p_scr[:, 0:nkeys], v1_ref[0:nkeys, :], preferred_element_type=F32)
    return acc[:, :LANES] * (1.0 / acc[:, LANES:])


def _attn_prompt_kernel(q_ref, kt_ref, v_ref, lf_ref, o_ref, k1_scr, v1_scr, s_scr, p_scr, *, blk):
    T = v_ref.shape[1]
    negf = -LOG2E * _cumsum_lanes(lf_ref[0, 0])
    hi = negf.astype(BF16).astype(F32)
    mid = (negf - hi).astype(BF16).astype(F32)
    lo = negf - hi - mid
    pieces = (hi, mid, lo)
    brow = lax.broadcasted_iota(jnp.int32, (BIAS_ROWS, T), 0)
    for pr in range(ATTN_PAIRS):
        v1_scr[pr, :, :LANES] = v_ref[0, :, pr * LANES:(pr + 1) * LANES]
        v1_scr[pr, :, LANES:] = jnp.ones((T, LANES), v1_scr.dtype)
        k1_scr[pr, 0:LANES, :] = kt_ref[0, pr * LANES:(pr + 1) * LANES, :]
        bias = jnp.zeros((BIAS_ROWS, T), F32)
        for h in range(HEADS_PER_BLOCK):
            hd = pr * HEADS_PER_BLOCK + h
            for j, piece in enumerate(pieces):
                bias = jnp.where(brow == len(pieces) * h + j, piece[hd:hd + 1, :], bias)
        k1_scr[pr, LANES:LANES + BIAS_ROWS, :] = bias.astype(BF16)
        k1_scr[pr, LANES + BIAS_ROWS:, :] = jnp.zeros((LANES - BIAS_ROWS, T), BF16)
    lane = lax.broadcasted_iota(jnp.int32, (blk, LANES), 1)
    sel = jnp.concatenate([jnp.where((lane >= len(pieces) * h) & (lane < len(pieces) * (h + 1)), 1.0, 0.0)
                           for h in range(HEADS_PER_BLOCK)], axis=0).astype(BF16)
    nq = T // blk
    order = list(range(0, nq, 2)) + list(range(nq - 1 - nq % 2, 0, -2))
    slot = 0
    for c in order:
        for pr in range(ATTN_PAIRS):
            lanes = slice(pr * LANES, (pr + 1) * LANES)
            q = q_ref[0, c * blk:(c + 1) * blk, lanes]
            zero = jnp.zeros_like(q)
            q2 = jnp.concatenate([jnp.where(lane < HEAD_DIM, q, zero), jnp.where(lane >= HEAD_DIM, q, zero)],
                                 axis=0)
            q2 = jnp.concatenate([q2, sel], axis=1)
            o = _attn_rows(q2, k1_scr.at[pr], v1_scr.at[pr],
                           s_scr.at[slot % ATTN_SLOTS], p_scr.at[slot % ATTN_SLOTS], c + 1, blk)
            o_ref[0, c * blk:(c + 1) * blk, lanes] = jnp.where(lane < HEAD_DIM, o[:blk], o[blk:]).astype(o_ref.dtype)
            slot += 1


def _attn_prompt(q, kt, v, logf, *, blk):
    B, T, A = q.shape
    width = ATTN_PAIRS * LANES
    ngrp = A // width
    heads = ATTN_PAIRS * HEADS_PER_BLOCK
    lf = logf.reshape(B, ngrp, heads, T)
    kern = functools.partial(_attn_prompt_kernel, blk=blk)
    grp = pl.BlockSpec((1, T, width), lambda b, p: (b, 0, p))
    grp_t = pl.BlockSpec((1, width, T), lambda b, p: (b, p, 0))
    return pl.pallas_call(
        kern,
        grid=(B, ngrp),
        in_specs=[grp, grp_t, grp, pl.BlockSpec((1, 1, heads, T), lambda b, p: (b, p, 0, 0))],
        out_specs=grp,
        out_shape=jax.ShapeDtypeStruct((B, T, A), BF16),
        scratch_shapes=[
            pltpu.VMEM((ATTN_PAIRS, 2 * LANES, T), BF16),
            pltpu.VMEM((ATTN_PAIRS, T, 2 * LANES), BF16),
            pltpu.VMEM((ATTN_SLOTS, HEADS_PER_BLOCK * blk, T), F32),
            pltpu.VMEM((ATTN_SLOTS, HEADS_PER_BLOCK * blk, T), BF16),
        ],
        compiler_params=_params(("parallel", "parallel")),
        name="attn_prompt",
    )(q, kt, v, lf)


def _attn_sample_kernel(q_ref, kn_ref, vn_ref, ckt_ref, cvt_ref, fp_ref, fn_ref, o_ref):
    tt = q_ref.shape[1]
    p_len = ckt_ref.shape[3]
    width = SAMPLE_STACK * HEAD_DIM
    nt = (((1,), (1,)), ((), ()))
    lane_head = lax.broadcasted_iota(jnp.int32, (tt, width), 1) // HEAD_DIM
    keep = lax.broadcasted_iota(jnp.int32, (tt, tt), 1) <= lax.broadcasted_iota(jnp.int32, (tt, tt), 0)
    for g in range(ckt_ref.shape[1] // SAMPLE_STACK):
        heads = range(g * SAMPLE_STACK, (g + 1) * SAMPLE_STACK)
        lanes = slice(g * width, (g + 1) * width)
        q = q_ref[0][:, lanes]
        zero = jnp.zeros_like(q)
        qs = jnp.concatenate([jnp.where(lane_head == i, q, zero) for i in range(SAMPLE_STACK)], axis=0)
        kt = ckt_ref[0, heads.start:heads.stop].reshape(width, p_len).astype(BF16)
        vt = cvt_ref[0, heads.start:heads.stop].reshape(width, p_len).astype(BF16)
        s_p = jnp.dot(qs, kt, preferred_element_type=F32)
        s_n = lax.dot_general(qs, kn_ref[0][:, lanes], nt, preferred_element_type=F32)
        s_p = jnp.concatenate([s_p[i * tt:(i + 1) * tt] - fp_ref[0, 0, hd:hd + 1, :]
                               for i, hd in enumerate(heads)], axis=0)
        s_n = jnp.concatenate([jnp.where(keep, s_n[i * tt:(i + 1) * tt] - fn_ref[0, 0, hd:hd + 1, :], -jnp.inf)
                               for i, hd in enumerate(heads)], axis=0)
        m = jnp.maximum(jnp.max(s_p, axis=-1, keepdims=True), jnp.max(s_n, axis=-1, keepdims=True))
        p_p = jnp.exp2(s_p - m)
        p_n = jnp.exp2(s_n - m)
        l = jnp.sum(p_p, axis=-1, keepdims=True) + jnp.sum(p_n, axis=-1, keepdims=True)
        o = (lax.dot_general(p_p.astype(BF16), vt, nt, preferred_element_type=F32)
             + jnp.dot(p_n.astype(BF16), vn_ref[0][:, lanes], preferred_element_type=F32)) * (1.0 / l)
        out = jnp.zeros((tt, width), F32)
        for i in range(SAMPLE_STACK):
            out = jnp.where(lane_head == i, o[i * tt:(i + 1) * tt], out)
        o_ref[0, :, lanes] = out.astype(o_ref.dtype)


def _attn_sample(q, k, v, cache_kt, cache_vt, f_past, f_new):
    B, T, A = q.shape
    P = cache_kt.shape[3]
    hg = SAMPLE_HEAD_GROUP
    ngrp = N_HEADS // hg
    clf = f_past.reshape(B, ngrp, hg, P)
    lf = f_new.reshape(B, ngrp, hg, T)
    tok = lambda b, p: (b, 0, p)
    grp = lambda b, p: (b, p, 0, 0)
    return pl.pallas_call(
        _attn_sample_kernel,
        grid=(B, ngrp),
        in_specs=[
            pl.BlockSpec((1, T, hg * HEAD_DIM), tok),
            pl.BlockSpec((1, T, hg * HEAD_DIM), tok),
            pl.BlockSpec((1, T, hg * HEAD_DIM), tok),
            pl.BlockSpec((1, hg, HEAD_DIM, P), grp),
            pl.BlockSpec((1, hg, HEAD_DIM, P), grp),
            pl.BlockSpec((1, 1, hg, P), grp),
            pl.BlockSpec((1, 1, hg, T), grp),
        ],
        out_specs=pl.BlockSpec((1, T, hg * HEAD_DIM), tok),
        out_shape=jax.ShapeDtypeStruct((B, T, A), BF16),
        compiler_params=_params(("parallel", "parallel")),
        name="attn_sample",
    )(q, k, v, cache_kt, cache_vt, clf, lf)


def _mix_kernel(x_ref, o_ref, mod_ref, g_ref, w5_ref, cw_ref, pu_ref, wo_ref, y_ref, nu_ref, carry_scr, h_scr):
    bb, tt, d = x_ref.shape
    tm = bb * tt

    @pl.when(pl.program_id(1) == 0)
    def _():
        carry_scr[...] = pu_ref[...]

    gate = mod_ref[:, MOD_MIX + 2:MOD_MIX + 3, :]
    _norm_tile(x_ref, mod_ref, g_ref, MOD_MIX, h_scr)
    h = h_scr[...]
    cdim = w5_ref.shape[0] // 5
    proj = lambda i: _dot_t(h, w5_ref[i * cdim:(i + 1) * cdim, :])

    u2 = proj(1) * proj(2)
    u = u2.reshape(bb, tt, cdim)
    tpos = lax.broadcasted_iota(jnp.int32, (1, tt, 1), 1)
    c0 = carry_scr[:, 0:1, :]
    c1 = carry_scr[:, 1:2, :]
    um1 = jnp.where(tpos == 0, c1, pltpu.roll(u2, 1, axis=0).reshape(bb, tt, cdim))
    um2 = jnp.where(tpos == 0, c0, jnp.where(tpos == 1, c1, pltpu.roll(u2, 2, axis=0).reshape(bb, tt, cdim)))
    conv = cw_ref[0:1, :] * um2 + cw_ref[1:2, :] * um1 + cw_ref[2:3, :] * u
    new_u = u[:, tt - (CONV_WIDTH - 1):, :]
    carry_scr[...] = new_u
    nu_ref[...] = new_u

    o_conv = proj(0) * conv.reshape(tm, cdim)
    m = _sigmoid(proj(3)) * o_ref[...].reshape(tm, cdim).astype(F32) + _sigmoid(proj(4)) * o_conv
    mo = jnp.dot(m.astype(BF16), wo_ref[...], preferred_element_type=F32)
    y_ref[...] = x_ref[...] + gate * mo.reshape(bb, tt, d)


def _mix(x, o_attn, mod, g, w_t, conv_w, past_u, w_out, *, bb, tt):
    B, T, D = x.shape
    C = conv_w.shape[1]
    tok = lambda b, t: (b, t, 0)
    per_b = lambda b, t: (b, 0, 0)
    return pl.pallas_call(
        _mix_kernel,
        grid=(B // bb, T // tt),
        in_specs=[
            pl.BlockSpec((bb, tt, D), tok),
            pl.BlockSpec((bb, tt, C), tok),
            pl.BlockSpec((bb, N_MOD, D), per_b),
            _resident((1, D)),
            _resident_rows(w_t, w_t.shape[0] - 5 * C, 5 * C),
            _resident(conv_w.shape),
            pl.BlockSpec((bb, CONV_WIDTH - 1, C), per_b),
            _resident(w_out.shape),
        ],
        out_specs=[
            pl.BlockSpec((bb, tt, D), tok),
            pl.BlockSpec((bb, CONV_WIDTH - 1, C), per_b),
        ],
        out_shape=[
            jax.ShapeDtypeStruct((B, T, D), F32),
            jax.ShapeDtypeStruct((B, CONV_WIDTH - 1, C), F32),
        ],
        scratch_shapes=[pltpu.VMEM((bb, CONV_WIDTH - 1, C), F32), pltpu.VMEM((bb * tt, D), BF16)],
        compiler_params=_params(("parallel", "arbitrary")),
        name="mix",
    )(x, o_attn, mod, g, w_t, conv_w, past_u, w_out)


def _layer(xp, xs, mod_p, mod_s, past_s, past_u_p, weights, g_final):
    (g1, w1a, w1b, gm, w_t, bf, conv_w, w_out, g2, w2a, w2b) = weights
    past_kt, past_vt, past_logf, past_u_s = past_s
    Ts = xs.shape[1]
    tt_p, bb_s = PROMPT_ROW_TILE, SAMPLE_ROW_TILE // Ts
    ffn_tiles = dict(tt_p=tt_p, bb_s=bb_s)
    xp1, xs1 = _ffn(xp, mod_p, xs, mod_s, g1, w1a, w1b, g_final, mod_base=MOD_FFN1, final_norm=False, **ffn_tiles)

    q, kbt, vb, kp, vp, fp = _qkv(xp1, mod_p, gm, w_t, bf, None, bb=1, tt=tt_p)
    op = _attn_prompt(q, kbt, vb, fp, blk=ATTN_BLOCK)
    kp, vp = jnp.swapaxes(kp, 2, 3), jnp.swapaxes(vp, 2, 3)
    q, kb, vb, ks, vs, fs, f_past, f_new = _qkv(xs1, mod_s, gm, w_t, bf, past_logf, bb=bb_s, tt=Ts)
    os_ = _attn_sample(q, kb, vb, past_kt, past_vt, f_past, f_new)

    xp2, up = _mix(xp1, op, mod_p, gm, w_t, conv_w, past_u_p, w_out, bb=1, tt=tt_p)
    xs2, us = _mix(xs1, os_, mod_s, gm, w_t, conv_w, past_u_s, w_out, bb=bb_s, tt=Ts)
    yp, ys = _ffn(xp2, mod_p, xs2, mod_s, g2, w2a, w2b, g_final, mod_base=MOD_FFN2, final_norm=True, **ffn_tiles)
    return (yp, kp, vp, fp, up), (ys, ks, vs, fs, us)


def kernel(x_prompt, x_sample, cache_k, cache_v, cache_logf, state_conv, c_prompt, c_sample, w_ada, b_ada, g_ffn1, w_ffn1_in, w_ffn1_out, g_mix, w_in, b_f, conv_w, w_out, g_ffn2, w_ffn2_in, w_ffn2_out, g_final):
    assert w_ada.shape[0] == 1, "single-layer encoder"
    Bp, Tp, D = x_prompt.shape
    Bs, Ts, _ = x_sample.shape
    A = N_HEADS * HEAD_DIM
    C = conv_w.shape[2]

    mod = _ada(jnp.concatenate([c_prompt, c_sample], axis=0), w_ada[0], b_ada[0])
    mod = mod.reshape(Bp + Bs, N_MOD, D)
    mod_p, mod_s = mod[:Bp], mod[Bp:]

    assert w_in.shape[2] == 3 * A + N_HEADS + 5 * C
    w_t = jnp.swapaxes(w_in[0], 0, 1).astype(BF16)
    bf = jnp.pad(b_f[0], (0, LANES - N_HEADS)).reshape(1, LANES)
    weights = (g_ffn1, w_ffn1_in[0].astype(BF16), w_ffn1_out[0].astype(BF16), g_mix, w_t, bf,
               conv_w[0], w_out[0].astype(BF16), g_ffn2, w_ffn2_in[0].astype(BF16), w_ffn2_out[0].astype(BF16))
    gfin = g_final.reshape(1, D)

    zero_u = jnp.zeros((Bp, CONV_WIDTH - 1, C), F32)
    past = (jnp.swapaxes(cache_k[0], 2, 3), jnp.swapaxes(cache_v[0], 2, 3), cache_logf[0], state_conv[0])
    (yp, kp, vp, fp, up), (ys, ks, vs, fs, us) = _layer(x_prompt, x_sample, mod_p, mod_s, past, zero_u,
                                                          weights, gfin)
    return (yp, ys, kp[None], vp[None], fp[None], up[None], ks[None], vs[None], fs[None], us[None])
```
